```python
import math
import jax, jax.numpy as jnp
from jax import lax
import numpy as np

D_MODEL = 2048
BATCH = 1
SEQ = 8192
DEPTH = 2

CHUNK = 64
HEAD_DIM = 128
D_FF = ((8 * D_MODEL // 3 + 127) // 128) * 128
N_SUB = 3
EPS = 1e-6
D_A = D_MODEL // 2
S5_GROUP = 16
S5_GROUPS = D_A // S5_GROUP
S5_STATE = 64
DT_MIN = 1e-3
DT_MAX = 1e-1
D_B = D_MODEL // 2
B_HEADS = D_B // HEAD_DIM
KV_RANK = D_MODEL // 8
IDX_HEADS = 16
IDX_DIM = 64
TOPK_MAX = 256
Q_BLOCK = 128
AB_SPLITS = (D_A, D_A + D_B, D_A + D_B + KV_RANK, D_A + D_B + KV_RANK + IDX_HEADS * IDX_DIM, D_A + D_B + KV_RANK + IDX_HEADS * IDX_DIM + IDX_DIM)
D_IN_AB = AB_SPLITS[-1] + IDX_HEADS
C_HEADS = D_MODEL // HEAD_DIM
C_LEFT_CHUNKS = 8
BAND = (C_LEFT_CHUNKS + 1) * CHUNK
MAX_REL = 256
N_EVEN = (DEPTH + 1) // 2
N_ODD = DEPTH // 2

kernel_name = 'hybrid_s5_dsa_chunkattn_macaron'


def rmsnorm(x, g):
    xf = x.astype(jnp.float32)
    y = xf * lax.rsqrt(jnp.mean(xf * xf, axis=-1, keepdims=True) + EPS)
    return (y * g.astype(jnp.float32)).astype(x.dtype)


def adaln(x, g, shift, scale):
    return rmsnorm(x, g) * (1.0 + scale[:, None, :]) + shift[:, None, :]


def swiglu(h, w_gate, w_up, w_down):
    return (jax.nn.silu(h @ w_gate) * (h @ w_up)) @ w_down


def alibi_slopes(n_heads):
    return jnp.asarray(2.0 ** (-8.0 * (np.arange(n_heads) + 1) / n_heads), dtype=jnp.float32)


def s5_mixer(u, lam_re, lam_im, log_dt, b_re, b_im, c_re, c_im, d_skip, w_glu, b_glu):
    f32 = jnp.float32
    bsz, L, _ = u.shape
    uf = u.astype(f32)
    ug = uf.reshape(bsz, L, S5_GROUPS, S5_GROUP)
    lr = lam_re.astype(f32)
    li = lam_im.astype(f32)
    dt = jnp.exp(log_dt.astype(f32))[:, None]
    mag = jnp.exp(lr * dt)
    ab_re = mag * jnp.cos(li * dt)
    ab_im = mag * jnp.sin(li * dt)
    den = lr * lr + li * li
    nr = ab_re - 1.0
    f_re = (nr * lr + ab_im * li) / den
    f_im = (ab_im * lr - nr * li) / den
    br = b_re.astype(f32)
    bi = b_im.astype(f32)
    bb_re = f_re[..., None] * br - f_im[..., None] * bi
    bb_im = f_re[..., None] * bi + f_im[..., None] * br
    bu_re = jnp.einsum('blgc,gpc->blgp', ug, bb_re)
    bu_im = jnp.einsum('blgc,gpc->blgp', ug, bb_im)
    a_re = jnp.broadcast_to(ab_re, bu_re.shape)
    a_im = jnp.broadcast_to(ab_im, bu_im.shape)

    def combine(e1, e2):
        a1r, a1i, b1r, b1i = e1
        a2r, a2i, b2r, b2i = e2
        return (a2r * a1r - a2i * a1i, a2r * a1i + a2i * a1r,
                a2r * b1r - a2i * b1i + b2r, a2r * b1i + a2i * b1r + b2i)

    _, _, xr, xi = lax.associative_scan(combine, (a_re, a_im, bu_re, bu_im), axis=1)
    y = jnp.einsum('blgp,gcp->blgc', xr, c_re.astype(f32)) - jnp.einsum('blgp,gcp->blgc', xi, c_im.astype(f32))
    y = y.reshape(bsz, L, D_A) + d_skip.astype(f32) * uf
    y = jax.nn.gelu(y.astype(u.dtype))
    return y * jax.nn.sigmoid(y @ w_glu + b_glu)


def dsa_mixer(q, kv_lat, q_idx, k_idx, w_idx, kv_norm_g, w_kv_up):
    f32 = jnp.float32
    bsz, L, _ = q.shape
    topk = min(TOPK_MAX, L // 4)
    kv = rmsnorm(kv_lat, kv_norm_g) @ w_kv_up
    k, v = jnp.split(kv, 2, axis=-1)
    k = k.reshape(bsz, L, B_HEADS, HEAD_DIM)
    v = v.reshape(bsz, L, B_HEADS, HEAD_DIM)
    q = q.reshape(bsz, L, B_HEADS, HEAD_DIM)
    q_idx = q_idx.reshape(bsz, L, IDX_HEADS, IDX_DIM)
    n_blk = L // Q_BLOCK
    key_chunk = jnp.arange(L) // CHUNK
    slopes = alibi_slopes(B_HEADS)
    k_idx32 = k_idx.astype(f32)

    def to_blocks(a):
        return jnp.moveaxis(a.reshape(bsz, n_blk, Q_BLOCK, *a.shape[2:]), 1, 0)

    def block(args):
        blk, qb, qib, wb = args
        t = blk * Q_BLOCK + jnp.arange(Q_BLOCK)
        s_h = jnp.einsum('bqhd,bsd->bqhs', qib.astype(f32), k_idx32) * (IDX_DIM ** -0.5)
        score = jnp.einsum('bqhs,bqh->bqs', jax.nn.relu(s_h), wb.astype(f32)) * (IDX_HEADS ** -0.5)
        admissible = key_chunk[None, :] <= (t // CHUNK)[:, None]
        score = jnp.where(admissible[None], score, -jnp.inf)
        _, sel = lax.top_k(score, topk)
        k_sel = jax.vmap(lambda kk, ii: kk[ii])(k, sel)
        v_sel = jax.vmap(lambda vv, ii: vv[ii])(v, sel)
        logits = jnp.einsum('bqhd,bqkhd->bhqk', qb, k_sel).astype(f32) * (HEAD_DIM ** -0.5)
        dist = jnp.abs(t[None, :, None] - sel).astype(f32)
        logits = logits - slopes[None, :, None, None] * dist[:, None]
        valid = (sel // CHUNK) <= (t // CHUNK)[None, :, None]
        logits = jnp.where(valid[:, None], logits, -jnp.inf)
        p = jax.nn.softmax(logits, axis=-1).astype(v.dtype)
        return jnp.einsum('bhqk,bqkhd->bqhd', p, v_sel)

    out = lax.map(block, (jnp.arange(n_blk), to_blocks(q), to_blocks(q_idx), to_blocks(w_idx)))
    return jnp.moveaxis(out, 0, 1).reshape(bsz, L, D_B)


def ab_mixer(h, w_in, lam_re, lam_im, log_dt, b_re, b_im, c_re, c_im, d_skip, w_glu, b_glu, kv_norm_g, w_kv_up, w_out):
    proj = h @ w_in
    u, q, kv_lat, q_idx, k_idx, w_idx = jnp.split(proj, list(AB_SPLITS), axis=-1)
    y_a = s5_mixer(u, lam_re, lam_im, log_dt, b_re, b_im, c_re, c_im, d_skip, w_glu, b_glu)
    y_b = dsa_mixer(q, kv_lat, q_idx, k_idx, w_idx, kv_norm_g, w_kv_up)
    return jnp.concatenate([y_a, y_b], axis=-1) @ w_out


def chunked_relpos_attention(h, w_qkv, rel_bias, w_out):
    f32 = jnp.float32
    bsz, L, _ = h.shape
    n_chunks = L // CHUNK
    pad = C_LEFT_CHUNKS * CHUNK
    q, k, v = jnp.split(h @ w_qkv, 3, axis=-1)
    q = q.reshape(bsz, L, C_HEADS, HEAD_DIM)
    k = jnp.pad(k.reshape(bsz, L, C_HEADS, HEAD_DIM), ((0, 0), (pad, 0), (0, 0), (0, 0)))
    v = jnp.pad(v.reshape(bsz, L, C_HEADS, HEAD_DIM), ((0, 0), (pad, 0), (0, 0), (0, 0)))
    i = jnp.arange(CHUNK)
    j = jnp.arange(BAND)
    rel = i[:, None] - j[None, :] + pad
    bias = rel_bias.astype(f32)[:, jnp.clip(rel, -MAX_REL, MAX_REL) + MAX_REL]
    q_chunks = jnp.moveaxis(q.reshape(bsz, n_chunks, CHUNK, C_HEADS, HEAD_DIM), 1, 0)

    def chunk_fn(args):
        cidx, qc = args
        kb = lax.dynamic_slice_in_dim(k, cidx * CHUNK, BAND, axis=1)
        vb = lax.dynamic_slice_in_dim(v, cidx * CHUNK, BAND, axis=1)
        logits = jnp.einsum('bqhd,bkhd->bhqk', qc, kb).astype(f32) * (HEAD_DIM ** -0.5) + bias[None]
        valid = (cidx * CHUNK - pad + j) >= 0
        logits = jnp.where(valid[None, None, None, :], logits, -jnp.inf)
        p = jax.nn.softmax(logits, axis=-1).astype(vb.dtype)
        return jnp.einsum('bhqk,bkhd->bqhd', p, vb)

    out = lax.map(chunk_fn, (jnp.arange(n_chunks), q_chunks))
    return jnp.moveaxis(out, 0, 1).reshape(bsz, L, D_MODEL) @ w_out


def setup_inputs(seed: int = 0) -> dict:
    key = jax.random.key(seed)
    ks = jax.random.split(key, 32)
    f32 = jnp.float32
    D = D_MODEL
    G, P, CG = S5_GROUPS, S5_STATE, S5_GROUP

    def nrm(k, shape, scale):
        return jax.random.normal(k, shape, f32) * scale

    return {
        'x': nrm(ks[0], (BATCH, SEQ, D), 1.0),
        'c': nrm(ks[1], (BATCH, D), 1.0),
        'ada_w': nrm(ks[2], (DEPTH, D, N_SUB * 3 * D), 0.5 * D ** -0.5),
        'ada_b': nrm(ks[3], (DEPTH, N_SUB * 3 * D), 0.02),
        'norm_g': 1.0 + nrm(ks[4], (DEPTH, N_SUB, D), 0.02),
        'ffn_w_gate': nrm(ks[5], (DEPTH, 2, D, D_FF), D ** -0.5),
        'ffn_w_up': nrm(ks[6], (DEPTH, 2, D, D_FF), D ** -0.5),
        'ffn_w_down': nrm(ks[7], (DEPTH, 2, D_FF, D), D_FF ** -0.5),
        'ab_w_in': nrm(ks[8], (N_EVEN, D, D_IN_AB), D ** -0.5),
        's5_lam_re': -0.5 + nrm(ks[9], (N_EVEN, G, P), 0.01),
        's5_lam_im': jnp.pi * jnp.arange(P, dtype=f32) + nrm(ks[10], (N_EVEN, G, P), 0.01),
        's5_log_dt': jax.random.uniform(ks[11], (N_EVEN, G), f32, minval=math.log(DT_MIN), maxval=math.log(DT_MAX)),
        's5_b_re': nrm(ks[12], (N_EVEN, G, P, CG), (2 * CG) ** -0.5),
        's5_b_im': nrm(ks[13], (N_EVEN, G, P, CG), (2 * CG) ** -0.5),
        's5_c_re': nrm(ks[14], (N_EVEN, G, CG, P), (2 * P) ** -0.5),
        's5_c_im': nrm(ks[15], (N_EVEN, G, CG, P), (2 * P) ** -0.5),
        's5_d': nrm(ks[16], (N_EVEN, D_A), 0.5),
        's5_w_glu': nrm(ks[17], (N_EVEN, D_A, D_A), D_A ** -0.5),
        's5_b_glu': nrm(ks[18], (N_EVEN, D_A), 0.02),
        'dsa_kv_norm_g': 1.0 + nrm(ks[19], (N_EVEN, KV_RANK), 0.02),
        'dsa_w_kv_up': nrm(ks[20], (N_EVEN, KV_RANK, 2 * D_B), KV_RANK ** -0.5),
        'ab_w_out': nrm(ks[21], (N_EVEN, D_A + D_B, D), (D_A + D_B) ** -0.5),
        'c_w_qkv': nrm(ks[22], (N_ODD, D, 3 * D), D ** -0.5),
        'c_rel_bias': nrm(ks[23], (N_ODD, C_HEADS, 2 * MAX_REL + 1), 0.5),
        'c_w_out': nrm(ks[24], (N_ODD, D, D), D ** -0.5),
        'final_norm_g': 1.0 + nrm(ks[25], (D,), 0.02),
    }


def reference(x, c, ada_w, ada_b, norm_g, ffn_w_gate, ffn_w_up, ffn_w_down,
              ab_w_in, s5_lam_re, s5_lam_im, s5_log_dt, s5_b_re, s5_b_im, s5_c_re, s5_c_im,
              s5_d, s5_w_glu, s5_b_glu, dsa_kv_norm_g, dsa_w_kv_up, ab_w_out,
              c_w_qkv, c_rel_bias, c_w_out, final_norm_g):
    bsz = x.shape[0]
    cond = jax.nn.silu(c)
    h = x
    for layer in range(DEPTH):
        mod = (cond @ ada_w[layer] + ada_b[layer]).reshape(bsz, N_SUB, 3, D_MODEL)
        shift, scale, gate = mod[:, :, 0], mod[:, :, 1], mod[:, :, 2]
        y = swiglu(adaln(h, norm_g[layer, 0], shift[:, 0], scale[:, 0]),
                   ffn_w_gate[layer, 0], ffn_w_up[layer, 0], ffn_w_down[layer, 0])
        h = h + 0.5 * gate[:, 0, None, :] * y
        hn = adaln(h, norm_g[layer, 1], shift[:, 1], scale[:, 1])
        if layer % 2 == 0:
            e = layer // 2
            y = ab_mixer(hn, ab_w_in[e], s5_lam_re[e], s5_lam_im[e], s5_log_dt[e], s5_b_re[e], s5_b_im[e],
                         s5_c_re[e], s5_c_im[e], s5_d[e], s5_w_glu[e], s5_b_glu[e],
                         dsa_kv_norm_g[e], dsa_w_kv_up[e], ab_w_out[e])
        else:
            o = layer // 2
            y = chunked_relpos_attention(hn, c_w_qkv[o], c_rel_bias[o], c_w_out[o])
        h = h + gate[:, 1, None, :] * y
        y = swiglu(adaln(h, norm_g[layer, 2], shift[:, 2], scale[:, 2]),
                   ffn_w_gate[layer, 1], ffn_w_up[layer, 1], ffn_w_down[layer, 1])
        h = h + 0.5 * gate[:, 2, None, :] * y
    return rmsnorm(h, final_norm_g)
```

```python
import functools
import math

import jax
import jax.numpy as jnp
import numpy as np
from jax import lax
from jax.experimental import pallas as pl
from jax.experimental.pallas import tpu as pltpu

F32 = jnp.float32
BF16 = jnp.bfloat16

D_MODEL = 2048
SEQ = 8192
DEPTH = 2
CHUNK = 64
HEAD_DIM = 128
D_FF = 5504
N_SUB = 3
EPS = 1e-6
D_A = D_MODEL // 2
S5_GROUP = 16
S5_GROUPS = D_A // S5_GROUP
S5_STATE = 64
D_B = D_MODEL // 2
B_HEADS = D_B // HEAD_DIM
KV_RANK = D_MODEL // 8
IDX_HEADS = 16
IDX_DIM = 64
TOPK = 256
C_HEADS = D_MODEL // HEAD_DIM
C_LEFT_CHUNKS = 8
MAX_REL = 256

LANES = 128
SUBLANES = 8
VMEM_LIMIT = 56 * 1024 * 1024
NEG_BIG = -1e30
F32_MAX = float(np.finfo(np.float32).max)

D_FF_PAD = 5632
FFN_TM = 512
FFN_TF = 512
PROJ_TM = 1024
PROJ_TN = 512
OUT_TM = 512
AB_N_PAD = 3584
S5_T = 256
S5_SLICES = D_A // LANES
S5_NSTATE = S5_GROUPS * S5_STATE
DSA_TQ = 256
DSA_KT = 1024
DSA_SUB = 256
CA_TQ = 256
CA_HB = 4
CA_NKB = 3


def _cparams(sem):
    return pltpu.CompilerParams(dimension_semantics=sem, vmem_limit_bytes=VMEM_LIMIT)


def _mod_kernel(c_ref, w_ref, b_ref, o_ref):
    rows = 256
    tn = o_ref.shape[-1]

    def body(i, acc):
        r0 = pl.multiple_of(i * rows, rows)
        cc = c_ref[pl.ds(r0, rows), :]
        cc = cc * jax.nn.sigmoid(cc)
        w = w_ref[0, pl.ds(r0, rows), :]
        return acc + jnp.sum((w * cc).reshape(rows // SUBLANES, SUBLANES, tn), axis=0)

    acc = lax.fori_loop(0, D_MODEL // rows, body, jnp.zeros((SUBLANES, tn), F32))
    o_ref[0] = jnp.sum(acc, axis=0, keepdims=True) + b_ref[0]


def _modulation(c, ada_w, ada_b):
    n = N_SUB * 3 * D_MODEL
    tn = 1024
    c_col = c.reshape(D_MODEL, 1)
    out = pl.pallas_call(
        _mod_kernel,
        grid=(DEPTH, n // tn),
        in_specs=[
            pl.BlockSpec((D_MODEL, 1), lambda l, j: (0, 0)),
            pl.BlockSpec((1, D_MODEL, tn), lambda l, j: (l, 0, j)),
            pl.BlockSpec((1, 1, tn), lambda l, j: (l, 0, j)),
        ],
        out_specs=pl.BlockSpec((1, 1, tn), lambda l, j: (l, 0, j)),
        out_shape=jax.ShapeDtypeStruct((DEPTH, 1, n), F32),
        compiler_params=_cparams(("arbitrary", "arbitrary")),
        name="adaln_mod",
    )(c_col, ada_w, ada_b.reshape(DEPTH, 1, n))
    return out.reshape(DEPTH, N_SUB, 3, D_MODEL)


def _adaln(x, g, mod):
    ms = jnp.mean(x * x, axis=-1, keepdims=True)
    y = (x * lax.rsqrt(ms + EPS)) * g
    return y * (1.0 + mod[1:2, :]) + mod[0:1, :]


def _ffn_kernel(h_ref, g_ref, mod_ref, wg_ref, wu_ref, wd_ref, *rest, n_f, final):
    if final:
        fg_ref, o_ref, hn_ref, acc_ref = rest
    else:
        o_ref, hn_ref, acc_ref = rest
    f = pl.program_id(1)

    @pl.when(f == 0)
    def _():
        hn_ref[...] = _adaln(h_ref[...], g_ref[...], mod_ref[...]).astype(BF16)
        acc_ref[...] = jnp.zeros_like(acc_ref)

    hn = hn_ref[...]
    gate = jnp.dot(hn, wg_ref[...], preferred_element_type=F32)
    up = jnp.dot(hn, wu_ref[...], preferred_element_type=F32)
    act = (gate * jax.nn.sigmoid(gate)) * up
    acc_ref[...] += jnp.dot(act.astype(BF16), wd_ref[...], preferred_element_type=F32)

    @pl.when(f == n_f - 1)
    def _():
        out = h_ref[...] + (0.5 * mod_ref[2:3, :]) * acc_ref[...]
        if final:
            ms = jnp.mean(out * out, axis=-1, keepdims=True)
            out = (out * lax.rsqrt(ms + EPS)) * fg_ref[...]
        o_ref[...] = out


def _ffn(h, g, mod, wg, wu, wd, final_g=None):
    n_f = D_FF_PAD // FFN_TF
    final = final_g is not None
    in_specs = [
        pl.BlockSpec((FFN_TM, D_MODEL), lambda i, f: (i, 0)),
        pl.BlockSpec((1, D_MODEL), lambda i, f: (0, 0)),
        pl.BlockSpec((3, D_MODEL), lambda i, f: (0, 0)),
        pl.BlockSpec((D_MODEL, FFN_TF), lambda i, f: (0, f)),
        pl.BlockSpec((D_MODEL, FFN_TF), lambda i, f: (0, f)),
        pl.BlockSpec((FFN_TF, D_MODEL), lambda i, f: (f, 0)),
    ]
    args = [h, g.reshape(1, D_MODEL), mod, wg, wu, wd]
    if final:
        in_specs.append(pl.BlockSpec((1, D_MODEL), lambda i, f: (0, 0)))
        args.append(final_g.reshape(1, D_MODEL))
    return pl.pallas_call(
        functools.partial(_ffn_kernel, n_f=n_f, final=final),
        grid=(SEQ // FFN_TM, n_f),
        in_specs=in_specs,
        out_specs=pl.BlockSpec((FFN_TM, D_MODEL), lambda i, f: (i, 0)),
        out_shape=jax.ShapeDtypeStruct((SEQ, D_MODEL), F32),
        scratch_shapes=[pltpu.VMEM((FFN_TM, D_MODEL), BF16), pltpu.VMEM((FFN_TM, D_MODEL), F32)],
        compiler_params=_cparams(("arbitrary", "arbitrary")),
        name="ffn_swiglu",
    )(*args)


def _proj_kernel(h_ref, g_ref, mod_ref, w_ref, o_ref, hn_ref):
    @pl.when(pl.program_id(1) == 0)
    def _():
        hn_ref[...] = _adaln(h_ref[...], g_ref[...], mod_ref[...]).astype(BF16)

    o_ref[...] = jnp.dot(hn_ref[...], w_ref[...], preferred_element_type=F32).astype(o_ref.dtype)


def _norm_proj(h, g, mod, w, out_dtype):
    n = w.shape[1]
    return pl.pallas_call(
        _proj_kernel,
        grid=(SEQ // PROJ_TM, n // PROJ_TN),
        in_specs=[
            pl.BlockSpec((PROJ_TM, D_MODEL), lambda i, j: (i, 0)),
            pl.BlockSpec((1, D_MODEL), lambda i, j: (0, 0)),
            pl.BlockSpec((3, D_MODEL), lambda i, j: (0, 0)),
            pl.BlockSpec((D_MODEL, PROJ_TN), lambda i, j: (0, j)),
        ],
        out_specs=pl.BlockSpec((PROJ_TM, PROJ_TN), lambda i, j: (i, j)),
        out_shape=jax.ShapeDtypeStruct((SEQ, n), out_dtype),
        scratch_shapes=[pltpu.VMEM((PROJ_TM, D_MODEL), BF16)],
        compiler_params=_cparams(("arbitrary", "arbitrary")),
        name="adaln_proj",
    )(h, g.reshape(1, D_MODEL), mod, w)


def _out_kernel(*refs, n_lhs):
    lhs = refs[:n_lhs]
    ws = refs[n_lhs:2 * n_lhs]
    h_ref, mod_ref, o_ref = refs[2 * n_lhs:]
    y = jnp.dot(lhs[0][...], ws[0][...], preferred_element_type=F32)
    for a_ref, w_ref in zip(lhs[1:], ws[1:]):
        y += jnp.dot(a_ref[...], w_ref[...], preferred_element_type=F32)
    o_ref[...] = h_ref[...] + mod_ref[2:3, :] * y


def _out_proj(lhs, ws, h, mod):
    n_lhs = len(lhs)
    in_specs = [pl.BlockSpec((OUT_TM, a.shape[1]), lambda i: (i, 0)) for a in lhs]
    in_specs += [pl.BlockSpec(w.shape, lambda i: (0, 0)) for w in ws]
    in_specs += [pl.BlockSpec((OUT_TM, D_MODEL), lambda i: (i, 0)), pl.BlockSpec((3, D_MODEL), lambda i: (0, 0))]
    return pl.pallas_call(
        functools.partial(_out_kernel, n_lhs=n_lhs),
        grid=(SEQ // OUT_TM,),
        in_specs=in_specs,
        out_specs=pl.BlockSpec((OUT_TM, D_MODEL), lambda i: (i, 0)),
        out_shape=jax.ShapeDtypeStruct((SEQ, D_MODEL), F32),
        compiler_params=_cparams(("arbitrary",)),
        name="out_proj_residual",
    )(*lhs, *ws, h, mod)


def _s5_param_kernel(lr_ref, li_ref, ldt_ref, br_ref, bi_ref, are_ref, aim_ref, bbr_ref, bbi_ref):
    lr = lr_ref[...]
    li = li_ref[...]
    dt = jnp.exp(ldt_ref[...])
    mag = jnp.exp(lr * dt)
    ab_re = mag * jnp.cos(li * dt)
    ab_im = mag * jnp.sin(li * dt)
    den = lr * lr + li * li
    nr = ab_re - 1.0
    f_re = (nr * lr + ab_im * li) / den
    f_im = (ab_im * lr - nr * li) / den
    are_ref[...] = ab_re
    aim_ref[...] = ab_im
    br = br_ref[...]
    bi = bi_ref[...]
    bbr_ref[...] = f_re * br - f_im * bi
    bbi_ref[...] = f_re * bi + f_im * br


def _s5_params(lam_re, lam_im, log_dt, b_re, b_im):
    G, P, CG = S5_GROUPS, S5_STATE, S5_GROUP
    return pl.pallas_call(
        _s5_param_kernel,
        out_shape=[jax.ShapeDtypeStruct((G, 1, P), F32), jax.ShapeDtypeStruct((G, 1, P), F32),
                   jax.ShapeDtypeStruct((G, CG, P), F32), jax.ShapeDtypeStruct((G, CG, P), F32)],
        name="s5_zoh_params",
    )(lam_re.reshape(G, 1, P), lam_im.reshape(G, 1, P), log_dt.reshape(G, 1, 1),
      jnp.swapaxes(b_re, 1, 2), jnp.swapaxes(b_im, 1, 2))


def _block_diag_slices(m):
    _, r, c = m.shape
    m4 = m.reshape(S5_SLICES, SUBLANES, r, c)
    eye = jnp.eye(SUBLANES, dtype=m.dtype)
    out = m4[:, :, :, None, :] * eye[None, :, None, :, None]
    return out.reshape(S5_SLICES, SUBLANES * r, SUBLANES * c)


def _s5_kernel(u_ref, bbr_ref, bbi_ref, are_ref, aim_ref, ccr_ref, cci_ref, d_ref, wglu_ref, bglu_ref,
               o_ref, xr_ref, xi_ref, sr_ref, si_ref):
    T = S5_T
    W = S5_NSTATE // S5_SLICES

    @pl.when(pl.program_id(0) == 0)
    def _():
        sr_ref[...] = jnp.zeros_like(sr_ref)
        si_ref[...] = jnp.zeros_like(si_ref)

    for k in range(S5_SLICES):
        uk = u_ref[:, k * LANES:(k + 1) * LANES].astype(BF16)
        xr_ref[:, k * W:(k + 1) * W] = jnp.dot(uk, bbr_ref[k], preferred_element_type=F32)
        xi_ref[:, k * W:(k + 1) * W] = jnp.dot(uk, bbi_ref[k], preferred_element_type=F32)

    CW = 1024
    for cg in range(S5_NSTATE // CW):
        cols = slice(cg * CW, (cg + 1) * CW)
        ar = are_ref[:, cols]
        ai = aim_ref[:, cols]

        def step(t, carry, cols=cols, ar=ar, ai=ai):
            sr, si = carry
            nr = ar * sr - ai * si + xr_ref[pl.ds(t, 1), cols]
            ni = ar * si + ai * sr + xi_ref[pl.ds(t, 1), cols]
            xr_ref[pl.ds(t, 1), cols] = nr
            xi_ref[pl.ds(t, 1), cols] = ni
            return nr, ni

        sr, si = lax.fori_loop(0, T, step, (sr_ref[:, cols], si_ref[:, cols]), unroll=4)
        sr_ref[:, cols] = sr
        si_ref[:, cols] = si

    ys = []
    for k in range(S5_SLICES):
        xr = xr_ref[:, k * W:(k + 1) * W].astype(BF16)
        xi = xi_ref[:, k * W:(k + 1) * W].astype(BF16)
        ys.append(jnp.dot(xr, ccr_ref[k], preferred_element_type=F32)
                  + jnp.dot(xi, cci_ref[k], preferred_element_type=F32))
    y = jnp.concatenate(ys, axis=-1) + d_ref[...] * u_ref[...]
    y = jax.nn.gelu(y, approximate=True)
    z = jnp.dot(y.astype(BF16), wglu_ref[...], preferred_element_type=F32) + bglu_ref[...]
    o_ref[...] = (y * jax.nn.sigmoid(z)).astype(o_ref.dtype)


def _s5_mixer(proj, a_re, a_im, bb_re, bb_im, c_re, c_im, d_skip, w_glu, b_glu):
    W = S5_NSTATE // S5_SLICES
    bbr = _block_diag_slices(bb_re).astype(BF16)
    bbi = _block_diag_slices(bb_im).astype(BF16)
    ccr = _block_diag_slices(jnp.swapaxes(c_re, 1, 2)).astype(BF16)
    cci = _block_diag_slices(-jnp.swapaxes(c_im, 1, 2)).astype(BF16)
    const3 = lambda t: (0, 0, 0)
    const2 = lambda t: (0, 0)
    return pl.pallas_call(
        _s5_kernel,
        grid=(SEQ // S5_T,),
        in_specs=[
            pl.BlockSpec((S5_T, D_A), lambda t: (t, 0)),
            pl.BlockSpec((S5_SLICES, LANES, W), const3),
            pl.BlockSpec((S5_SLICES, LANES, W), const3),
            pl.BlockSpec((1, S5_NSTATE), const2),
            pl.BlockSpec((1, S5_NSTATE), const2),
            pl.BlockSpec((S5_SLICES, W, LANES), const3),
            pl.BlockSpec((S5_SLICES, W, LANES), const3),
            pl.BlockSpec((1, D_A), const2),
            pl.BlockSpec((D_A, D_A), const2),
            pl.BlockSpec((1, D_A), const2),
        ],
        out_specs=pl.BlockSpec((S5_T, D_A), lambda t: (t, 0)),
        out_shape=jax.ShapeDtypeStruct((SEQ, D_A), BF16),
        scratch_shapes=[pltpu.VMEM((S5_T, S5_NSTATE), F32), pltpu.VMEM((S5_T, S5_NSTATE), F32),
                        pltpu.VMEM((1, S5_NSTATE), F32), pltpu.VMEM((1, S5_NSTATE), F32)],
        compiler_params=_cparams(("arbitrary",)),
        name="s5_mixer",
    )(proj, bbr, bbi, a_re.reshape(1, S5_NSTATE), a_im.reshape(1, S5_NSTATE), ccr, cci,
      d_skip.reshape(1, D_A), w_glu, b_glu.reshape(1, D_A))


def _kv_kernel(lat_ref, g_ref, w_ref, k_ref, v_ref):
    x = lat_ref[...]
    ms = jnp.mean(x * x, axis=-1, keepdims=True)
    xn = ((x * lax.rsqrt(ms + EPS)) * g_ref[...]).astype(BF16)
    kv = jnp.dot(xn, w_ref[...], preferred_element_type=F32)
    k_ref[...] = kv[:, :D_B].astype(k_ref.dtype)
    v_ref[...] = kv[:, D_B:].astype(v_ref.dtype)


def _kv_up(proj, g, w):
    tm = 1024
    lat_block = (2 * D_A + IDX_HEADS * IDX_DIM) // KV_RANK
    return pl.pallas_call(
        _kv_kernel,
        grid=(SEQ // tm,),
        in_specs=[
            pl.BlockSpec((tm, KV_RANK), lambda i: (i, lat_block)),
            pl.BlockSpec((1, KV_RANK), lambda i: (0, 0)),
            pl.BlockSpec((KV_RANK, 2 * D_B), lambda i: (0, 0)),
        ],
        out_specs=[pl.BlockSpec((tm, D_B), lambda i: (i, 0)), pl.BlockSpec((tm, D_B), lambda i: (i, 0))],
        out_shape=[jax.ShapeDtypeStruct((SEQ, D_B), BF16), jax.ShapeDtypeStruct((SEQ, D_B), BF16)],
        compiler_params=_cparams(("arbitrary",)),
        name="dsa_kv_up",
    )(proj, g.reshape(1, KV_RANK), w)


def _dsa_kernel(q_ref, qi_ref, w_ref, kidx_ref, k_ref, v_ref, o_ref,
                sc_ref, wb_ref, qs_ref, thr_ref, lo_ref, hi_ref, cnt_ref, m_ref, l_ref, acc_ref):
    TQ, KT, SUB = DSA_TQ, DSA_KT, DSA_SUB
    i = pl.program_id(0)
    kt = pl.program_id(1)
    last_kt = (i * TQ + TQ - 1) // KT
    n_sub = (i * TQ + TQ) // SUB
    t_row = i * TQ + lax.broadcasted_iota(jnp.int32, (TQ, 1), 0)
    row_chunk = t_row // CHUNK

    @pl.when(kt == 0)
    def _():
        qs_ref[...] = (q_ref[...] * (HEAD_DIM ** -0.5)).astype(BF16)
        w = w_ref[...]
        for h in range(IDX_HEADS):
            wb_ref[h] = jnp.broadcast_to(w[:, IDX_DIM + h:IDX_DIM + h + 1], (TQ, LANES))
        qi = qi_ref[0]

        def score_tile(j, carry):
            mx, mn = carry
            s0 = pl.multiple_of(j * SUB, SUB)
            kk = kidx_ref[pl.ds(s0, SUB), :IDX_DIM].astype(BF16)
            s_all = lax.dot_general(qi, kk, (((1,), (1,)), ((), ())), preferred_element_type=F32)
            key_chunk = (s0 + lax.broadcasted_iota(jnp.int32, (1, SUB), 1)) // CHUNK
            adm = key_chunk <= row_chunk
            halves = []
            for half in range(SUB // LANES):
                tot = jnp.zeros((TQ, LANES), F32)
                for h in range(IDX_HEADS):
                    blk = s_all[h * TQ:(h + 1) * TQ, half * LANES:(half + 1) * LANES]
                    tot += jnp.maximum(blk, 0.0) * wb_ref[h]
                halves.append(tot)
            tot = jnp.concatenate(halves, axis=-1) * ((IDX_DIM ** -0.5) * (IDX_HEADS ** -0.5))
            sc_ref[j] = jnp.where(adm, tot, -jnp.inf)
            mx = jnp.maximum(mx, jnp.max(jnp.where(adm, tot, -jnp.inf), axis=-1, keepdims=True))
            mn = jnp.minimum(mn, jnp.min(jnp.where(adm, tot, jnp.inf), axis=-1, keepdims=True))
            return mx, mn

        mx, mn = lax.fori_loop(0, n_sub, score_tile,
                               (jnp.full((TQ, 1), -jnp.inf, F32), jnp.full((TQ, 1), jnp.inf, F32)))

        def fill_tile(j, c):
            sc_ref[j] = jnp.full((TQ, SUB), -jnp.inf, F32)
            return c

        lax.fori_loop(n_sub, (last_kt + 1) * (KT // SUB), fill_tile, 0)

        n_adm = ((row_chunk + 1) * CHUNK).astype(F32)
        keep_all = n_adm <= float(TOPK)
        lo_ref[...] = jnp.where(keep_all, -F32_MAX, mn)
        hi_ref[...] = mx
        cnt_ref[...] = jnp.where(keep_all, float(TOPK), n_adm)

        RG = 64

        def count_ge(mid):
            parts = []
            for rg in range(TQ // RG):
                midb = jnp.broadcast_to(mid[rg * RG:(rg + 1) * RG], (RG, LANES))

                def body(j, acc, rg=rg, midb=midb):
                    t = sc_ref[j, rg * RG:(rg + 1) * RG, :]
                    for half in range(SUB // LANES):
                        acc = acc + jnp.where(t[:, half * LANES:(half + 1) * LANES] >= midb, 1.0, 0.0)
                    return acc

                acc = lax.fori_loop(0, n_sub, body, jnp.zeros((RG, LANES), F32))
                parts.append(jnp.sum(acc, axis=-1, keepdims=True))
            return jnp.concatenate(parts, axis=0)

        def n_open():
            return jnp.max(jnp.where(cnt_ref[...] != float(TOPK), 1.0, 0.0))

        def cond(c):
            it, open_rows = c
            return jnp.logical_and(it < 48, open_rows > 0.5)

        def body(c):
            it, _ = c
            lo = lo_ref[...]
            hi = hi_ref[...]
            cnt = cnt_ref[...]
            mid = lo + 0.5 * (hi - lo)
            c_mid = count_ge(mid)
            live = cnt != float(TOPK)
            ge = jnp.logical_and(live, c_mid >= float(TOPK))
            lt = jnp.logical_and(live, c_mid < float(TOPK))
            lo_ref[...] = jnp.where(ge, mid, lo)
            cnt_ref[...] = jnp.where(ge, c_mid, cnt)
            hi_ref[...] = jnp.where(lt, mid, hi)
            return it + 1, n_open()

        lax.while_loop(cond, body, (jnp.int32(0), n_open()))
        thr_ref[...] = lo_ref[...]

        m_ref[...] = jnp.full(m_ref.shape, NEG_BIG, F32)
        l_ref[...] = jnp.zeros_like(l_ref)
        acc_ref[...] = jnp.zeros_like(acc_ref)

    @pl.when(kt <= last_kt)
    def _():
        thr = thr_ref[...]
        sel = jnp.concatenate([sc_ref[kt * (KT // SUB) + jj] for jj in range(KT // SUB)], axis=-1) >= thr
        s_idx = kt * KT + lax.broadcasted_iota(jnp.int32, (1, KT), 1)
        dist = jnp.abs(t_row - s_idx).astype(F32)
        for h in range(B_HEADS):
            hs = slice(h * HEAD_DIM, (h + 1) * HEAD_DIM)
            lg = lax.dot_general(qs_ref[:, hs], k_ref[:, hs], (((1,), (1,)), ((), ())),
                                 preferred_element_type=F32)
            slope = 2.0 ** (-8.0 * (h + 1) / B_HEADS)
            lg = jnp.where(sel, lg - slope * dist, NEG_BIG)
            m_old = m_ref[h]
            m_new = jnp.maximum(m_old, jnp.max(lg, axis=-1, keepdims=True))
            alpha = jnp.exp(m_old - m_new)
            p = jnp.exp(lg - m_new)
            l_ref[h] = alpha * l_ref[h] + jnp.sum(p, axis=-1, keepdims=True)
            acc_ref[:, hs] = alpha * acc_ref[:, hs] + jnp.dot(p.astype(BF16), v_ref[:, hs],
                                                              preferred_element_type=F32)
            m_ref[h] = m_new

    @pl.when(kt == last_kt)
    def _():
        outs = []
        for h in range(B_HEADS):
            hs = slice(h * HEAD_DIM, (h + 1) * HEAD_DIM)
            outs.append(acc_ref[:, hs] / l_ref[h])
        o_ref[...] = jnp.concatenate(outs, axis=-1).astype(o_ref.dtype)


def _dsa_attention(proj, qi_blocks, k, v):
    TQ, KT = DSA_TQ, DSA_KT
    nqb, nkt = SEQ // TQ, SEQ // KT
    kw_block = (2 * D_A + IDX_HEADS * IDX_DIM + KV_RANK) // LANES

    def kv_map(i, kt):
        return (jnp.minimum(kt, (i * TQ + TQ - 1) // KT), 0)

    return pl.pallas_call(
        _dsa_kernel,
        grid=(nqb, nkt),
        in_specs=[
            pl.BlockSpec((TQ, D_B), lambda i, kt: (i, 1)),
            pl.BlockSpec((1, IDX_HEADS * TQ, IDX_DIM), lambda i, kt: (i, 0, 0)),
            pl.BlockSpec((TQ, LANES), lambda i, kt: (i, kw_block)),
            pl.BlockSpec((SEQ, LANES), lambda i, kt: (0, kw_block)),
            pl.BlockSpec((KT, D_B), kv_map),
            pl.BlockSpec((KT, D_B), kv_map),
        ],
        out_specs=pl.BlockSpec((TQ, D_B), lambda i, kt: (i, 0)),
        out_shape=jax.ShapeDtypeStruct((SEQ, D_B), BF16),
        scratch_shapes=[
            pltpu.VMEM((SEQ // DSA_SUB, TQ, DSA_SUB), F32),
            pltpu.VMEM((IDX_HEADS, TQ, LANES), F32),
            pltpu.VMEM((TQ, D_B), BF16),
            pltpu.VMEM((TQ, 1), F32),
            pltpu.VMEM((TQ, 1), F32),
            pltpu.VMEM((TQ, 1), F32),
            pltpu.VMEM((TQ, 1), F32),
            pltpu.VMEM((B_HEADS, TQ, 1), F32),
            pltpu.VMEM((B_HEADS, TQ, 1), F32),
            pltpu.VMEM((TQ, D_B), F32),
        ],
        compiler_params=_cparams(("arbitrary", "arbitrary")),
        name="dsa_attention",
    )(proj, qi_blocks, proj, proj, k, v)


def _ca_kernel(q_ref, k0_ref, k1_ref, k2_ref, v0_ref, v1_ref, v2_ref, bias_ref, o_ref):
    i = pl.program_id(1)
    k_refs = (k0_ref, k1_ref, k2_ref)
    v_refs = (v0_ref, v1_ref, v2_ref)
    outs = []
    for hh in range(CA_HB):
        hs = slice(hh * HEAD_DIM, (hh + 1) * HEAD_DIM)
        qh = q_ref[:, hs]
        lgs = []
        for j in range(CA_NKB):
            lg = lax.dot_general(qh, k_refs[j][:, hs], (((1,), (1,)), ((), ())), preferred_element_type=F32)
            lg = lg * (HEAD_DIM ** -0.5) + bias_ref[hh, :, j * CA_TQ:(j + 1) * CA_TQ]
            lgs.append(jnp.where(i + j >= CA_NKB - 1, lg, NEG_BIG))
        lg = jnp.concatenate(lgs, axis=-1)
        m = jnp.max(lg, axis=-1, keepdims=True)
        p = jnp.exp(lg - m)
        l = jnp.sum(p, axis=-1, keepdims=True)
        pb = p.astype(BF16)
        acc = jnp.dot(pb[:, :CA_TQ], v_refs[0][:, hs], preferred_element_type=F32)
        for j in range(1, CA_NKB):
            acc += jnp.dot(pb[:, j * CA_TQ:(j + 1) * CA_TQ], v_refs[j][:, hs], preferred_element_type=F32)
        outs.append(acc / l)
    o_ref[...] = jnp.concatenate(outs, axis=-1).astype(o_ref.dtype)


def _ca_bias_table(rel_bias):
    r = np.arange(CA_TQ)[:, None]
    c = np.arange(CA_NKB * CA_TQ)[None, :]
    rel = r - c + (CA_NKB - 1) * CA_TQ
    idx = np.clip(rel, -MAX_REL, MAX_REL) + MAX_REL
    kc = c // CHUNK
    qc = r // CHUNK + (CA_NKB - 1) * CA_TQ // CHUNK
    band = (kc >= qc - C_LEFT_CHUNKS) & (kc <= qc)
    tab = rel_bias.astype(F32)[:, idx]
    return jnp.where(band[None], tab, NEG_BIG)


def _chunk_attention(qkv, bias_tab):
    nt = SEQ // CA_TQ
    hw = CA_HB * HEAD_DIM
    nhb = C_HEADS // CA_HB

    def kmap(j, base):
        return lambda hb, i: (jnp.maximum(i - (CA_NKB - 1) + j, 0), base + hb)

    in_specs = [pl.BlockSpec((CA_TQ, hw), lambda hb, i: (i, hb))]
    in_specs += [pl.BlockSpec((CA_TQ, hw), kmap(j, nhb)) for j in range(CA_NKB)]
    in_specs += [pl.BlockSpec((CA_TQ, hw), kmap(j, 2 * nhb)) for j in range(CA_NKB)]
    in_specs += [pl.BlockSpec((CA_HB, CA_TQ, CA_NKB * CA_TQ), lambda hb, i: (hb, 0, 0))]
    return pl.pallas_call(
        _ca_kernel,
        grid=(nhb, nt),
        in_specs=in_specs,
        out_specs=pl.BlockSpec((CA_TQ, hw), lambda hb, i: (i, hb)),
        out_shape=jax.ShapeDtypeStruct((SEQ, D_MODEL), BF16),
        compiler_params=_cparams(("arbitrary", "arbitrary")),
        name="chunk_attention",
    )(qkv, qkv, qkv, qkv, qkv, qkv, qkv, bias_tab)


def _pad_cols(w, n):
    return jnp.pad(w, ((0, 0), (0, n - w.shape[1])))


def _ffn_weights(wg, wu, wd):
    pad = D_FF_PAD - D_FF
    return (_pad_cols(wg, D_FF_PAD).astype(BF16), _pad_cols(wu, D_FF_PAD).astype(BF16),
            jnp.pad(wd, ((0, pad), (0, 0))).astype(BF16))


def kernel(x, c, ada_w, ada_b, norm_g, ffn_w_gate, ffn_w_up, ffn_w_down, ab_w_in, s5_lam_re, s5_lam_im, s5_log_dt, s5_b_re, s5_b_im, s5_c_re, s5_c_im, s5_d, s5_w_glu, s5_b_glu, dsa_kv_norm_g, dsa_w_kv_up, ab_w_out, c_w_qkv, c_rel_bias, c_w_out, final_norm_g):
    mod = _modulation(c, ada_w, ada_b)
    h = x.reshape(SEQ, D_MODEL)

    for layer in range(DEPTH):
        h = _ffn(h, norm_g[layer, 0], mod[layer, 0],
                 *_ffn_weights(ffn_w_gate[layer, 0], ffn_w_up[layer, 0], ffn_w_down[layer, 0]))
        if layer % 2 == 0:
            e = layer // 2
            w_in = ab_w_in[e]
            o_q, o_kv, o_qi = D_A, D_A + D_B, D_A + D_B + KV_RANK
            o_ki = o_qi + IDX_HEADS * IDX_DIM
            w_in = jnp.concatenate([w_in[:, :o_kv], w_in[:, o_qi:o_ki], w_in[:, o_kv:o_qi], w_in[:, o_ki:]], axis=1)
            proj = _norm_proj(h, norm_g[layer, 1], mod[layer, 1], _pad_cols(w_in, AB_N_PAD).astype(BF16), F32)

            a_re, a_im, bb_re, bb_im = _s5_params(s5_lam_re[e], s5_lam_im[e], s5_log_dt[e], s5_b_re[e], s5_b_im[e])
            y_a = _s5_mixer(proj, a_re, a_im, bb_re, bb_im, s5_c_re[e], s5_c_im[e], s5_d[e],
                            s5_w_glu[e].astype(BF16), s5_b_glu[e])

            k, v = _kv_up(proj, dsa_kv_norm_g[e], dsa_w_kv_up[e].astype(BF16))
            q_idx = proj[:, 2 * D_A:2 * D_A + IDX_HEADS * IDX_DIM]
            qi_blocks = (q_idx.reshape(SEQ // DSA_TQ, DSA_TQ, IDX_HEADS, IDX_DIM).transpose(0, 2, 1, 3)
                         .reshape(SEQ // DSA_TQ, IDX_HEADS * DSA_TQ, IDX_DIM).astype(BF16))
            y_b = _dsa_attention(proj, qi_blocks, k, v)

            w_out = ab_w_out[e].astype(BF16)
            h = _out_proj([y_a, y_b], [w_out[:D_A], w_out[D_A:]], h, mod[layer, 1])
        else:
            o = layer // 2
            qkv = _norm_proj(h, norm_g[layer, 1], mod[layer, 1], c_w_qkv[o].astype(BF16), BF16)
            att = _chunk_attention(qkv, _ca_bias_table(c_rel_bias[o]))
            h = _out_proj([att], [c_w_out[o].astype(BF16)], h, mod[layer, 1])
        h = _ffn(h, norm_g[layer, 2], mod[layer, 2],
                 *_ffn_weights(ffn_w_gate[layer, 1], ffn_w_up[layer, 1], ffn_w_down[layer, 1]),
                 final_g=final_norm_g if layer == DEPTH - 1 else None)
    return h.reshape(1, SEQ, D_MODEL)
```

```python
import functools
import math

import jax
import jax.numpy as jnp
import numpy as np
from jax import lax
from jax.experimental import pallas as pl
from jax.experimental.pallas import tpu as pltpu

F32 = jnp.float32
BF16 = jnp.bfloat16

D_MODEL = 2048
SEQ = 8192
DEPTH = 2
CHUNK = 64
HEAD_DIM = 128
D_FF = 5504
N_SUB = 3
EPS = 1e-6
D_A = D_MODEL // 2
S5_GROUP = 16
S5_GROUPS = D_A // S5_GROUP
S5_STATE = 64
D_B = D_MODEL // 2
B_HEADS = D_B // HEAD_DIM
KV_RANK = D_MODEL // 8
IDX_HEADS = 16
IDX_DIM = 64
TOPK = 256
C_HEADS = D_MODEL // HEAD_DIM
C_LEFT_CHUNKS = 8
MAX_REL = 256

LANES = 128
SUBLANES = 8
VMEM_LIMIT = 56 * 1024 * 1024
NEG_BIG = -1e30
F32_MAX = float(np.finfo(np.float32).max)

FFN_TM = 512
FFN_TF = 512
PROJ_TM = 1024
PROJ_TN = 512
OUT_TM = 512
AB_N_PAD = 3584
S5_T = 256
S5_SLICES = D_A // LANES
S5_NSTATE = S5_GROUPS * S5_STATE
DSA_TQ = 256
DSA_KT = 1024
DSA_SUB = 256
CA_TQ = 256
CA_HB = 4
CA_NKB = 3


def _cparams(sem):
    return pltpu.CompilerParams(dimension_semantics=sem, vmem_limit_bytes=VMEM_LIMIT)


def _mod_kernel(c_ref, w_ref, b_ref, o_ref):
    rows = 256
    tn = o_ref.shape[-1]

    def body(i, acc):
        r0 = pl.multiple_of(i * rows, rows)
        cc = c_ref[pl.ds(r0, rows), :]
        cc = cc * jax.nn.sigmoid(cc)
        w = w_ref[0, pl.ds(r0, rows), :]
        return acc + jnp.sum((w * cc).reshape(rows // SUBLANES, SUBLANES, tn), axis=0)

    acc = lax.fori_loop(0, D_MODEL // rows, body, jnp.zeros((SUBLANES, tn), F32))
    o_ref[0] = jnp.sum(acc, axis=0, keepdims=True) + b_ref[0]


def _modulation(c, ada_w, ada_b):
    n = N_SUB * 3 * D_MODEL
    tn = 1024
    c_col = c.reshape(D_MODEL, 1)
    out = pl.pallas_call(
        _mod_kernel,
        grid=(DEPTH, n // tn),
        in_specs=[
            pl.BlockSpec((D_MODEL, 1), lambda l, j: (0, 0)),
            pl.BlockSpec((1, D_MODEL, tn), lambda l, j: (l, 0, j)),
            pl.BlockSpec((1, 1, tn), lambda l, j: (l, 0, j)),
        ],
        out_specs=pl.BlockSpec((1, 1, tn), lambda l, j: (l, 0, j)),
        out_shape=jax.ShapeDtypeStruct((DEPTH, 1, n), F32),
        compiler_params=_cparams(("arbitrary", "arbitrary")),
        name="adaln_mod",
    )(c_col, ada_w, ada_b.reshape(DEPTH, 1, n))
    return out.reshape(DEPTH, N_SUB, 3, D_MODEL)


def _adaln(x, g, mod):
    ms = jnp.mean(x * x, axis=-1, keepdims=True)
    y = (x * lax.rsqrt(ms + EPS)) * g
    return y * (1.0 + mod[1:2, :]) + mod[0:1, :]


def _ffn_kernel(h_ref, g_ref, mod_ref, wg_ref, wu_ref, wd_ref, *rest, n_f, final):
    if final:
        fg_ref, o_ref, hn_ref, acc_ref = rest
    else:
        o_ref, hn_ref, acc_ref = rest
    f = pl.program_id(1)

    @pl.when(f == 0)
    def _():
        hn_ref[...] = _adaln(h_ref[...], g_ref[...], mod_ref[...]).astype(BF16)
        acc_ref[...] = jnp.zeros_like(acc_ref)

    def accumulate(width):
        hn = hn_ref[...]
        gate = jnp.dot(hn, wg_ref[:, :width], preferred_element_type=F32)
        up = jnp.dot(hn, wu_ref[:, :width], preferred_element_type=F32)
        act = (gate * jax.nn.sigmoid(gate)) * up
        acc_ref[...] += jnp.dot(act.astype(BF16), wd_ref[:width, :], preferred_element_type=F32)

    @pl.when(f < n_f - 1)
    def _():
        accumulate(FFN_TF)

    @pl.when(f == n_f - 1)
    def _():
        accumulate(D_FF - (n_f - 1) * FFN_TF)
        out = h_ref[...] + (0.5 * mod_ref[2:3, :]) * acc_ref[...]
        if final:
            ms = jnp.mean(out * out, axis=-1, keepdims=True)
            out = (out * lax.rsqrt(ms + EPS)) * fg_ref[...]
        o_ref[...] = out


def _ffn(h, g, mod, wg, wu, wd, layer, which, final_g=None):
    n_f = pl.cdiv(D_FF, FFN_TF)
    final = final_g is not None
    in_specs = [
        pl.BlockSpec((FFN_TM, D_MODEL), lambda i, f: (i, 0)),
        pl.BlockSpec((1, D_MODEL), lambda i, f: (0, 0)),
        pl.BlockSpec((3, D_MODEL), lambda i, f: (0, 0)),
        pl.BlockSpec((None, None, D_MODEL, FFN_TF), lambda i, f: (layer, which, 0, f)),
        pl.BlockSpec((None, None, D_MODEL, FFN_TF), lambda i, f: (layer, which, 0, f)),
        pl.BlockSpec((None, None, FFN_TF, D_MODEL), lambda i, f: (layer, which, f, 0)),
    ]
    args = [h, g.reshape(1, D_MODEL), mod, wg, wu, wd]
    if final:
        in_specs.append(pl.BlockSpec((1, D_MODEL), lambda i, f: (0, 0)))
        args.append(final_g.reshape(1, D_MODEL))
    return pl.pallas_call(
        functools.partial(_ffn_kernel, n_f=n_f, final=final),
        grid=(SEQ // FFN_TM, n_f),
        in_specs=in_specs,
        out_specs=pl.BlockSpec((FFN_TM, D_MODEL), lambda i, f: (i, 0)),
        out_shape=jax.ShapeDtypeStruct((SEQ, D_MODEL), F32),
        scratch_shapes=[pltpu.VMEM((FFN_TM, D_MODEL), BF16), pltpu.VMEM((FFN_TM, D_MODEL), F32)],
        compiler_params=_cparams(("arbitrary", "arbitrary")),
        name="ffn_swiglu",
    )(*args)


def _proj_kernel(h_ref, g_ref, mod_ref, w_ref, o_ref, hn_ref):
    @pl.when(pl.program_id(1) == 0)
    def _():
        hn_ref[...] = _adaln(h_ref[...], g_ref[...], mod_ref[...]).astype(BF16)

    o_ref[...] = jnp.dot(hn_ref[...], w_ref[...], preferred_element_type=F32).astype(o_ref.dtype)


def _norm_proj(h, g, mod, w, out_dtype):
    n = w.shape[1]
    return pl.pallas_call(
        _proj_kernel,
        grid=(SEQ // PROJ_TM, n // PROJ_TN),
        in_specs=[
            pl.BlockSpec((PROJ_TM, D_MODEL), lambda i, j: (i, 0)),
            pl.BlockSpec((1, D_MODEL), lambda i, j: (0, 0)),
            pl.BlockSpec((3, D_MODEL), lambda i, j: (0, 0)),
            pl.BlockSpec((D_MODEL, PROJ_TN), lambda i, j: (0, j)),
        ],
        out_specs=pl.BlockSpec((PROJ_TM, PROJ_TN), lambda i, j: (i, j)),
        out_shape=jax.ShapeDtypeStruct((SEQ, n), out_dtype),
        scratch_shapes=[pltpu.VMEM((PROJ_TM, D_MODEL), BF16)],
        compiler_params=_cparams(("arbitrary", "arbitrary")),
        name="adaln_proj",
    )(h, g.reshape(1, D_MODEL), mod, w)


def _out_kernel(*refs, n_lhs):
    lhs = refs[:n_lhs]
    ws = refs[n_lhs:2 * n_lhs]
    h_ref, mod_ref, o_ref = refs[2 * n_lhs:]
    y = jnp.dot(lhs[0][...], ws[0][...], preferred_element_type=F32)
    for a_ref, w_ref in zip(lhs[1:], ws[1:]):
        y += jnp.dot(a_ref[...], w_ref[...], preferred_element_type=F32)
    o_ref[...] = h_ref[...] + mod_ref[2:3, :] * y


def _out_proj(lhs, ws, h, mod):
    n_lhs = len(lhs)
    in_specs = [pl.BlockSpec((OUT_TM, a.shape[1]), lambda i: (i, 0)) for a in lhs]
    in_specs += [pl.BlockSpec(w.shape, lambda i: (0, 0)) for w in ws]
    in_specs += [pl.BlockSpec((OUT_TM, D_MODEL), lambda i: (i, 0)), pl.BlockSpec((3, D_MODEL), lambda i: (0, 0))]
    return pl.pallas_call(
        functools.partial(_out_kernel, n_lhs=n_lhs),
        grid=(SEQ // OUT_TM,),
        in_specs=in_specs,
        out_specs=pl.BlockSpec((OUT_TM, D_MODEL), lambda i: (i, 0)),
        out_shape=jax.ShapeDtypeStruct((SEQ, D_MODEL), F32),
        compiler_params=_cparams(("arbitrary",)),
        name="out_proj_residual",
    )(*lhs, *ws, h, mod)


def _s5_param_kernel(lr_ref, li_ref, ldt_ref, br_ref, bi_ref, are_ref, aim_ref, bbr_ref, bbi_ref):
    lr = lr_ref[...]
    li = li_ref[...]
    dt = jnp.exp(ldt_ref[...])
    mag = jnp.exp(lr * dt)
    ab_re = mag * jnp.cos(li * dt)
    ab_im = mag * jnp.sin(li * dt)
    den = lr * lr + li * li
    nr = ab_re - 1.0
    f_re = (nr * lr + ab_im * li) / den
    f_im = (ab_im * lr - nr * li) / den
    are_ref[...] = ab_re
    aim_ref[...] = ab_im
    br = br_ref[...]
    bi = bi_ref[...]
    bbr_ref[...] = f_re * br - f_im * bi
    bbi_ref[...] = f_re * bi + f_im * br


def _s5_params(lam_re, lam_im, log_dt, b_re, b_im):
    G, P, CG = S5_GROUPS, S5_STATE, S5_GROUP
    return pl.pallas_call(
        _s5_param_kernel,
        out_shape=[jax.ShapeDtypeStruct((G, 1, P), F32), jax.ShapeDtypeStruct((G, 1, P), F32),
                   jax.ShapeDtypeStruct((G, CG, P), F32), jax.ShapeDtypeStruct((G, CG, P), F32)],
        name="s5_zoh_params",
    )(lam_re.reshape(G, 1, P), lam_im.reshape(G, 1, P), log_dt.reshape(G, 1, 1),
      jnp.swapaxes(b_re, 1, 2), jnp.swapaxes(b_im, 1, 2))


def _block_diag_slices(m):
    _, r, c = m.shape
    m4 = m.reshape(S5_SLICES, SUBLANES, r, c)
    eye = jnp.eye(SUBLANES, dtype=m.dtype)
    out = m4[:, :, :, None, :] * eye[None, :, None, :, None]
    return out.reshape(S5_SLICES, SUBLANES * r, SUBLANES * c)


def _s5_kernel(u_ref, bbr_ref, bbi_ref, are_ref, aim_ref, ccr_ref, cci_ref, d_ref, wglu_ref, bglu_ref,
               o_ref, xr_ref, xi_ref, sr_ref, si_ref):
    T = S5_T
    W = S5_NSTATE // S5_SLICES

    @pl.when(pl.program_id(0) == 0)
    def _():
        sr_ref[...] = jnp.zeros_like(sr_ref)
        si_ref[...] = jnp.zeros_like(si_ref)

    for k in range(S5_SLICES):
        uk = u_ref[:, k * LANES:(k + 1) * LANES].astype(BF16)
        xr_ref[:, k * W:(k + 1) * W] = jnp.dot(uk, bbr_ref[k], preferred_element_type=F32)
        xi_ref[:, k * W:(k + 1) * W] = jnp.dot(uk, bbi_ref[k], preferred_element_type=F32)

    CW = 1024
    for cg in range(S5_NSTATE // CW):
        cols = slice(cg * CW, (cg + 1) * CW)
        ar = are_ref[:, cols]
        ai = aim_ref[:, cols]

        def step(t, carry, cols=cols, ar=ar, ai=ai):
            sr, si = carry
            nr = ar * sr - ai * si + xr_ref[pl.ds(t, 1), cols]
            ni = ar * si + ai * sr + xi_ref[pl.ds(t, 1), cols]
            xr_ref[pl.ds(t, 1), cols] = nr
            xi_ref[pl.ds(t, 1), cols] = ni
            return nr, ni

        sr, si = lax.fori_loop(0, T, step, (sr_ref[:, cols], si_ref[:, cols]), unroll=4)
        sr_ref[:, cols] = sr
        si_ref[:, cols] = si

    ys = []
    for k in range(S5_SLICES):
        xr = xr_ref[:, k * W:(k + 1) * W].astype(BF16)
        xi = xi_ref[:, k * W:(k + 1) * W].astype(BF16)
        ys.append(jnp.dot(xr, ccr_ref[k], preferred_element_type=F32)
                  + jnp.dot(xi, cci_ref[k], preferred_element_type=F32))
    y = jnp.concatenate(ys, axis=-1) + d_ref[...] * u_ref[...]
    y = jax.nn.gelu(y, approximate=True)
    z = jnp.dot(y.astype(BF16), wglu_ref[...], preferred_element_type=F32) + bglu_ref[...]
    o_ref[...] = (y * jax.nn.sigmoid(z)).astype(o_ref.dtype)


def _s5_mixer(proj, a_re, a_im, bb_re, bb_im, c_re, c_im, d_skip, w_glu, b_glu):
    W = S5_NSTATE // S5_SLICES
    bbr = _block_diag_slices(bb_re).astype(BF16)
    bbi = _block_diag_slices(bb_im).astype(BF16)
    ccr = _block_diag_slices(jnp.swapaxes(c_re, 1, 2)).astype(BF16)
    cci = _block_diag_slices(-jnp.swapaxes(c_im, 1, 2)).astype(BF16)
    const3 = lambda t: (0, 0, 0)
    const2 = lambda t: (0, 0)
    return pl.pallas_call(
        _s5_kernel,
        grid=(SEQ // S5_T,),
        in_specs=[
            pl.BlockSpec((S5_T, D_A), lambda t: (t, 0)),
            pl.BlockSpec((S5_SLICES, LANES, W), const3),
            pl.BlockSpec((S5_SLICES, LANES, W), const3),
            pl.BlockSpec((1, S5_NSTATE), const2),
            pl.BlockSpec((1, S5_NSTATE), const2),
            pl.BlockSpec((S5_SLICES, W, LANES), const3),
            pl.BlockSpec((S5_SLICES, W, LANES), const3),
            pl.BlockSpec((1, D_A), const2),
            pl.BlockSpec((D_A, D_A), const2),
            pl.BlockSpec((1, D_A), const2),
        ],
        out_specs=pl.BlockSpec((S5_T, D_A), lambda t: (t, 0)),
        out_shape=jax.ShapeDtypeStruct((SEQ, D_A), BF16),
        scratch_shapes=[pltpu.VMEM((S5_T, S5_NSTATE), F32), pltpu.VMEM((S5_T, S5_NSTATE), F32),
                        pltpu.VMEM((1, S5_NSTATE), F32), pltpu.VMEM((1, S5_NSTATE), F32)],
        compiler_params=_cparams(("arbitrary",)),
        name="s5_mixer",
    )(proj, bbr, bbi, a_re.reshape(1, S5_NSTATE), a_im.reshape(1, S5_NSTATE), ccr, cci,
      d_skip.reshape(1, D_A), w_glu, b_glu.reshape(1, D_A))


def _kv_kernel(lat_ref, g_ref, w_ref, k_ref, v_ref):
    x = lat_ref[...]
    ms = jnp.mean(x * x, axis=-1, keepdims=True)
    xn = ((x * lax.rsqrt(ms + EPS)) * g_ref[...]).astype(BF16)
    kv = jnp.dot(xn, w_ref[...], preferred_element_type=F32)
    k_ref[...] = kv[:, :D_B].astype(k_ref.dtype)
    v_ref[...] = kv[:, D_B:].astype(v_ref.dtype)


def _kv_up(proj, g, w):
    tm = 1024
    lat_block = (2 * D_A + IDX_HEADS * IDX_DIM) // KV_RANK
    return pl.pallas_call(
        _kv_kernel,
        grid=(SEQ // tm,),
        in_specs=[
            pl.BlockSpec((tm, KV_RANK), lambda i: (i, lat_block)),
            pl.BlockSpec((1, KV_RANK), lambda i: (0, 0)),
            pl.BlockSpec((KV_RANK, 2 * D_B), lambda i: (0, 0)),
        ],
        out_specs=[pl.BlockSpec((tm, D_B), lambda i: (i, 0)), pl.BlockSpec((tm, D_B), lambda i: (i, 0))],
        out_shape=[jax.ShapeDtypeStruct((SEQ, D_B), BF16), jax.ShapeDtypeStruct((SEQ, D_B), BF16)],
        compiler_params=_cparams(("arbitrary",)),
        name="dsa_kv_up",
    )(proj, g.reshape(1, KV_RANK), w)


def _dsa_kernel(q_ref, qi_ref, w_ref, kidx_ref, k_ref, v_ref, o_ref,
                sc_ref, wb_ref, qs_ref, thr_ref, lo_ref, hi_ref, cnt_ref, m_ref, l_ref, acc_ref):
    TQ, KT, SUB = DSA_TQ, DSA_KT, DSA_SUB
    i = pl.program_id(0)
    kt = pl.program_id(1)
    last_kt = (i * TQ + TQ - 1) // KT
    n_sub = (i * TQ + TQ) // SUB
    t_row = i * TQ + lax.broadcasted_iota(jnp.int32, (TQ, 1), 0)
    row_chunk = t_row // CHUNK

    @pl.when(kt == 0)
    def _():
        qs_ref[...] = (q_ref[...] * (HEAD_DIM ** -0.5)).astype(BF16)
        w = w_ref[...]
        for h in range(IDX_HEADS):
            wb_ref[h] = jnp.broadcast_to(w[:, IDX_DIM + h:IDX_DIM + h + 1], (TQ, LANES))
        qi = qi_ref[0]

        def score_tile(j, carry):
            mx, mn = carry
            s0 = pl.multiple_of(j * SUB, SUB)
            kk = kidx_ref[pl.ds(s0, SUB), :IDX_DIM].astype(BF16)
            s_all = lax.dot_general(qi, kk, (((1,), (1,)), ((), ())), preferred_element_type=F32)
            key_chunk = (s0 + lax.broadcasted_iota(jnp.int32, (1, SUB), 1)) // CHUNK
            adm = key_chunk <= row_chunk
            halves = []
            for half in range(SUB // LANES):
                tot = jnp.zeros((TQ, LANES), F32)
                for h in range(IDX_HEADS):
                    blk = s_all[h * TQ:(h + 1) * TQ, half * LANES:(half + 1) * LANES]
                    tot += jnp.maximum(blk, 0.0) * wb_ref[h]
                halves.append(tot)
            tot = jnp.concatenate(halves, axis=-1) * ((IDX_DIM ** -0.5) * (IDX_HEADS ** -0.5))
            sc_ref[j] = jnp.where(adm, tot, -jnp.inf)
            mx = jnp.maximum(mx, jnp.max(jnp.where(adm, tot, -jnp.inf), axis=-1, keepdims=True))
            mn = jnp.minimum(mn, jnp.min(jnp.where(adm, tot, jnp.inf), axis=-1, keepdims=True))
            return mx, mn

        mx, mn = lax.fori_loop(0, n_sub, score_tile,
                               (jnp.full((TQ, 1), -jnp.inf, F32), jnp.full((TQ, 1), jnp.inf, F32)))

        def fill_tile(j, c):
            sc_ref[j] = jnp.full((TQ, SUB), -jnp.inf, F32)
            return c

        lax.fori_loop(n_sub, (last_kt + 1) * (KT // SUB), fill_tile, 0)

        n_adm = ((row_chunk + 1) * CHUNK).astype(F32)
        keep_all = n_adm <= float(TOPK)
        lo_ref[...] = jnp.where(keep_all, -F32_MAX, mn)
        hi_ref[...] = mx
        cnt_ref[...] = jnp.where(keep_all, float(TOPK), n_adm)

        RG = 64

        def count_ge(mid):
            parts = []
            for rg in range(TQ // RG):
                midb = jnp.broadcast_to(mid[rg * RG:(rg + 1) * RG], (RG, LANES))

                def body(j, acc, rg=rg, midb=midb):
                    t = sc_ref[j, rg * RG:(rg + 1) * RG, :]
                    for half in range(SUB // LANES):
                        acc = acc + jnp.where(t[:, half * LANES:(half + 1) * LANES] >= midb, 1.0, 0.0)
                    return acc

                acc = lax.fori_loop(0, n_sub, body, jnp.zeros((RG, LANES), F32))
                parts.append(jnp.sum(acc, axis=-1, keepdims=True))
            return jnp.concatenate(parts, axis=0)

        def n_open():
            return jnp.max(jnp.where(cnt_ref[...] != float(TOPK), 1.0, 0.0))

        def cond(c):
            it, open_rows = c
            return jnp.logical_and(it < 48, open_rows > 0.5)

        def body(c):
            it, _ = c
            lo = lo_ref[...]
            hi = hi_ref[...]
            cnt = cnt_ref[...]
            mid = lo + 0.5 * (hi - lo)
            c_mid = count_ge(mid)
            live = cnt != float(TOPK)
            ge = jnp.logical_and(live, c_mid >= float(TOPK))
            lt = jnp.logical_and(live, c_mid < float(TOPK))
            lo_ref[...] = jnp.where(ge, mid, lo)
            cnt_ref[...] = jnp.where(ge, c_mid, cnt)
            hi_ref[...] = jnp.where(lt, mid, hi)
            return it + 1, n_open()

        lax.while_loop(cond, body, (jnp.int32(0), n_open()))
        thr_ref[...] = lo_ref[...]

        m_ref[...] = jnp.full(m_ref.shape, NEG_BIG, F32)
        l_ref[...] = jnp.zeros_like(l_ref)
        acc_ref[...] = jnp.zeros_like(acc_ref)

    @pl.when(kt <= last_kt)
    def _():
        thr = thr_ref[...]
        sel = jnp.concatenate([sc_ref[kt * (KT // SUB) + jj] for jj in range(KT // SUB)], axis=-1) >= thr
        s_idx = kt * KT + lax.broadcasted_iota(jnp.int32, (1, KT), 1)
        dist = jnp.abs(t_row - s_idx).astype(F32)
        for h in range(B_HEADS):
            hs = slice(h * HEAD_DIM, (h + 1) * HEAD_DIM)
            lg = lax.dot_general(qs_ref[:, hs], k_ref[:, hs], (((1,), (1,)), ((), ())),
                                 preferred_element_type=F32)
            slope = 2.0 ** (-8.0 * (h + 1) / B_HEADS)
            lg = jnp.where(sel, lg - slope * dist, NEG_BIG)
            m_old = m_ref[h]
            m_new = jnp.maximum(m_old, jnp.max(lg, axis=-1, keepdims=True))
            alpha = jnp.exp(m_old - m_new)
            p = jnp.exp(lg - m_new)
            l_ref[h] = alpha * l_ref[h] + jnp.sum(p, axis=-1, keepdims=True)
            acc_ref[:, hs] = alpha * acc_ref[:, hs] + jnp.dot(p.astype(BF16), v_ref[:, hs],
                                                              preferred_element_type=F32)
            m_ref[h] = m_new

    @pl.when(kt == last_kt)
    def _():
        outs = []
        for h in range(B_HEADS):
            hs = slice(h * HEAD_DIM, (h + 1) * HEAD_DIM)
            outs.append(acc_ref[:, hs] / l_ref[h])
        o_ref[...] = jnp.concatenate(outs, axis=-1).astype(o_ref.dtype)


def _dsa_attention(proj, qi_blocks, k, v):
    TQ, KT = DSA_TQ, DSA_KT
    nqb, nkt = SEQ // TQ, SEQ // KT
    kw_block = (2 * D_A + IDX_HEADS * IDX_DIM + KV_RANK) // LANES

    def kv_map(i, kt):
        return (jnp.minimum(kt, (i * TQ + TQ - 1) // KT), 0)

    return pl.pallas_call(
        _dsa_kernel,
        grid=(nqb, nkt),
        in_specs=[
            pl.BlockSpec((TQ, D_B), lambda i, kt: (i, 1)),
            pl.BlockSpec((1, IDX_HEADS * TQ, IDX_DIM), lambda i, kt: (i, 0, 0)),
            pl.BlockSpec((TQ, LANES), lambda i, kt: (i, kw_block)),
            pl.BlockSpec((SEQ, LANES), lambda i, kt: (0, kw_block)),
            pl.BlockSpec((KT, D_B), kv_map),
            pl.BlockSpec((KT, D_B), kv_map),
        ],
        out_specs=pl.BlockSpec((TQ, D_B), lambda i, kt: (i, 0)),
        out_shape=jax.ShapeDtypeStruct((SEQ, D_B), BF16),
        scratch_shapes=[
            pltpu.VMEM((SEQ // DSA_SUB, TQ, DSA_SUB), F32),
            pltpu.VMEM((IDX_HEADS, TQ, LANES), F32),
            pltpu.VMEM((TQ, D_B), BF16),
            pltpu.VMEM((TQ, 1), F32),
            pltpu.VMEM((TQ, 1), F32),
            pltpu.VMEM((TQ, 1), F32),
            pltpu.VMEM((TQ, 1), F32),
            pltpu.VMEM((B_HEADS, TQ, 1), F32),
            pltpu.VMEM((B_HEADS, TQ, 1), F32),
            pltpu.VMEM((TQ, D_B), F32),
        ],
        compiler_params=_cparams(("arbitrary", "arbitrary")),
        name="dsa_attention",
    )(proj, qi_blocks, proj, proj, k, v)


def _ca_kernel(q_ref, k0_ref, k1_ref, k2_ref, v0_ref, v1_ref, v2_ref, bias_ref, o_ref):
    i = pl.program_id(1)
    k_refs = (k0_ref, k1_ref, k2_ref)
    v_refs = (v0_ref, v1_ref, v2_ref)
    outs = []
    for hh in range(CA_HB):
        hs = slice(hh * HEAD_DIM, (hh + 1) * HEAD_DIM)
        qh = q_ref[:, hs]
        lgs = []
        for j in range(CA_NKB):
            lg = lax.dot_general(qh, k_refs[j][:, hs], (((1,), (1,)), ((), ())), preferred_element_type=F32)
            lg = lg * (HEAD_DIM ** -0.5) + bias_ref[hh, :, j * CA_TQ:(j + 1) * CA_TQ]
            lgs.append(jnp.where(i + j >= CA_NKB - 1, lg, NEG_BIG))
        lg = jnp.concatenate(lgs, axis=-1)
        m = jnp.max(lg, axis=-1, keepdims=True)
        p = jnp.exp(lg - m)
        l = jnp.sum(p, axis=-1, keepdims=True)
        pb = p.astype(BF16)
        acc = jnp.dot(pb[:, :CA_TQ], v_refs[0][:, hs], preferred_element_type=F32)
        for j in range(1, CA_NKB):
            acc += jnp.dot(pb[:, j * CA_TQ:(j + 1) * CA_TQ], v_refs[j][:, hs], preferred_element_type=F32)
        outs.append(acc / l)
    o_ref[...] = jnp.concatenate(outs, axis=-1).astype(o_ref.dtype)


def _ca_bias_table(rel_bias):
    n_heads = rel_bias.shape[0]
    width = CA_NKB * CA_TQ
    span = width + CA_TQ - 1
    period = span + 1
    v = jnp.concatenate([rel_bias.astype(F32)[:, MAX_REL - CA_TQ + 1:],
                         jnp.broadcast_to(rel_bias.astype(F32)[:, -1:], (n_heads, span - MAX_REL - CA_TQ))], axis=1)
    w = jnp.pad(v[:, ::-1], ((0, 0), (0, 1)))
    skew = jnp.tile(w, (1, CA_TQ))[:, :CA_TQ * (period - 1)].reshape(n_heads, CA_TQ, period - 1)
    tab = skew[:, :, CA_TQ - 1:CA_TQ - 1 + width]
    r = np.arange(CA_TQ)[:, None]
    c = np.arange(width)[None, :]
    kc = c // CHUNK
    qc = r // CHUNK + (CA_NKB - 1) * CA_TQ // CHUNK
    band = (kc >= qc - C_LEFT_CHUNKS) & (kc <= qc)
    return jnp.where(band[None], tab, NEG_BIG)


def _chunk_attention(qkv, bias_tab):
    nt = SEQ // CA_TQ
    hw = CA_HB * HEAD_DIM
    nhb = C_HEADS // CA_HB

    def kmap(j, base):
        return lambda hb, i: (jnp.maximum(i - (CA_NKB - 1) + j, 0), base + hb)

    in_specs = [pl.BlockSpec((CA_TQ, hw), lambda hb, i: (i, hb))]
    in_specs += [pl.BlockSpec((CA_TQ, hw), kmap(j, nhb)) for j in range(CA_NKB)]
    in_specs += [pl.BlockSpec((CA_TQ, hw), kmap(j, 2 * nhb)) for j in range(CA_NKB)]
    in_specs += [pl.BlockSpec((CA_HB, CA_TQ, CA_NKB * CA_TQ), lambda hb, i: (hb, 0, 0))]
    return pl.pallas_call(
        _ca_kernel,
        grid=(nhb, nt),
        in_specs=in_specs,
        out_specs=pl.BlockSpec((CA_TQ, hw), lambda hb, i: (i, hb)),
        out_shape=jax.ShapeDtypeStruct((SEQ, D_MODEL), BF16),
        compiler_params=_cparams(("arbitrary", "arbitrary")),
        name="chunk_attention",
    )(qkv, qkv, qkv, qkv, qkv, qkv, qkv, bias_tab)


def _pad_cols(w, n):
    return jnp.pad(w, ((0, 0), (0, n - w.shape[1])))


def kernel(x, c, ada_w, ada_b, norm_g, ffn_w_gate, ffn_w_up, ffn_w_down, ab_w_in, s5_lam_re, s5_lam_im, s5_log_dt, s5_b_re, s5_b_im, s5_c_re, s5_c_im, s5_d, s5_w_glu, s5_b_glu, dsa_kv_norm_g, dsa_w_kv_up, ab_w_out, c_w_qkv, c_rel_bias, c_w_out, final_norm_g):
    mod = _modulation(c, ada_w, ada_b)
    h = x.reshape(SEQ, D_MODEL)
    wg, wu, wd = ffn_w_gate.astype(BF16), ffn_w_up.astype(BF16), ffn_w_down.astype(BF16)

    for layer in range(DEPTH):
        h = _ffn(h, norm_g[layer, 0], mod[layer, 0], wg, wu, wd, layer, 0)
        if layer % 2 == 0:
            e = layer // 2
            w_in = ab_w_in[e]
            o_q, o_kv, o_qi = D_A, D_A + D_B, D_A + D_B + KV_RANK
            o_ki = o_qi + IDX_HEADS * IDX_DIM
            w_in = jnp.concatenate([w_in[:, :o_kv], w_in[:, o_qi:o_ki], w_in[:, o_kv:o_qi], w_in[:, o_ki:]], axis=1)
            proj = _norm_proj(h, norm_g[layer, 1], mod[layer, 1], _pad_cols(w_in, AB_N_PAD).astype(BF16), F32)

            a_re, a_im, bb_re, bb_im = _s5_params(s5_lam_re[e], s5_lam_im[e], s5_log_dt[e], s5_b_re[e], s5_b_im[e])
            y_a = _s5_mixer(proj, a_re, a_im, bb_re, bb_im, s5_c_re[e], s5_c_im[e], s5_d[e],
                            s5_w_glu[e].astype(BF16), s5_b_glu[e])

            k, v = _kv_up(proj, dsa_kv_norm_g[e], dsa_w_kv_up[e].astype(BF16))
            q_idx = proj[:, 2 * D_A:2 * D_A + IDX_HEADS * IDX_DIM]
            qi_blocks = (q_idx.reshape(SEQ // DSA_TQ, DSA_TQ, IDX_HEADS, IDX_DIM).transpose(0, 2, 1, 3)
                         .reshape(SEQ // DSA_TQ, IDX_HEADS * DSA_TQ, IDX_DIM).astype(BF16))
            y_b = _dsa_attention(proj, qi_blocks, k, v)

            w_out = ab_w_out[e].astype(BF16)
            h = _out_proj([y_a, y_b], [w_out[:D_A], w_out[D_A:]], h, mod[layer, 1])
        else:
            o = layer // 2
            qkv = _norm_proj(h, norm_g[layer, 1], mod[layer, 1], c_w_qkv[o].astype(BF16), BF16)
            att = _chunk_attention(qkv, _ca_bias_table(c_rel_bias[o]))
            h = _out_proj([att], [c_w_out[o].astype(BF16)], h, mod[layer, 1])
        h = _ffn(h, norm_g[layer, 2], mod[layer, 2], wg, wu, wd, layer, 1,
                 final_g=final_norm_g if layer == DEPTH - 1 else None)
    return h.reshape(1, SEQ, D_MODEL)
```

```python
import functools
import math

import jax
import jax.numpy as jnp
import numpy as np
from jax import lax
from jax.experimental import pallas as pl
from jax.experimental.pallas import tpu as pltpu

F32 = jnp.float32
BF16 = jnp.bfloat16

D_MODEL = 2048
SEQ = 8192
DEPTH = 2
CHUNK = 64
HEAD_DIM = 128
D_FF = 5504
N_SUB = 3
EPS = 1e-6
D_A = D_MODEL // 2
S5_GROUP = 16
S5_GROUPS = D_A // S5_GROUP
S5_STATE = 64
D_B = D_MODEL // 2
B_HEADS = D_B // HEAD_DIM
KV_RANK = D_MODEL // 8
IDX_HEADS = 16
IDX_DIM = 64
TOPK = 256
C_HEADS = D_MODEL // HEAD_DIM
C_LEFT_CHUNKS = 8
MAX_REL = 256

LANES = 128
SUBLANES = 8
VMEM_LIMIT = 56 * 1024 * 1024
NEG_BIG = -1e30
LOG2E = math.log2(math.e)
F32_MAX = float(np.finfo(np.float32).max)

FFN_TM = 512
FFN_TF = 512
PROJ_TM = 1024
PROJ_TN = 512
OUT_TM = 512
AB_N_PAD = 3584
S5_T = 256
S5_NSEG = SUBLANES
S5_SLICES = D_A // LANES
S5_NSTATE = S5_GROUPS * S5_STATE
DSA_TQ = 256
DSA_KT = 1024
DSA_SUB = 256
CA_TQ = 256
CA_HB = 4
CA_NKB = 3


def _cparams(sem):
    return pltpu.CompilerParams(dimension_semantics=sem, vmem_limit_bytes=VMEM_LIMIT)


def _mod_kernel(c_ref, w_ref, b_ref, o_ref):
    rows = 256
    tn = o_ref.shape[-1]

    def body(i, acc):
        r0 = pl.multiple_of(i * rows, rows)
        cc = c_ref[pl.ds(r0, rows), :]
        cc = cc * jax.nn.sigmoid(cc)
        w = w_ref[0, pl.ds(r0, rows), :]
        return acc + jnp.sum((w * cc).reshape(rows // SUBLANES, SUBLANES, tn), axis=0)

    acc = lax.fori_loop(0, D_MODEL // rows, body, jnp.zeros((SUBLANES, tn), F32))
    o_ref[0] = jnp.sum(acc, axis=0, keepdims=True) + b_ref[0]


def _modulation(c, ada_w, ada_b):
    n = N_SUB * 3 * D_MODEL
    tn = 1024
    c_col = c.reshape(D_MODEL, 1)
    out = pl.pallas_call(
        _mod_kernel,
        grid=(DEPTH, n // tn),
        in_specs=[
            pl.BlockSpec((D_MODEL, 1), lambda l, j: (0, 0)),
            pl.BlockSpec((1, D_MODEL, tn), lambda l, j: (l, 0, j)),
            pl.BlockSpec((1, 1, tn), lambda l, j: (l, 0, j)),
        ],
        out_specs=pl.BlockSpec((1, 1, tn), lambda l, j: (l, 0, j)),
        out_shape=jax.ShapeDtypeStruct((DEPTH, 1, n), F32),
        compiler_params=_cparams(("arbitrary", "arbitrary")),
        name="adaln_mod",
    )(c_col, ada_w, ada_b.reshape(DEPTH, 1, n))
    return out.reshape(DEPTH, N_SUB, 3, D_MODEL)


def _adaln(x, g, mod):
    ms = jnp.mean(x * x, axis=-1, keepdims=True)
    y = (x * lax.rsqrt(ms + EPS)) * g
    return y * (1.0 + mod[1:2, :]) + mod[0:1, :]


def _ffn_kernel(h_ref, g_ref, mod_ref, wg_ref, wu_ref, wd_ref, *rest, n_f, final):
    if final:
        fg_ref, o_ref, hn_ref, acc_ref = rest
    else:
        o_ref, hn_ref, acc_ref = rest
    f = pl.program_id(1)

    @pl.when(f == 0)
    def _():
        hn_ref[...] = _adaln(h_ref[...], g_ref[...], mod_ref[...]).astype(BF16)
        acc_ref[...] = jnp.zeros_like(acc_ref)

    def accumulate(width):
        hn = hn_ref[...]
        gate = jnp.dot(hn, wg_ref[:, :width], preferred_element_type=F32)
        up = jnp.dot(hn, wu_ref[:, :width], preferred_element_type=F32)
        act = (gate * jax.nn.sigmoid(gate)) * up
        acc_ref[...] += jnp.dot(act.astype(BF16), wd_ref[:width, :], preferred_element_type=F32)

    @pl.when(f < n_f - 1)
    def _():
        accumulate(FFN_TF)

    @pl.when(f == n_f - 1)
    def _():
        accumulate(D_FF - (n_f - 1) * FFN_TF)
        out = h_ref[...] + (0.5 * mod_ref[2:3, :]) * acc_ref[...]
        if final:
            ms = jnp.mean(out * out, axis=-1, keepdims=True)
            out = (out * lax.rsqrt(ms + EPS)) * fg_ref[...]
        o_ref[...] = out


def _ffn(h, g, mod, wg, wu, wd, layer, which, final_g=None):
    n_f = pl.cdiv(D_FF, FFN_TF)
    final = final_g is not None
    in_specs = [
        pl.BlockSpec((FFN_TM, D_MODEL), lambda i, f: (i, 0)),
        pl.BlockSpec((1, D_MODEL), lambda i, f: (0, 0)),
        pl.BlockSpec((3, D_MODEL), lambda i, f: (0, 0)),
        pl.BlockSpec((None, None, D_MODEL, FFN_TF), lambda i, f: (layer, which, 0, f)),
        pl.BlockSpec((None, None, D_MODEL, FFN_TF), lambda i, f: (layer, which, 0, f)),
        pl.BlockSpec((None, None, FFN_TF, D_MODEL), lambda i, f: (layer, which, f, 0)),
    ]
    args = [h, g.reshape(1, D_MODEL), mod, wg, wu, wd]
    if final:
        in_specs.append(pl.BlockSpec((1, D_MODEL), lambda i, f: (0, 0)))
        args.append(final_g.reshape(1, D_MODEL))
    return pl.pallas_call(
        functools.partial(_ffn_kernel, n_f=n_f, final=final),
        grid=(SEQ // FFN_TM, n_f),
        in_specs=in_specs,
        out_specs=pl.BlockSpec((FFN_TM, D_MODEL), lambda i, f: (i, 0)),
        out_shape=jax.ShapeDtypeStruct((SEQ, D_MODEL), F32),
        scratch_shapes=[pltpu.VMEM((FFN_TM, D_MODEL), BF16), pltpu.VMEM((FFN_TM, D_MODEL), F32)],
        compiler_params=_cparams(("arbitrary", "arbitrary")),
        name="ffn_swiglu",
    )(*args)


def _proj_kernel(h_ref, g_ref, mod_ref, w_ref, o_ref, hn_ref):
    @pl.when(pl.program_id(1) == 0)
    def _():
        hn_ref[...] = _adaln(h_ref[...], g_ref[...], mod_ref[...]).astype(BF16)

    o_ref[...] = jnp.dot(hn_ref[...], w_ref[...], preferred_element_type=F32).astype(o_ref.dtype)


def _norm_proj(h, g, mod, w, out_dtype):
    n = w.shape[1]
    return pl.pallas_call(
        _proj_kernel,
        grid=(SEQ // PROJ_TM, n // PROJ_TN),
        in_specs=[
            pl.BlockSpec((PROJ_TM, D_MODEL), lambda i, j: (i, 0)),
            pl.BlockSpec((1, D_MODEL), lambda i, j: (0, 0)),
            pl.BlockSpec((3, D_MODEL), lambda i, j: (0, 0)),
            pl.BlockSpec((D_MODEL, PROJ_TN), lambda i, j: (0, j)),
        ],
        out_specs=pl.BlockSpec((PROJ_TM, PROJ_TN), lambda i, j: (i, j)),
        out_shape=jax.ShapeDtypeStruct((SEQ, n), out_dtype),
        scratch_shapes=[pltpu.VMEM((PROJ_TM, D_MODEL), BF16)],
        compiler_params=_cparams(("arbitrary", "arbitrary")),
        name="adaln_proj",
    )(h, g.reshape(1, D_MODEL), mod, w)


def _out_kernel(*refs, n_lhs):
    lhs = refs[:n_lhs]
    ws = refs[n_lhs:2 * n_lhs]
    h_ref, mod_ref, o_ref = refs[2 * n_lhs:]
    y = jnp.dot(lhs[0][...], ws[0][...], preferred_element_type=F32)
    for a_ref, w_ref in zip(lhs[1:], ws[1:]):
        y += jnp.dot(a_ref[...], w_ref[...], preferred_element_type=F32)
    o_ref[...] = h_ref[...] + mod_ref[2:3, :] * y


def _out_proj(lhs, ws, h, mod):
    n_lhs = len(lhs)
    in_specs = [pl.BlockSpec((OUT_TM, a.shape[1]), lambda i: (i, 0)) for a in lhs]
    in_specs += [pl.BlockSpec(w.shape, lambda i: (0, 0)) for w in ws]
    in_specs += [pl.BlockSpec((OUT_TM, D_MODEL), lambda i: (i, 0)), pl.BlockSpec((3, D_MODEL), lambda i: (0, 0))]
    return pl.pallas_call(
        functools.partial(_out_kernel, n_lhs=n_lhs),
        grid=(SEQ // OUT_TM,),
        in_specs=in_specs,
        out_specs=pl.BlockSpec((OUT_TM, D_MODEL), lambda i: (i, 0)),
        out_shape=jax.ShapeDtypeStruct((SEQ, D_MODEL), F32),
        compiler_params=_cparams(("arbitrary",)),
        name="out_proj_residual",
    )(*lhs, *ws, h, mod)


def _s5_param_kernel(lr_ref, li_ref, ldt_ref, br_ref, bi_ref, are_ref, aim_ref, bbr_ref, bbi_ref):
    lr = lr_ref[...]
    li = li_ref[...]
    dt = jnp.exp(ldt_ref[...])
    mag = jnp.exp(lr * dt)
    ab_re = mag * jnp.cos(li * dt)
    ab_im = mag * jnp.sin(li * dt)
    den = lr * lr + li * li
    nr = ab_re - 1.0
    f_re = (nr * lr + ab_im * li) / den
    f_im = (ab_im * lr - nr * li) / den
    are_ref[...] = ab_re
    aim_ref[...] = ab_im
    br = br_ref[...]
    bi = bi_ref[...]
    bbr_ref[...] = f_re * br - f_im * bi
    bbi_ref[...] = f_re * bi + f_im * br


def _s5_params(lam_re, lam_im, log_dt, b_re, b_im):
    G, P, CG = S5_GROUPS, S5_STATE, S5_GROUP
    return pl.pallas_call(
        _s5_param_kernel,
        out_shape=[jax.ShapeDtypeStruct((G, 1, P), F32), jax.ShapeDtypeStruct((G, 1, P), F32),
                   jax.ShapeDtypeStruct((G, CG, P), F32), jax.ShapeDtypeStruct((G, CG, P), F32)],
        name="s5_zoh_params",
    )(lam_re.reshape(G, 1, P), lam_im.reshape(G, 1, P), log_dt.reshape(G, 1, 1),
      jnp.swapaxes(b_re, 1, 2), jnp.swapaxes(b_im, 1, 2))


def _block_diag_slices(m):
    _, r, c = m.shape
    m4 = m.reshape(S5_SLICES, SUBLANES, r, c)
    eye = jnp.eye(SUBLANES, dtype=m.dtype)
    out = m4[:, :, :, None, :] * eye[None, :, None, :, None]
    return out.reshape(S5_SLICES, SUBLANES * r, SUBLANES * c)


def _s5_kernel(u_ref, bbr_ref, bbi_ref, are_ref, aim_ref, ccr_ref, cci_ref, d_ref, wglu_ref, bglu_ref,
               o_ref, xr_ref, xi_ref, pr_ref, pi_ref, er_ref, ei_ref, cr_ref, ci_ref):
    W = S5_NSTATE // S5_SLICES
    TS = S5_T // S5_NSEG
    CW = 512

    @pl.when(pl.program_id(0) == 0)
    def _():
        pr_ref[0:1, :] = are_ref[...]
        pi_ref[0:1, :] = aim_ref[...]

        def power(s, c):
            ar, ai = are_ref[...], aim_ref[...]
            qr, qi = pr_ref[pl.ds(s - 1, 1), :], pi_ref[pl.ds(s - 1, 1), :]
            pr_ref[pl.ds(s, 1), :] = ar * qr - ai * qi
            pi_ref[pl.ds(s, 1), :] = ar * qi + ai * qr
            return c

        lax.fori_loop(1, TS, power, 0)
        cr_ref[...] = jnp.zeros_like(cr_ref)
        ci_ref[...] = jnp.zeros_like(ci_ref)

    for k in range(S5_SLICES):
        uk = u_ref[:, k * LANES:(k + 1) * LANES].astype(BF16)
        xr_ref[:, k * W:(k + 1) * W] = jnp.dot(uk, bbr_ref[k], preferred_element_type=F32)
        xi_ref[:, k * W:(k + 1) * W] = jnp.dot(uk, bbi_ref[k], preferred_element_type=F32)

    for cg in range(S5_NSTATE // CW):
        cols = slice(cg * CW, (cg + 1) * CW)
        ar = jnp.broadcast_to(are_ref[:, cols], (S5_NSEG, CW))
        ai = jnp.broadcast_to(aim_ref[:, cols], (S5_NSEG, CW))

        def step(s, carry, cols=cols, ar=ar, ai=ai):
            sr, si = carry
            rows = pl.ds(pl.multiple_of(s * S5_NSEG, S5_NSEG), S5_NSEG)
            nr = ar * sr - ai * si + xr_ref[rows, cols]
            ni = ar * si + ai * sr + xi_ref[rows, cols]
            xr_ref[rows, cols] = nr
            xi_ref[rows, cols] = ni
            return nr, ni

        zero = jnp.zeros((S5_NSEG, CW), F32)
        er, ei = lax.fori_loop(0, TS, step, (zero, zero), unroll=4)
        er_ref[:, cols] = er
        ei_ref[:, cols] = ei

    pwr, pwi = pr_ref[TS - 1:TS, :], pi_ref[TS - 1:TS, :]
    c_r, c_i = cr_ref[S5_NSEG:S5_NSEG + 1, :], ci_ref[S5_NSEG:S5_NSEG + 1, :]
    for j in range(S5_NSEG):
        cr_ref[j:j + 1, :] = c_r
        ci_ref[j:j + 1, :] = c_i
        c_r, c_i = (er_ref[j:j + 1, :] + pwr * c_r - pwi * c_i,
                    ei_ref[j:j + 1, :] + pwr * c_i + pwi * c_r)
    cr_ref[S5_NSEG:S5_NSEG + 1, :] = c_r
    ci_ref[S5_NSEG:S5_NSEG + 1, :] = c_i

    for cg in range(S5_NSTATE // CW):
        cols = slice(cg * CW, (cg + 1) * CW)
        c_r, c_i = cr_ref[0:S5_NSEG, cols], ci_ref[0:S5_NSEG, cols]

        def fix(s, carry, cols=cols, c_r=c_r, c_i=c_i):
            rows = pl.ds(pl.multiple_of(s * S5_NSEG, S5_NSEG), S5_NSEG)
            p_r, p_i = pr_ref[pl.ds(s, 1), cols], pi_ref[pl.ds(s, 1), cols]
            xr_ref[rows, cols] = xr_ref[rows, cols] + (p_r * c_r - p_i * c_i)
            xi_ref[rows, cols] = xi_ref[rows, cols] + (p_r * c_i + p_i * c_r)
            return carry

        lax.fori_loop(0, TS, fix, 0, unroll=4)

    ys = []
    for k in range(S5_SLICES):
        xr = xr_ref[:, k * W:(k + 1) * W].astype(BF16)
        xi = xi_ref[:, k * W:(k + 1) * W].astype(BF16)
        ys.append(jnp.dot(xr, ccr_ref[k], preferred_element_type=F32)
                  + jnp.dot(xi, cci_ref[k], preferred_element_type=F32))
    y = jnp.concatenate(ys, axis=-1) + d_ref[...] * u_ref[...]
    y = jax.nn.gelu(y, approximate=True)
    z = jnp.dot(y.astype(BF16), wglu_ref[...], preferred_element_type=F32) + bglu_ref[...]
    o_ref[...] = (y * jax.nn.sigmoid(z)).astype(o_ref.dtype)


def _s5_mixer(proj, a_re, a_im, bb_re, bb_im, c_re, c_im, d_skip, w_glu, b_glu):
    W = S5_NSTATE // S5_SLICES
    bbr = _block_diag_slices(bb_re).astype(BF16)
    bbi = _block_diag_slices(bb_im).astype(BF16)
    ccr = _block_diag_slices(jnp.swapaxes(c_re, 1, 2)).astype(BF16)
    cci = _block_diag_slices(-jnp.swapaxes(c_im, 1, 2)).astype(BF16)
    const3 = lambda t: (0, 0, 0)
    const2 = lambda t: (0, 0)
    return pl.pallas_call(
        _s5_kernel,
        grid=(SEQ // S5_T,),
        in_specs=[
            pl.BlockSpec((S5_T, D_A), lambda t: (t, 0)),
            pl.BlockSpec((S5_SLICES, LANES, W), const3),
            pl.BlockSpec((S5_SLICES, LANES, W), const3),
            pl.BlockSpec((1, S5_NSTATE), const2),
            pl.BlockSpec((1, S5_NSTATE), const2),
            pl.BlockSpec((S5_SLICES, W, LANES), const3),
            pl.BlockSpec((S5_SLICES, W, LANES), const3),
            pl.BlockSpec((1, D_A), const2),
            pl.BlockSpec((D_A, D_A), const2),
            pl.BlockSpec((1, D_A), const2),
        ],
        out_specs=pl.BlockSpec((S5_T, D_A), lambda t: (t, 0)),
        out_shape=jax.ShapeDtypeStruct((SEQ, D_A), BF16),
        scratch_shapes=[pltpu.VMEM((S5_T, S5_NSTATE), F32), pltpu.VMEM((S5_T, S5_NSTATE), F32),
                        pltpu.VMEM((S5_T // S5_NSEG, S5_NSTATE), F32), pltpu.VMEM((S5_T // S5_NSEG, S5_NSTATE), F32),
                        pltpu.VMEM((S5_NSEG, S5_NSTATE), F32), pltpu.VMEM((S5_NSEG, S5_NSTATE), F32),
                        pltpu.VMEM((2 * S5_NSEG, S5_NSTATE), F32), pltpu.VMEM((2 * S5_NSEG, S5_NSTATE), F32)],
        compiler_params=_cparams(("arbitrary",)),
        name="s5_mixer",
    )(proj, bbr, bbi, a_re.reshape(1, S5_NSTATE), a_im.reshape(1, S5_NSTATE), ccr, cci,
      d_skip.reshape(1, D_A), w_glu, b_glu.reshape(1, D_A))


def _kv_kernel(lat_ref, g_ref, wk_ref, wvt_ref, k_ref, vt_ref):
    x = lat_ref[...]
    ms = jnp.mean(x * x, axis=-1, keepdims=True)
    xn = ((x * lax.rsqrt(ms + EPS)) * g_ref[...]).astype(BF16)
    k_ref[...] = jnp.dot(xn, wk_ref[...], preferred_element_type=F32).astype(k_ref.dtype)
    for j in range(vt_ref.shape[0]):
        vt = lax.dot_general(wvt_ref[...], xn[j * DSA_SUB:(j + 1) * DSA_SUB], (((1,), (1,)), ((), ())),
                             preferred_element_type=F32)
        vt_ref[j] = vt.astype(vt_ref.dtype)


def _kv_up(proj, g, w):
    tm = DSA_KT
    lat_block = (2 * D_A + IDX_HEADS * IDX_DIM) // KV_RANK
    wk = w[:, :D_B]
    wvt = w[:, D_B:].T
    return pl.pallas_call(
        _kv_kernel,
        grid=(SEQ // tm,),
        in_specs=[
            pl.BlockSpec((tm, KV_RANK), lambda i: (i, lat_block)),
            pl.BlockSpec((1, KV_RANK), lambda i: (0, 0)),
            pl.BlockSpec((KV_RANK, D_B), lambda i: (0, 0)),
            pl.BlockSpec((D_B, KV_RANK), lambda i: (0, 0)),
        ],
        out_specs=[pl.BlockSpec((tm, D_B), lambda i: (i, 0)),
                   pl.BlockSpec((tm // DSA_SUB, D_B, DSA_SUB), lambda i: (i, 0, 0))],
        out_shape=[jax.ShapeDtypeStruct((SEQ, D_B), BF16),
                   jax.ShapeDtypeStruct((SEQ // DSA_SUB, D_B, DSA_SUB), BF16)],
        compiler_params=_cparams(("arbitrary",)),
        name="dsa_kv_up",
    )(proj, g.reshape(1, KV_RANK), wk, wvt)


def _dsa_kernel(q_ref, qidx_ref, w_ref, kidx_ref, k_ref, vt_ref, o_ref,
                sc_ref, a2_ref, mb_ref, lg_ref, pb_ref, qs_ref, qi_ref, wt_ref, thr_ref, lo_ref, hi_ref, clo_ref,
                chi_ref, m_ref, l_ref, acc_ref):
    TQ, KT, SUB = DSA_TQ, DSA_KT, DSA_SUB
    K = float(TOPK)
    i = pl.program_id(0)
    kt = pl.program_id(1)
    last_kt = (i * TQ + TQ - 1) // KT
    n_sub = i + 1
    q_pos = i * TQ + lax.broadcasted_iota(jnp.int32, (1, TQ), 1)
    q_chunk = q_pos // CHUNK

    @pl.when(jnp.logical_and(i == 0, kt == 0))
    def _():
        d = (lax.broadcasted_iota(jnp.int32, (KT, TQ), 1) - lax.broadcasted_iota(jnp.int32, (KT, TQ), 0)).astype(F32)
        for h in range(B_HEADS):
            a2_ref[h] = (LOG2E * 2.0 ** (-8.0 * (h + 1) / B_HEADS)) * d

    @pl.when(kt == 0)
    def _():
        qs_ref[...] = (q_ref[...] * (HEAD_DIM ** -0.5 * LOG2E)).astype(BF16)
        qv = qidx_ref[...]
        for h in range(IDX_HEADS):
            qi_ref[h] = qv[:, h * IDX_DIM:(h + 1) * IDX_DIM].astype(BF16)
        wt_ref[...] = w_ref[...].T[IDX_DIM:IDX_DIM + IDX_HEADS, :]

        def scores(g):
            s0 = pl.multiple_of(g * SUB, SUB)
            kk = kidx_ref[pl.ds(s0, SUB), :IDX_DIM].astype(BF16)
            tot = jnp.zeros((SUB, TQ), F32)
            for h in range(IDX_HEADS):
                s = lax.dot_general(kk, qi_ref[h], (((1,), (1,)), ((), ())), preferred_element_type=F32)
                tot = tot + jnp.maximum(s, 0.0) * wt_ref[h:h + 1, :]
            return tot * ((IDX_DIM ** -0.5) * (IDX_HEADS ** -0.5))

        def past_tile(g, carry):
            mx, mn = carry
            tot = scores(g)
            sc_ref[g] = tot
            return (jnp.maximum(mx, jnp.max(tot, axis=0, keepdims=True)),
                    jnp.minimum(mn, jnp.min(tot, axis=0, keepdims=True)))

        mx, mn = lax.fori_loop(0, i, past_tile,
                               (jnp.full((1, TQ), -jnp.inf, F32), jnp.full((1, TQ), jnp.inf, F32)))
        tot = scores(i)
        key_chunk = (i * SUB + lax.broadcasted_iota(jnp.int32, (SUB, 1), 0)) // CHUNK
        adm = key_chunk <= q_chunk
        sc_ref[i] = jnp.where(adm, tot, -jnp.inf)
        mx = jnp.maximum(mx, jnp.max(jnp.where(adm, tot, -jnp.inf), axis=0, keepdims=True))
        mn = jnp.minimum(mn, jnp.min(jnp.where(adm, tot, jnp.inf), axis=0, keepdims=True))

        def fill_tile(g, c):
            sc_ref[g] = jnp.full((SUB, TQ), -jnp.inf, F32)
            return c

        lax.fori_loop(n_sub, (last_kt + 1) * (KT // SUB), fill_tile, 0)

        n_adm = ((q_chunk + 1) * CHUNK).astype(F32)
        keep_all = n_adm <= K
        lo_ref[...] = jnp.where(keep_all, -F32_MAX, mn)
        hi_ref[...] = mx
        clo_ref[...] = jnp.where(keep_all, K, n_adm)
        chi_ref[...] = jnp.ones((1, TQ), F32)

        def count_ge(mid):
            def body(g, acc):
                ind = jnp.where(sc_ref[g] >= mid, 1.0, 0.0)
                return acc + jnp.sum(ind.reshape(SUB // SUBLANES, SUBLANES, TQ), axis=0)

            acc = lax.fori_loop(0, n_sub, body, jnp.zeros((SUBLANES, TQ), F32))
            return jnp.sum(acc, axis=0, keepdims=True)

        def n_open():
            return jnp.max(jnp.where(clo_ref[...] != K, 1.0, 0.0))

        def cond(c):
            it, open_rows = c
            return jnp.logical_and(it < 64, open_rows > 0.5)

        def body(c):
            it, _ = c
            lo, hi, clo, chi = lo_ref[...], hi_ref[...], clo_ref[...], chi_ref[...]
            frac = jnp.clip((clo - K + 0.5) / (clo - chi), 1.0 / 64, 63.0 / 64)
            frac = jnp.where(jnp.bitwise_and(it, 1) == 0, frac, 0.5)
            mid = lo + frac * (hi - lo)
            c_mid = count_ge(mid)
            live = clo != K
            ge = jnp.logical_and(live, c_mid >= K)
            lt = jnp.logical_and(live, c_mid < K)
            lo_ref[...] = jnp.where(ge, mid, lo)
            clo_ref[...] = jnp.where(ge, c_mid, clo)
            hi_ref[...] = jnp.where(lt, mid, hi)
            chi_ref[...] = jnp.where(lt, c_mid, chi)
            return it + 1, n_open()

        lax.while_loop(cond, body, (jnp.int32(0), n_open()))
        thr_ref[...] = lo_ref[...]

        m_ref[...] = jnp.full(m_ref.shape, NEG_BIG, F32)
        l_ref[...] = jnp.zeros_like(l_ref)
        acc_ref[...] = jnp.zeros_like(acc_ref)

    def attend(last):
        nj = KT // SUB
        thr = thr_ref[...]
        for jj in range(nj):
            mb_ref[jj * SUB:(jj + 1) * SUB, :] = jnp.where(sc_ref[kt * nj + jj] >= thr, 0.0, NEG_BIG)
        gap = (i * TQ - kt * KT).astype(F32)
        m_all, l_all = m_ref[...], l_ref[...]
        m_rows, l_rows = [], []

        def qk(h):
            hs = slice(h * HEAD_DIM, (h + 1) * HEAD_DIM)
            lg_ref[h % 2] = lax.dot_general(k_ref[:, hs], qs_ref[:, hs], (((1,), (1,)), ((), ())),
                                            preferred_element_type=F32)

        qk(0)
        for h in range(B_HEADS):
            if h + 1 < B_HEADS:
                qk(h + 1)
            hs = slice(h * HEAD_DIM, (h + 1) * HEAD_DIM)
            slope2 = LOG2E * 2.0 ** (-8.0 * (h + 1) / B_HEADS)
            if last:
                lg = (lg_ref[h % 2] - jnp.abs(a2_ref[h] + slope2 * gap)) + mb_ref[...]
                off = 0.0
            else:
                lg = (lg_ref[h % 2] - a2_ref[h]) + mb_ref[...]
                off = slope2 * gap
            lg_ref[h % 2] = lg
            m_old = m_all[h:h + 1, :]
            m_new = jnp.maximum(m_old, jnp.max(lg, axis=0, keepdims=True) - off)
            alpha = jnp.exp2(m_old - m_new)
            p = jnp.exp2(lg_ref[h % 2] - (m_new + off))
            l_rows.append(alpha * l_all[h:h + 1, :] + jnp.sum(p, axis=0, keepdims=True))
            m_rows.append(m_new)
            pb_ref[h % 2] = p.astype(BF16)
            pv = jnp.dot(vt_ref[0, hs, :], pb_ref[h % 2, :SUB, :], preferred_element_type=F32)
            for jj in range(1, nj):
                pv += jnp.dot(vt_ref[jj, hs, :], pb_ref[h % 2, jj * SUB:(jj + 1) * SUB, :],
                              preferred_element_type=F32)
            acc_ref[hs, :] = alpha * acc_ref[hs, :] + pv
        m_ref[...] = jnp.concatenate(m_rows, axis=0)
        l_ref[...] = jnp.concatenate(l_rows, axis=0)

    @pl.when(kt < last_kt)
    def _():
        attend(False)

    @pl.when(kt == last_kt)
    def _():
        attend(True)

    @pl.when(kt == last_kt)
    def _():
        for h in range(B_HEADS):
            hs = slice(h * HEAD_DIM, (h + 1) * HEAD_DIM)
            o_ref[:, hs] = (acc_ref[hs, :] / l_ref[h:h + 1, :]).T.astype(o_ref.dtype)


def _dsa_attention(proj, k, vt):
    TQ, KT = DSA_TQ, DSA_KT
    nqb, nkt = SEQ // TQ, SEQ // KT
    kw_block = (2 * D_A + IDX_HEADS * IDX_DIM + KV_RANK) // LANES

    def last_tile(i, kt):
        return jnp.minimum(kt, (i * TQ + TQ - 1) // KT)

    return pl.pallas_call(
        _dsa_kernel,
        grid=(nqb, nkt),
        in_specs=[
            pl.BlockSpec((TQ, D_B), lambda i, kt: (i, 1)),
            pl.BlockSpec((TQ, IDX_HEADS * IDX_DIM), lambda i, kt: (i, 2)),
            pl.BlockSpec((TQ, LANES), lambda i, kt: (i, kw_block)),
            pl.BlockSpec((SEQ, LANES), lambda i, kt: (0, kw_block)),
            pl.BlockSpec((KT, D_B), lambda i, kt: (last_tile(i, kt), 0)),
            pl.BlockSpec((KT // DSA_SUB, D_B, DSA_SUB), lambda i, kt: (last_tile(i, kt), 0, 0)),
        ],
        out_specs=pl.BlockSpec((TQ, D_B), lambda i, kt: (i, 0)),
        out_shape=jax.ShapeDtypeStruct((SEQ, D_B), BF16),
        scratch_shapes=[
            pltpu.VMEM((SEQ // DSA_SUB, DSA_SUB, TQ), F32),
            pltpu.VMEM((B_HEADS, KT, TQ), F32),
            pltpu.VMEM((KT, TQ), F32),
            pltpu.VMEM((2, KT, TQ), F32),
            pltpu.VMEM((2, KT, TQ), BF16),
            pltpu.VMEM((TQ, D_B), BF16),
            pltpu.VMEM((IDX_HEADS, TQ, IDX_DIM), BF16),
            pltpu.VMEM((IDX_HEADS, TQ), F32),
            pltpu.VMEM((1, TQ), F32),
            pltpu.VMEM((1, TQ), F32),
            pltpu.VMEM((1, TQ), F32),
            pltpu.VMEM((1, TQ), F32),
            pltpu.VMEM((1, TQ), F32),
            pltpu.VMEM((B_HEADS, TQ), F32),
            pltpu.VMEM((B_HEADS, TQ), F32),
            pltpu.VMEM((D_B, TQ), F32),
        ],
        compiler_params=_cparams(("arbitrary", "arbitrary")),
        name="dsa_attention",
    )(proj, proj, proj, proj, k, vt)


def _ca_kernel(q_ref, k0_ref, k1_ref, k2_ref, v0_ref, v1_ref, v2_ref, bias_ref, o_ref):
    i = pl.program_id(1)
    k_refs = (k0_ref, k1_ref, k2_ref)
    v_refs = (v0_ref, v1_ref, v2_ref)
    outs = []
    for hh in range(CA_HB):
        hs = slice(hh * HEAD_DIM, (hh + 1) * HEAD_DIM)
        qh = q_ref[:, hs]
        lgs = []
        for j in range(CA_NKB):
            lg = lax.dot_general(qh, k_refs[j][:, hs], (((1,), (1,)), ((), ())), preferred_element_type=F32)
            lg = lg * (HEAD_DIM ** -0.5) + bias_ref[hh, :, j * CA_TQ:(j + 1) * CA_TQ]
            lgs.append(jnp.where(i + j >= CA_NKB - 1, lg, NEG_BIG))
        lg = jnp.concatenate(lgs, axis=-1)
        m = jnp.max(lg, axis=-1, keepdims=True)
        p = jnp.exp(lg - m)
        l = jnp.sum(p, axis=-1, keepdims=True)
        pb = p.astype(BF16)
        acc = jnp.dot(pb[:, :CA_TQ], v_refs[0][:, hs], preferred_element_type=F32)
        for j in range(1, CA_NKB):
            acc += jnp.dot(pb[:, j * CA_TQ:(j + 1) * CA_TQ], v_refs[j][:, hs], preferred_element_type=F32)
        outs.append(acc / l)
    o_ref[...] = jnp.concatenate(outs, axis=-1).astype(o_ref.dtype)


def _ca_bias_table(rel_bias):
    n_heads = rel_bias.shape[0]
    width = CA_NKB * CA_TQ
    span = width + CA_TQ - 1
    period = span + 1
    v = jnp.concatenate([rel_bias.astype(F32)[:, MAX_REL - CA_TQ + 1:],
                         jnp.broadcast_to(rel_bias.astype(F32)[:, -1:], (n_heads, span - MAX_REL - CA_TQ))], axis=1)
    w = jnp.pad(v[:, ::-1], ((0, 0), (0, 1)))
    skew = jnp.tile(w, (1, CA_TQ))[:, :CA_TQ * (period - 1)].reshape(n_heads, CA_TQ, period - 1)
    tab = skew[:, :, CA_TQ - 1:CA_TQ - 1 + width]
    r = np.arange(CA_TQ)[:, None]
    c = np.arange(width)[None, :]
    kc = c // CHUNK
    qc = r // CHUNK + (CA_NKB - 1) * CA_TQ // CHUNK
    band = (kc >= qc - C_LEFT_CHUNKS) & (kc <= qc)
    return jnp.where(band[None], tab, NEG_BIG)


def _chunk_attention(qkv, bias_tab):
    nt = SEQ // CA_TQ
    hw = CA_HB * HEAD_DIM
    nhb = C_HEADS // CA_HB

    def kmap(j, base):
        return lambda hb, i: (jnp.maximum(i - (CA_NKB - 1) + j, 0), base + hb)

    in_specs = [pl.BlockSpec((CA_TQ, hw), lambda hb, i: (i, hb))]
    in_specs += [pl.BlockSpec((CA_TQ, hw), kmap(j, nhb)) for j in range(CA_NKB)]
    in_specs += [pl.BlockSpec((CA_TQ, hw), kmap(j, 2 * nhb)) for j in range(CA_NKB)]
    in_specs += [pl.BlockSpec((CA_HB, CA_TQ, CA_NKB * CA_TQ), lambda hb, i: (hb, 0, 0))]
    return pl.pallas_call(
        _ca_kernel,
        grid=(nhb, nt),
        in_specs=in_specs,
        out_specs=pl.BlockSpec((CA_TQ, hw), lambda hb, i: (i, hb)),
        out_shape=jax.ShapeDtypeStruct((SEQ, D_MODEL), BF16),
        compiler_params=_cparams(("arbitrary", "arbitrary")),
        name="chunk_attention",
    )(qkv, qkv, qkv, qkv, qkv, qkv, qkv, bias_tab)


def _pad_cols(w, n):
    return jnp.pad(w, ((0, 0), (0, n - w.shape[1])))


def kernel(x, c, ada_w, ada_b, norm_g, ffn_w_gate, ffn_w_up, ffn_w_down, ab_w_in, s5_lam_re, s5_lam_im, s5_log_dt, s5_b_re, s5_b_im, s5_c_re, s5_c_im, s5_d, s5_w_glu, s5_b_glu, dsa_kv_norm_g, dsa_w_kv_up, ab_w_out, c_w_qkv, c_rel_bias, c_w_out, final_norm_g):
    mod = _modulation(c, ada_w, ada_b)
    h = x.reshape(SEQ, D_MODEL)
    wg, wu, wd = ffn_w_gate.astype(BF16), ffn_w_up.astype(BF16), ffn_w_down.astype(BF16)

    for layer in range(DEPTH):
        h = _ffn(h, norm_g[layer, 0], mod[layer, 0], wg, wu, wd, layer, 0)
        if layer % 2 == 0:
            e = layer // 2
            w_in = ab_w_in[e]
            o_q, o_kv, o_qi = D_A, D_A + D_B, D_A + D_B + KV_RANK
            o_ki = o_qi + IDX_HEADS * IDX_DIM
            w_in = jnp.concatenate([w_in[:, :o_kv], w_in[:, o_qi:o_ki], w_in[:, o_kv:o_qi], w_in[:, o_ki:]], axis=1)
            proj = _norm_proj(h, norm_g[layer, 1], mod[layer, 1], _pad_cols(w_in, AB_N_PAD).astype(BF16), F32)

            a_re, a_im, bb_re, bb_im = _s5_params(s5_lam_re[e], s5_lam_im[e], s5_log_dt[e], s5_b_re[e], s5_b_im[e])
            n_blk, ts = SEQ // S5_T, S5_T // S5_NSEG
            u_il = (proj[:, :D_A].reshape(n_blk, S5_NSEG, ts, D_A).transpose(0, 2, 1, 3).reshape(SEQ, D_A))
            y_a = _s5_mixer(u_il, a_re, a_im, bb_re, bb_im, s5_c_re[e], s5_c_im[e], s5_d[e],
                            s5_w_glu[e].astype(BF16), s5_b_glu[e])
            y_a = y_a.reshape(n_blk, ts, S5_NSEG, D_A).transpose(0, 2, 1, 3).reshape(SEQ, D_A)

            k, vt = _kv_up(proj, dsa_kv_norm_g[e], dsa_w_kv_up[e].astype(BF16))
            y_b = _dsa_attention(proj, k, vt)

            w_out = ab_w_out[e].astype(BF16)
            h = _out_proj([y_a, y_b], [w_out[:D_A], w_out[D_A:]], h, mod[layer, 1])
        else:
            o = layer // 2
            qkv = _norm_proj(h, norm_g[layer, 1], mod[layer, 1], c_w_qkv[o].astype(BF16), BF16)
            att = _chunk_attention(qkv, _ca_bias_table(c_rel_bias[o]))
            h = _out_proj([att], [c_w_out[o].astype(BF16)], h, mod[layer, 1])
        h = _ffn(h, norm_g[layer, 2], mod[layer, 2], wg, wu, wd, layer, 1,
                 final_g=final_norm_g if layer == DEPTH - 1 else None)
    return h.reshape(1, SEQ, D_MODEL)
```

```python
import functools
import math

import jax
import jax.numpy as jnp
import numpy as np
from jax import lax
from jax.experimental import pallas as pl
from jax.experimental.pallas import tpu as pltpu

F32 = jnp.float32
BF16 = jnp.bfloat16

D_MODEL = 2048
SEQ = 8192
DEPTH = 2
CHUNK = 64
HEAD_DIM = 128
D_FF = 5504
N_SUB = 3
EPS = 1e-6
D_A = D_MODEL // 2
S5_GROUP = 16
S5_GROUPS = D_A // S5_GROUP
S5_STATE = 64
D_B = D_MODEL // 2
B_HEADS = D_B // HEAD_DIM
KV_RANK = D_MODEL // 8
IDX_HEADS = 16
IDX_DIM = 64
TOPK = 256
C_HEADS = D_MODEL // HEAD_DIM
C_LEFT_CHUNKS = 8
MAX_REL = 256

LANES = 128
SUBLANES = 8
VMEM_LIMIT = 56 * 1024 * 1024
NEG_BIG = -1e30
LOG2E = math.log2(math.e)
F32_MAX = float(np.finfo(np.float32).max)

FFN_TM = 512
FFN_TF = 512
PROJ_TM = 1024
PROJ_TN = 512
OUT_TM = 512
AB_N_PAD = 3584
S5_T = 256
S5_NSEG = SUBLANES
S5_SLICES = D_A // LANES
S5_NSTATE = S5_GROUPS * S5_STATE
DSA_TQ = 256
DSA_KT = 1024
DSA_SUB = 256
CA_TQ = 256
CA_HB = 4
CA_NKB = 3


def _cparams(sem):
    return pltpu.CompilerParams(dimension_semantics=sem, vmem_limit_bytes=VMEM_LIMIT)


def _mod_kernel(c_ref, w_ref, b_ref, o_ref):
    rows = 256
    tn = o_ref.shape[-1]

    def body(i, acc):
        r0 = pl.multiple_of(i * rows, rows)
        cc = c_ref[pl.ds(r0, rows), :]
        cc = cc * jax.nn.sigmoid(cc)
        w = w_ref[0, pl.ds(r0, rows), :]
        return acc + jnp.sum((w * cc).reshape(rows // SUBLANES, SUBLANES, tn), axis=0)

    acc = lax.fori_loop(0, D_MODEL // rows, body, jnp.zeros((SUBLANES, tn), F32))
    o_ref[0] = jnp.sum(acc, axis=0, keepdims=True) + b_ref[0]


def _modulation(c, ada_w, ada_b):
    n = N_SUB * 3 * D_MODEL
    tn = 1024
    c_col = c.reshape(D_MODEL, 1)
    out = pl.pallas_call(
        _mod_kernel,
        grid=(DEPTH, n // tn),
        in_specs=[
            pl.BlockSpec((D_MODEL, 1), lambda l, j: (0, 0)),
            pl.BlockSpec((1, D_MODEL, tn), lambda l, j: (l, 0, j)),
            pl.BlockSpec((1, 1, tn), lambda l, j: (l, 0, j)),
        ],
        out_specs=pl.BlockSpec((1, 1, tn), lambda l, j: (l, 0, j)),
        out_shape=jax.ShapeDtypeStruct((DEPTH, 1, n), F32),
        compiler_params=_cparams(("arbitrary", "arbitrary")),
        name="adaln_mod",
    )(c_col, ada_w, ada_b.reshape(DEPTH, 1, n))
    return out.reshape(DEPTH, N_SUB, 3, D_MODEL)


ADALN_ROWS = 32


def _adaln_to(hn_ref, h_ref, g_ref, mod_ref):
    g = g_ref[...]
    scale1 = 1.0 + mod_ref[1:2, :]
    shift = mod_ref[0:1, :]

    def body(r, c):
        rows = pl.ds(pl.multiple_of(r * ADALN_ROWS, ADALN_ROWS), ADALN_ROWS)
        x = h_ref[rows, :]
        ms = jnp.mean(x * x, axis=-1, keepdims=True)
        y = (x * lax.rsqrt(ms + EPS)) * g
        hn_ref[rows, :] = (y * scale1 + shift).astype(hn_ref.dtype)
        return c

    lax.fori_loop(0, h_ref.shape[0] // ADALN_ROWS, body, 0, unroll=4)


def _ffn_kernel(h_ref, g_ref, mod_ref, wg_ref, wu_ref, wd_ref, *rest, n_f, final):
    if final:
        fg_ref, o_ref, hn_ref, acc_ref = rest
    else:
        o_ref, hn_ref, acc_ref = rest
    f = pl.program_id(1)

    @pl.when(f == 0)
    def _():
        _adaln_to(hn_ref, h_ref, g_ref, mod_ref)
        acc_ref[...] = jnp.zeros_like(acc_ref)

    def accumulate(width):
        hn = hn_ref[...]
        gate = jnp.dot(hn, wg_ref[:, :width], preferred_element_type=F32)
        up = jnp.dot(hn, wu_ref[:, :width], preferred_element_type=F32)
        act = (gate * jax.nn.sigmoid(gate)) * up
        acc_ref[...] += jnp.dot(act.astype(BF16), wd_ref[:width, :], preferred_element_type=F32)

    @pl.when(f < n_f - 1)
    def _():
        accumulate(FFN_TF)

    @pl.when(f == n_f - 1)
    def _():
        accumulate(D_FF - (n_f - 1) * FFN_TF)
        half_gate = 0.5 * mod_ref[2:3, :]

        def finish(r, c):
            rows = pl.ds(pl.multiple_of(r * ADALN_ROWS, ADALN_ROWS), ADALN_ROWS)
            out = h_ref[rows, :] + half_gate * acc_ref[rows, :]
            if final:
                ms = jnp.mean(out * out, axis=-1, keepdims=True)
                out = (out * lax.rsqrt(ms + EPS)) * fg_ref[...]
            o_ref[rows, :] = out
            return c

        lax.fori_loop(0, FFN_TM // ADALN_ROWS, finish, 0)


def _ffn(h, g, mod, wg, wu, wd, layer, which, final_g=None):
    n_f = pl.cdiv(D_FF, FFN_TF)
    final = final_g is not None
    in_specs = [
        pl.BlockSpec((FFN_TM, D_MODEL), lambda i, f: (i, 0)),
        pl.BlockSpec((1, D_MODEL), lambda i, f: (0, 0)),
        pl.BlockSpec((3, D_MODEL), lambda i, f: (0, 0)),
        pl.BlockSpec((None, None, D_MODEL, FFN_TF), lambda i, f: (layer, which, 0, f)),
        pl.BlockSpec((None, None, D_MODEL, FFN_TF), lambda i, f: (layer, which, 0, f)),
        pl.BlockSpec((None, None, FFN_TF, D_MODEL), lambda i, f: (layer, which, f, 0)),
    ]
    args = [h, g.reshape(1, D_MODEL), mod, wg, wu, wd]
    if final:
        in_specs.append(pl.BlockSpec((1, D_MODEL), lambda i, f: (0, 0)))
        args.append(final_g.reshape(1, D_MODEL))
    return pl.pallas_call(
        functools.partial(_ffn_kernel, n_f=n_f, final=final),
        grid=(SEQ // FFN_TM, n_f),
        in_specs=in_specs,
        out_specs=pl.BlockSpec((FFN_TM, D_MODEL), lambda i, f: (i, 0)),
        out_shape=jax.ShapeDtypeStruct((SEQ, D_MODEL), F32),
        scratch_shapes=[pltpu.VMEM((FFN_TM, D_MODEL), BF16), pltpu.VMEM((FFN_TM, D_MODEL), F32)],
        compiler_params=_cparams(("arbitrary", "arbitrary")),
        name="ffn_swiglu",
    )(*args)


def _proj_kernel(h_ref, g_ref, mod_ref, w_ref, o_ref, hn_ref):
    @pl.when(pl.program_id(1) == 0)
    def _():
        _adaln_to(hn_ref, h_ref, g_ref, mod_ref)

    o_ref[...] = jnp.dot(hn_ref[...], w_ref[...], preferred_element_type=F32).astype(o_ref.dtype)


def _norm_proj(h, g, mod, w, out_dtype):
    n = w.shape[1]
    return pl.pallas_call(
        _proj_kernel,
        grid=(SEQ // PROJ_TM, n // PROJ_TN),
        in_specs=[
            pl.BlockSpec((PROJ_TM, D_MODEL), lambda i, j: (i, 0)),
            pl.BlockSpec((1, D_MODEL), lambda i, j: (0, 0)),
            pl.BlockSpec((3, D_MODEL), lambda i, j: (0, 0)),
            pl.BlockSpec((D_MODEL, PROJ_TN), lambda i, j: (0, j)),
        ],
        out_specs=pl.BlockSpec((PROJ_TM, PROJ_TN), lambda i, j: (i, j)),
        out_shape=jax.ShapeDtypeStruct((SEQ, n), out_dtype),
        scratch_shapes=[pltpu.VMEM((PROJ_TM, D_MODEL), BF16)],
        compiler_params=_cparams(("arbitrary", "arbitrary")),
        name="adaln_proj",
    )(h, g.reshape(1, D_MODEL), mod, w)


def _out_kernel(*refs, n_lhs):
    lhs = refs[:n_lhs]
    ws = refs[n_lhs:2 * n_lhs]
    h_ref, mod_ref, o_ref = refs[2 * n_lhs:]
    y = jnp.dot(lhs[0][...], ws[0][...], preferred_element_type=F32)
    for a_ref, w_ref in zip(lhs[1:], ws[1:]):
        y += jnp.dot(a_ref[...], w_ref[...], preferred_element_type=F32)
    o_ref[...] = h_ref[...] + mod_ref[2:3, :] * y


def _out_proj(lhs, ws, h, mod):
    n_lhs = len(lhs)
    in_specs = [pl.BlockSpec((OUT_TM, a.shape[1]), lambda i: (i, 0)) for a in lhs]
    in_specs += [pl.BlockSpec(w.shape, lambda i: (0, 0)) for w in ws]
    in_specs += [pl.BlockSpec((OUT_TM, D_MODEL), lambda i: (i, 0)), pl.BlockSpec((3, D_MODEL), lambda i: (0, 0))]
    return pl.pallas_call(
        functools.partial(_out_kernel, n_lhs=n_lhs),
        grid=(SEQ // OUT_TM,),
        in_specs=in_specs,
        out_specs=pl.BlockSpec((OUT_TM, D_MODEL), lambda i: (i, 0)),
        out_shape=jax.ShapeDtypeStruct((SEQ, D_MODEL), F32),
        compiler_params=_cparams(("arbitrary",)),
        name="out_proj_residual",
    )(*lhs, *ws, h, mod)


def _s5_param_kernel(lr_ref, li_ref, ldt_ref, br_ref, bi_ref, are_ref, aim_ref, bbr_ref, bbi_ref):
    lr = lr_ref[...]
    li = li_ref[...]
    dt = jnp.exp(ldt_ref[...])
    mag = jnp.exp(lr * dt)
    ab_re = mag * jnp.cos(li * dt)
    ab_im = mag * jnp.sin(li * dt)
    den = lr * lr + li * li
    nr = ab_re - 1.0
    f_re = (nr * lr + ab_im * li) / den
    f_im = (ab_im * lr - nr * li) / den
    are_ref[...] = ab_re
    aim_ref[...] = ab_im
    br = br_ref[...]
    bi = bi_ref[...]
    bbr_ref[...] = f_re * br - f_im * bi
    bbi_ref[...] = f_re * bi + f_im * br


def _s5_params(lam_re, lam_im, log_dt, b_re, b_im):
    G, P, CG = S5_GROUPS, S5_STATE, S5_GROUP
    return pl.pallas_call(
        _s5_param_kernel,
        out_shape=[jax.ShapeDtypeStruct((G, 1, P), F32), jax.ShapeDtypeStruct((G, 1, P), F32),
                   jax.ShapeDtypeStruct((G, CG, P), F32), jax.ShapeDtypeStruct((G, CG, P), F32)],
        name="s5_zoh_params",
    )(lam_re.reshape(G, 1, P), lam_im.reshape(G, 1, P), log_dt.reshape(G, 1, 1),
      jnp.swapaxes(b_re, 1, 2), jnp.swapaxes(b_im, 1, 2))


def _block_diag_slices(m):
    _, r, c = m.shape
    m4 = m.reshape(S5_SLICES, SUBLANES, r, c)
    eye = jnp.eye(SUBLANES, dtype=m.dtype)
    out = m4[:, :, :, None, :] * eye[None, :, None, :, None]
    return out.reshape(S5_SLICES, SUBLANES * r, SUBLANES * c)


def _s5_kernel(u_ref, bbr_ref, bbi_ref, are_ref, aim_ref, ccr_ref, cci_ref, d_ref, wglu_ref, bglu_ref,
               o_ref, xr_ref, xi_ref, pr_ref, pi_ref, er_ref, ei_ref, cr_ref, ci_ref):
    W = S5_NSTATE // S5_SLICES
    TS = S5_T // S5_NSEG
    CW = 512

    @pl.when(pl.program_id(0) == 0)
    def _():
        pr_ref[0:1, :] = are_ref[...]
        pi_ref[0:1, :] = aim_ref[...]

        def power(s, c):
            ar, ai = are_ref[...], aim_ref[...]
            qr, qi = pr_ref[pl.ds(s - 1, 1), :], pi_ref[pl.ds(s - 1, 1), :]
            pr_ref[pl.ds(s, 1), :] = ar * qr - ai * qi
            pi_ref[pl.ds(s, 1), :] = ar * qi + ai * qr
            return c

        lax.fori_loop(1, TS, power, 0)
        cr_ref[...] = jnp.zeros_like(cr_ref)
        ci_ref[...] = jnp.zeros_like(ci_ref)

    for k in range(S5_SLICES):
        uk = u_ref[:, k * LANES:(k + 1) * LANES].astype(BF16)
        xr_ref[:, k * W:(k + 1) * W] = jnp.dot(uk, bbr_ref[k], preferred_element_type=F32)
        xi_ref[:, k * W:(k + 1) * W] = jnp.dot(uk, bbi_ref[k], preferred_element_type=F32)

    for cg in range(S5_NSTATE // CW):
        cols = slice(cg * CW, (cg + 1) * CW)
        ar = jnp.broadcast_to(are_ref[:, cols], (S5_NSEG, CW))
        ai = jnp.broadcast_to(aim_ref[:, cols], (S5_NSEG, CW))

        def step(s, carry, cols=cols, ar=ar, ai=ai):
            sr, si = carry
            rows = pl.ds(pl.multiple_of(s * S5_NSEG, S5_NSEG), S5_NSEG)
            nr = ar * sr - ai * si + xr_ref[rows, cols]
            ni = ar * si + ai * sr + xi_ref[rows, cols]
            xr_ref[rows, cols] = nr
            xi_ref[rows, cols] = ni
            return nr, ni

        zero = jnp.zeros((S5_NSEG, CW), F32)
        er, ei = lax.fori_loop(0, TS, step, (zero, zero), unroll=4)
        er_ref[:, cols] = er
        ei_ref[:, cols] = ei

    pwr, pwi = pr_ref[TS - 1:TS, :], pi_ref[TS - 1:TS, :]
    c_r, c_i = cr_ref[S5_NSEG:S5_NSEG + 1, :], ci_ref[S5_NSEG:S5_NSEG + 1, :]
    for j in range(S5_NSEG):
        cr_ref[j:j + 1, :] = c_r
        ci_ref[j:j + 1, :] = c_i
        c_r, c_i = (er_ref[j:j + 1, :] + pwr * c_r - pwi * c_i,
                    ei_ref[j:j + 1, :] + pwr * c_i + pwi * c_r)
    cr_ref[S5_NSEG:S5_NSEG + 1, :] = c_r
    ci_ref[S5_NSEG:S5_NSEG + 1, :] = c_i

    for cg in range(S5_NSTATE // CW):
        cols = slice(cg * CW, (cg + 1) * CW)
        c_r, c_i = cr_ref[0:S5_NSEG, cols], ci_ref[0:S5_NSEG, cols]

        def fix(s, carry, cols=cols, c_r=c_r, c_i=c_i):
            rows = pl.ds(pl.multiple_of(s * S5_NSEG, S5_NSEG), S5_NSEG)
            p_r, p_i = pr_ref[pl.ds(s, 1), cols], pi_ref[pl.ds(s, 1), cols]
            xr_ref[rows, cols] = xr_ref[rows, cols] + (p_r * c_r - p_i * c_i)
            xi_ref[rows, cols] = xi_ref[rows, cols] + (p_r * c_i + p_i * c_r)
            return carry

        lax.fori_loop(0, TS, fix, 0, unroll=4)

    ys = []
    for k in range(S5_SLICES):
        xr = xr_ref[:, k * W:(k + 1) * W].astype(BF16)
        xi = xi_ref[:, k * W:(k + 1) * W].astype(BF16)
        ys.append(jnp.dot(xr, ccr_ref[k], preferred_element_type=F32)
                  + jnp.dot(xi, cci_ref[k], preferred_element_type=F32))
    y = jnp.concatenate(ys, axis=-1) + d_ref[...] * u_ref[...]
    y = jax.nn.gelu(y, approximate=True)
    z = jnp.dot(y.astype(BF16), wglu_ref[...], preferred_element_type=F32) + bglu_ref[...]
    o_ref[...] = (y * jax.nn.sigmoid(z)).astype(o_ref.dtype)


def _s5_mixer(proj, a_re, a_im, bb_re, bb_im, c_re, c_im, d_skip, w_glu, b_glu):
    W = S5_NSTATE // S5_SLICES
    bbr = _block_diag_slices(bb_re).astype(BF16)
    bbi = _block_diag_slices(bb_im).astype(BF16)
    ccr = _block_diag_slices(jnp.swapaxes(c_re, 1, 2)).astype(BF16)
    cci = _block_diag_slices(-jnp.swapaxes(c_im, 1, 2)).astype(BF16)
    const3 = lambda t: (0, 0, 0)
    const2 = lambda t: (0, 0)
    return pl.pallas_call(
        _s5_kernel,
        grid=(SEQ // S5_T,),
        in_specs=[
            pl.BlockSpec((S5_T, D_A), lambda t: (t, 0)),
            pl.BlockSpec((S5_SLICES, LANES, W), const3),
            pl.BlockSpec((S5_SLICES, LANES, W), const3),
            pl.BlockSpec((1, S5_NSTATE), const2),
            pl.BlockSpec((1, S5_NSTATE), const2),
            pl.BlockSpec((S5_SLICES, W, LANES), const3),
            pl.BlockSpec((S5_SLICES, W, LANES), const3),
            pl.BlockSpec((1, D_A), const2),
            pl.BlockSpec((D_A, D_A), const2),
            pl.BlockSpec((1, D_A), const2),
        ],
        out_specs=pl.BlockSpec((S5_T, D_A), lambda t: (t, 0)),
        out_shape=jax.ShapeDtypeStruct((SEQ, D_A), BF16),
        scratch_shapes=[pltpu.VMEM((S5_T, S5_NSTATE), F32), pltpu.VMEM((S5_T, S5_NSTATE), F32),
                        pltpu.VMEM((S5_T // S5_NSEG, S5_NSTATE), F32), pltpu.VMEM((S5_T // S5_NSEG, S5_NSTATE), F32),
                        pltpu.VMEM((S5_NSEG, S5_NSTATE), F32), pltpu.VMEM((S5_NSEG, S5_NSTATE), F32),
                        pltpu.VMEM((2 * S5_NSEG, S5_NSTATE), F32), pltpu.VMEM((2 * S5_NSEG, S5_NSTATE), F32)],
        compiler_params=_cparams(("arbitrary",)),
        name="s5_mixer",
    )(proj, bbr, bbi, a_re.reshape(1, S5_NSTATE), a_im.reshape(1, S5_NSTATE), ccr, cci,
      d_skip.reshape(1, D_A), w_glu, b_glu.reshape(1, D_A))


def _kv_kernel(lat_ref, g_ref, wk_ref, wvt_ref, k_ref, vt_ref):
    x = lat_ref[...]
    ms = jnp.mean(x * x, axis=-1, keepdims=True)
    xn = ((x * lax.rsqrt(ms + EPS)) * g_ref[...]).astype(BF16)
    k_ref[...] = jnp.dot(xn, wk_ref[...], preferred_element_type=F32).astype(k_ref.dtype)
    for j in range(vt_ref.shape[0]):
        vt = lax.dot_general(wvt_ref[...], xn[j * DSA_SUB:(j + 1) * DSA_SUB], (((1,), (1,)), ((), ())),
                             preferred_element_type=F32)
        vt_ref[j] = vt.astype(vt_ref.dtype)


def _kv_up(proj, g, w):
    tm = DSA_KT
    lat_block = (2 * D_A + IDX_HEADS * IDX_DIM) // KV_RANK
    wk = w[:, :D_B]
    wvt = w[:, D_B:].T
    return pl.pallas_call(
        _kv_kernel,
        grid=(SEQ // tm,),
        in_specs=[
            pl.BlockSpec((tm, KV_RANK), lambda i: (i, lat_block)),
            pl.BlockSpec((1, KV_RANK), lambda i: (0, 0)),
            pl.BlockSpec((KV_RANK, D_B), lambda i: (0, 0)),
            pl.BlockSpec((D_B, KV_RANK), lambda i: (0, 0)),
        ],
        out_specs=[pl.BlockSpec((tm, D_B), lambda i: (i, 0)),
                   pl.BlockSpec((tm // DSA_SUB, D_B, DSA_SUB), lambda i: (i, 0, 0))],
        out_shape=[jax.ShapeDtypeStruct((SEQ, D_B), BF16),
                   jax.ShapeDtypeStruct((SEQ // DSA_SUB, D_B, DSA_SUB), BF16)],
        compiler_params=_cparams(("arbitrary",)),
        name="dsa_kv_up",
    )(proj, g.reshape(1, KV_RANK), wk, wvt)


def _dsa_kernel(q_ref, qidx_ref, w_ref, kidx_ref, k_ref, vt_ref, o_ref,
                sc_ref, a2_ref, mb_ref, lg_ref, pb_ref, qst_ref, qi_ref, wt_ref, thr_ref, lo_ref, hi_ref, clo_ref,
                m_ref, l_ref, acc_ref):
    TQ, KT, SUB = DSA_TQ, DSA_KT, DSA_SUB
    K = float(TOPK)
    i = pl.program_id(0)
    kt = pl.program_id(1)
    last_kt = (i * TQ + TQ - 1) // KT
    n_sub = i + 1
    q_pos = i * TQ + lax.broadcasted_iota(jnp.int32, (1, TQ), 1)
    q_chunk = q_pos // CHUNK

    @pl.when(jnp.logical_and(i == 0, kt == 0))
    def _():
        d = (lax.broadcasted_iota(jnp.int32, (KT, TQ), 1) - lax.broadcasted_iota(jnp.int32, (KT, TQ), 0)).astype(F32)
        for h in range(B_HEADS):
            a2_ref[h] = (LOG2E * 2.0 ** (-8.0 * (h + 1) / B_HEADS)) * d

    @pl.when(kt == 0)
    def _():
        qst_ref[...] = (q_ref[...] * (HEAD_DIM ** -0.5 * LOG2E)).T.astype(BF16)
        qv = qidx_ref[...]
        for h in range(IDX_HEADS):
            qi_ref[h] = qv[:, h * IDX_DIM:(h + 1) * IDX_DIM].astype(BF16)
        wt_ref[...] = w_ref[...].T[IDX_DIM:IDX_DIM + IDX_HEADS, :]

        def scores(g):
            s0 = pl.multiple_of(g * SUB, SUB)
            kk = kidx_ref[pl.ds(s0, SUB), :IDX_DIM].astype(BF16)
            tot = jnp.zeros((SUB, TQ), F32)
            for h in range(IDX_HEADS):
                s = lax.dot_general(kk, qi_ref[h], (((1,), (1,)), ((), ())), preferred_element_type=F32)
                tot = tot + jnp.maximum(s, 0.0) * wt_ref[h:h + 1, :]
            return tot * ((IDX_DIM ** -0.5) * (IDX_HEADS ** -0.5))

        def past_tile(g, carry):
            mx, mn = carry
            tot = scores(g)
            sc_ref[g] = tot
            return (jnp.maximum(mx, jnp.max(tot, axis=0, keepdims=True)),
                    jnp.minimum(mn, jnp.min(tot, axis=0, keepdims=True)))

        mx, mn = lax.fori_loop(0, i, past_tile,
                               (jnp.full((1, TQ), -jnp.inf, F32), jnp.full((1, TQ), jnp.inf, F32)))
        tot = scores(i)
        key_chunk = (i * SUB + lax.broadcasted_iota(jnp.int32, (SUB, 1), 0)) // CHUNK
        adm = key_chunk <= q_chunk
        sc_ref[i] = jnp.where(adm, tot, -jnp.inf)
        mx = jnp.maximum(mx, jnp.max(jnp.where(adm, tot, -jnp.inf), axis=0, keepdims=True))
        mn = jnp.minimum(mn, jnp.min(jnp.where(adm, tot, jnp.inf), axis=0, keepdims=True))

        def fill_tile(g, c):
            sc_ref[g] = jnp.full((SUB, TQ), -jnp.inf, F32)
            return c

        lax.fori_loop(n_sub, (last_kt + 1) * (KT // SUB), fill_tile, 0)

        n_adm = ((q_chunk + 1) * CHUNK).astype(F32)
        keep_all = n_adm <= K
        lo_ref[...] = jnp.where(keep_all, -F32_MAX, mn)
        hi_ref[...] = mx
        clo_ref[...] = jnp.where(keep_all, K, n_adm)

        def count_ge(mid):
            def tile_count(g):
                ind = jnp.where(sc_ref[g] >= mid, 1.0, 0.0)
                return jnp.sum(ind.reshape(SUB // SUBLANES, SUBLANES, TQ), axis=0)

            def body(g2, acc):
                return acc + tile_count(2 * g2) + tile_count(2 * g2 + 1)

            acc = lax.fori_loop(0, n_sub // 2, body, jnp.zeros((SUBLANES, TQ), F32))
            odd = (n_sub % 2).astype(F32)
            acc = acc + odd * tile_count(n_sub - 1)
            return jnp.sum(acc, axis=0, keepdims=True)

        def n_open():
            return jnp.max(jnp.where(clo_ref[...] != K, 1.0, 0.0))

        def cond(c):
            it, open_rows = c
            return jnp.logical_and(it < 64, open_rows > 0.5)

        def body(c):
            it, _ = c
            lo, hi, clo = lo_ref[...], hi_ref[...], clo_ref[...]
            mid = lo + 0.5 * (hi - lo)
            c_mid = count_ge(mid)
            live = clo != K
            ge = jnp.logical_and(live, c_mid >= K)
            lt = jnp.logical_and(live, c_mid < K)
            lo_ref[...] = jnp.where(ge, mid, lo)
            clo_ref[...] = jnp.where(ge, c_mid, clo)
            hi_ref[...] = jnp.where(lt, mid, hi)
            return it + 1, n_open()

        lax.while_loop(cond, body, (jnp.int32(0), n_open()))
        thr_ref[...] = lo_ref[...]

        m_ref[...] = jnp.full(m_ref.shape, NEG_BIG, F32)
        l_ref[...] = jnp.zeros_like(l_ref)
        acc_ref[...] = jnp.zeros_like(acc_ref)

    def attend(last):
        nj = KT // SUB
        thr = thr_ref[...]
        for jj in range(nj):
            mb_ref[jj * SUB:(jj + 1) * SUB, :] = jnp.where(sc_ref[kt * nj + jj] >= thr, 0.0, NEG_BIG)
        gap = (i * TQ - kt * KT).astype(F32)
        m_all, l_all = m_ref[...], l_ref[...]
        m_rows, l_rows = [], []

        def qk(h):
            hs = slice(h * HEAD_DIM, (h + 1) * HEAD_DIM)
            for half in range(2):
                rows = slice(half * (KT // 2), (half + 1) * (KT // 2))
                lg_ref[h % 2, rows, :] = jnp.dot(k_ref[rows, hs], qst_ref[hs, :], preferred_element_type=F32)

        qk(0)
        for h in range(B_HEADS):
            if h + 1 < B_HEADS:
                qk(h + 1)
            hs = slice(h * HEAD_DIM, (h + 1) * HEAD_DIM)
            slope2 = LOG2E * 2.0 ** (-8.0 * (h + 1) / B_HEADS)
            if last:
                lg = (lg_ref[h % 2] - jnp.abs(a2_ref[h] + slope2 * gap)) + mb_ref[...]
                off = 0.0
            else:
                lg = (lg_ref[h % 2] - a2_ref[h]) + mb_ref[...]
                off = slope2 * gap
            lg_ref[h % 2] = lg
            m_old = m_all[h:h + 1, :]
            m_new = jnp.maximum(m_old, jnp.max(lg, axis=0, keepdims=True) - off)
            alpha = jnp.exp2(m_old - m_new)
            p = jnp.exp2(lg_ref[h % 2] - (m_new + off))
            l_rows.append(alpha * l_all[h:h + 1, :] + jnp.sum(p, axis=0, keepdims=True))
            m_rows.append(m_new)
            pb_ref[h % 2] = p.astype(BF16)
            pv = jnp.dot(vt_ref[0, hs, :], pb_ref[h % 2, :SUB, :], preferred_element_type=F32)
            for jj in range(1, nj):
                pv += jnp.dot(vt_ref[jj, hs, :], pb_ref[h % 2, jj * SUB:(jj + 1) * SUB, :],
                              preferred_element_type=F32)
            acc_ref[hs, :] = alpha * acc_ref[hs, :] + pv
        m_ref[...] = jnp.concatenate(m_rows, axis=0)
        l_ref[...] = jnp.concatenate(l_rows, axis=0)

    @pl.when(kt < last_kt)
    def _():
        attend(False)

    @pl.when(kt == last_kt)
    def _():
        attend(True)

    @pl.when(kt == last_kt)
    def _():
        for h in range(B_HEADS):
            hs = slice(h * HEAD_DIM, (h + 1) * HEAD_DIM)
            o_ref[:, hs] = (acc_ref[hs, :] / l_ref[h:h + 1, :]).T.astype(o_ref.dtype)


def _dsa_attention(proj, k, vt):
    TQ, KT = DSA_TQ, DSA_KT
    nqb, nkt = SEQ // TQ, SEQ // KT
    kw_block = (2 * D_A + IDX_HEADS * IDX_DIM + KV_RANK) // LANES

    def last_tile(i, kt):
        return jnp.minimum(kt, (i * TQ + TQ - 1) // KT)

    return pl.pallas_call(
        _dsa_kernel,
        grid=(nqb, nkt),
        in_specs=[
            pl.BlockSpec((TQ, D_B), lambda i, kt: (i, 1)),
            pl.BlockSpec((TQ, IDX_HEADS * IDX_DIM), lambda i, kt: (i, 2)),
            pl.BlockSpec((TQ, LANES), lambda i, kt: (i, kw_block)),
            pl.BlockSpec((SEQ, LANES), lambda i, kt: (0, kw_block)),
            pl.BlockSpec((KT, D_B), lambda i, kt: (last_tile(i, kt), 0)),
            pl.BlockSpec((KT // DSA_SUB, D_B, DSA_SUB), lambda i, kt: (last_tile(i, kt), 0, 0)),
        ],
        out_specs=pl.BlockSpec((TQ, D_B), lambda i, kt: (i, 0)),
        out_shape=jax.ShapeDtypeStruct((SEQ, D_B), BF16),
        scratch_shapes=[
            pltpu.VMEM((SEQ // DSA_SUB, DSA_SUB, TQ), F32),
            pltpu.VMEM((B_HEADS, KT, TQ), F32),
            pltpu.VMEM((KT, TQ), F32),
            pltpu.VMEM((2, KT, TQ), F32),
            pltpu.VMEM((2, KT, TQ), BF16),
            pltpu.VMEM((D_B, TQ), BF16),
            pltpu.VMEM((IDX_HEADS, TQ, IDX_DIM), BF16),
            pltpu.VMEM((IDX_HEADS, TQ), F32),
            pltpu.VMEM((1, TQ), F32),
            pltpu.VMEM((1, TQ), F32),
            pltpu.VMEM((1, TQ), F32),
            pltpu.VMEM((1, TQ), F32),
            pltpu.VMEM((B_HEADS, TQ), F32),
            pltpu.VMEM((B_HEADS, TQ), F32),
            pltpu.VMEM((D_B, TQ), F32),
        ],
        compiler_params=_cparams(("arbitrary", "arbitrary")),
        name="dsa_attention",
    )(proj, proj, proj, proj, k, vt)


def _ca_kernel(q_ref, k0_ref, k1_ref, k2_ref, v0_ref, v1_ref, v2_ref, bias_ref, o_ref, lg_ref, pb_ref):
    i = pl.program_id(1)
    k_refs = (k0_ref, k1_ref, k2_ref)
    v_refs = (v0_ref, v1_ref, v2_ref)

    def qk(hh):
        hs = slice(hh * HEAD_DIM, (hh + 1) * HEAD_DIM)
        qh = q_ref[:, hs]
        for j in range(CA_NKB):
            cols = slice(j * CA_TQ, (j + 1) * CA_TQ)
            lg = lax.dot_general(qh, k_refs[j][:, hs], (((1,), (1,)), ((), ())), preferred_element_type=F32)
            lg = lg * (HEAD_DIM ** -0.5 * LOG2E) + bias_ref[hh, :, cols]
            lg_ref[hh % 2, :, cols] = jnp.where(i + j >= CA_NKB - 1, lg, NEG_BIG)

    qk(0)
    for hh in range(CA_HB):
        if hh + 1 < CA_HB:
            qk(hh + 1)
        hs = slice(hh * HEAD_DIM, (hh + 1) * HEAD_DIM)
        lg = lg_ref[hh % 2]
        m = jnp.max(lg, axis=-1, keepdims=True)
        p = jnp.exp2(lg - m)
        l = jnp.sum(p, axis=-1, keepdims=True)
        pb_ref[hh % 2] = p.astype(BF16)
        acc = jnp.dot(pb_ref[hh % 2, :, :CA_TQ], v_refs[0][:, hs], preferred_element_type=F32)
        for j in range(1, CA_NKB):
            acc += jnp.dot(pb_ref[hh % 2, :, j * CA_TQ:(j + 1) * CA_TQ], v_refs[j][:, hs],
                           preferred_element_type=F32)
        o_ref[:, hs] = (acc / l).astype(o_ref.dtype)


def _ca_bias_table(rel_bias):
    n_heads = rel_bias.shape[0]
    width = CA_NKB * CA_TQ
    span = width + CA_TQ - 1
    period = span + 1
    v = jnp.concatenate([rel_bias.astype(F32)[:, MAX_REL - CA_TQ + 1:],
                         jnp.broadcast_to(rel_bias.astype(F32)[:, -1:], (n_heads, span - MAX_REL - CA_TQ))], axis=1)
    w = jnp.pad(v[:, ::-1], ((0, 0), (0, 1)))
    skew = jnp.tile(w, (1, CA_TQ))[:, :CA_TQ * (period - 1)].reshape(n_heads, CA_TQ, period - 1)
    tab = skew[:, :, CA_TQ - 1:CA_TQ - 1 + width]
    r = np.arange(CA_TQ)[:, None]
    c = np.arange(width)[None, :]
    kc = c // CHUNK
    qc = r // CHUNK + (CA_NKB - 1) * CA_TQ // CHUNK
    band = (kc >= qc - C_LEFT_CHUNKS) & (kc <= qc)
    return jnp.where(band[None], tab * LOG2E, NEG_BIG)


def _chunk_attention(qkv, bias_tab):
    nt = SEQ // CA_TQ
    hw = CA_HB * HEAD_DIM
    nhb = C_HEADS // CA_HB

    def kmap(j, base):
        return lambda hb, i: (jnp.maximum(i - (CA_NKB - 1) + j, 0), base + hb)

    in_specs = [pl.BlockSpec((CA_TQ, hw), lambda hb, i: (i, hb))]
    in_specs += [pl.BlockSpec((CA_TQ, hw), kmap(j, nhb)) for j in range(CA_NKB)]
    in_specs += [pl.BlockSpec((CA_TQ, hw), kmap(j, 2 * nhb)) for j in range(CA_NKB)]
    in_specs += [pl.BlockSpec((CA_HB, CA_TQ, CA_NKB * CA_TQ), lambda hb, i: (hb, 0, 0))]
    return pl.pallas_call(
        _ca_kernel,
        grid=(nhb, nt),
        in_specs=in_specs,
        out_specs=pl.BlockSpec((CA_TQ, hw), lambda hb, i: (i, hb)),
        out_shape=jax.ShapeDtypeStruct((SEQ, D_MODEL), BF16),
        scratch_shapes=[pltpu.VMEM((2, CA_TQ, CA_NKB * CA_TQ), F32),
                        pltpu.VMEM((2, CA_TQ, CA_NKB * CA_TQ), BF16)],
        compiler_params=_cparams(("arbitrary", "arbitrary")),
        name="chunk_attention",
    )(qkv, qkv, qkv, qkv, qkv, qkv, qkv, bias_tab)


def _pad_cols(w, n):
    return jnp.pad(w, ((0, 0), (0, n - w.shape[1])))


def kernel(x, c, ada_w, ada_b, norm_g, ffn_w_gate, ffn_w_up, ffn_w_down, ab_w_in, s5_lam_re, s5_lam_im, s5_log_dt, s5_b_re, s5_b_im, s5_c_re, s5_c_im, s5_d, s5_w_glu, s5_b_glu, dsa_kv_norm_g, dsa_w_kv_up, ab_w_out, c_w_qkv, c_rel_bias, c_w_out, final_norm_g):
    mod = _modulation(c, ada_w, ada_b)
    h = x.reshape(SEQ, D_MODEL)
    wg, wu, wd = ffn_w_gate.astype(BF16), ffn_w_up.astype(BF16), ffn_w_down.astype(BF16)

    for layer in range(DEPTH):
        h = _ffn(h, norm_g[layer, 0], mod[layer, 0], wg, wu, wd, layer, 0)
        if layer % 2 == 0:
            e = layer // 2
            w_in = ab_w_in[e]
            o_q, o_kv, o_qi = D_A, D_A + D_B, D_A + D_B + KV_RANK
            o_ki = o_qi + IDX_HEADS * IDX_DIM
            w_in = jnp.concatenate([w_in[:, :o_kv], w_in[:, o_qi:o_ki], w_in[:, o_kv:o_qi], w_in[:, o_ki:]], axis=1)
            proj = _norm_proj(h, norm_g[layer, 1], mod[layer, 1], _pad_cols(w_in, AB_N_PAD).astype(BF16), F32)

            a_re, a_im, bb_re, bb_im = _s5_params(s5_lam_re[e], s5_lam_im[e], s5_log_dt[e], s5_b_re[e], s5_b_im[e])
            n_blk, ts = SEQ // S5_T, S5_T // S5_NSEG
            u_il = (proj[:, :D_A].reshape(n_blk, S5_NSEG, ts, D_A).transpose(0, 2, 1, 3).reshape(SEQ, D_A))
            y_a = _s5_mixer(u_il, a_re, a_im, bb_re, bb_im, s5_c_re[e], s5_c_im[e], s5_d[e],
                            s5_w_glu[e].astype(BF16), s5_b_glu[e])
            y_a = y_a.reshape(n_blk, ts, S5_NSEG, D_A).transpose(0, 2, 1, 3).reshape(SEQ, D_A)

            k, vt = _kv_up(proj, dsa_kv_norm_g[e], dsa_w_kv_up[e].astype(BF16))
            y_b = _dsa_attention(proj, k, vt)

            w_out = ab_w_out[e].astype(BF16)
            h = _out_proj([y_a, y_b], [w_out[:D_A], w_out[D_A:]], h, mod[layer, 1])
        else:
            o = layer // 2
            qkv = _norm_proj(h, norm_g[layer, 1], mod[layer, 1], c_w_qkv[o].astype(BF16), BF16)
            att = _chunk_attention(qkv, _ca_bias_table(c_rel_bias[o]))
            h = _out_proj([att], [c_w_out[o].astype(BF16)], h, mod[layer, 1])
        h = _ffn(h, norm_g[layer, 2], mod[layer, 2], wg, wu, wd, layer, 1,
                 final_g=final_norm_g if layer == DEPTH - 1 else None)
    return h.reshape(1, SEQ, D_MODEL)
```

```python
import functools
import math

import jax
import jax.numpy as jnp
import numpy as np
from jax import lax
from jax.experimental import pallas as pl
from jax.experimental.pallas import tpu as pltpu

F32 = jnp.float32
BF16 = jnp.bfloat16

D_MODEL = 2048
SEQ = 8192
DEPTH = 2
CHUNK = 64
HEAD_DIM = 128
D_FF = 5504
N_SUB = 3
EPS = 1e-6
D_A = D_MODEL // 2
S5_GROUP = 16
S5_GROUPS = D_A // S5_GROUP
S5_STATE = 64
D_B = D_MODEL // 2
B_HEADS = D_B // HEAD_DIM
KV_RANK = D_MODEL // 8
IDX_HEADS = 16
IDX_DIM = 64
TOPK = 256
C_HEADS = D_MODEL // HEAD_DIM
C_LEFT_CHUNKS = 8
MAX_REL = 256

LANES = 128
SUBLANES = 8
VMEM_LIMIT = 56 * 1024 * 1024
NEG_BIG = -1e30
LOG2E = math.log2(math.e)
F32_MAX = float(np.finfo(np.float32).max)

FFN_TM = 1024
FFN_TF = 256
FFN_VMEM_LIMIT = 60 * 1024 * 1024
PROJ_TM = 1024
PROJ_TN = 512
OUT_TM = 512
AB_N_PAD = 3584
S5_T = 256
S5_NSEG = SUBLANES
S5_SLICES = D_A // LANES
S5_NSTATE = S5_GROUPS * S5_STATE
DSA_TQ = 256
DSA_KT = 1024
DSA_SUB = 256
CA_TQ = 256
CA_HB = 4
CA_NKB = 3
CA_PERIOD = 1024


def _cparams(sem):
    return pltpu.CompilerParams(dimension_semantics=sem, vmem_limit_bytes=VMEM_LIMIT)


def _mod_kernel(c_ref, w_ref, b_ref, o_ref):
    rows = 256
    tn = o_ref.shape[-1]

    def body(i, acc):
        r0 = pl.multiple_of(i * rows, rows)
        cc = c_ref[pl.ds(r0, rows), :]
        cc = cc * jax.nn.sigmoid(cc)
        w = w_ref[0, pl.ds(r0, rows), :]
        return acc + jnp.sum((w * cc).reshape(rows // SUBLANES, SUBLANES, tn), axis=0)

    acc = lax.fori_loop(0, D_MODEL // rows, body, jnp.zeros((SUBLANES, tn), F32))
    o_ref[0] = jnp.sum(acc, axis=0, keepdims=True) + b_ref[0]


def _modulation(c, ada_w, ada_b):
    n = N_SUB * 3 * D_MODEL
    tn = 1024
    c_col = c.reshape(D_MODEL, 1)
    out = pl.pallas_call(
        _mod_kernel,
        grid=(DEPTH, n // tn),
        in_specs=[
            pl.BlockSpec((D_MODEL, 1), lambda l, j: (0, 0)),
            pl.BlockSpec((1, D_MODEL, tn), lambda l, j: (l, 0, j)),
            pl.BlockSpec((1, 1, tn), lambda l, j: (l, 0, j)),
        ],
        out_specs=pl.BlockSpec((1, 1, tn), lambda l, j: (l, 0, j)),
        out_shape=jax.ShapeDtypeStruct((DEPTH, 1, n), F32),
        compiler_params=_cparams(("arbitrary", "arbitrary")),
        name="adaln_mod",
    )(c_col, ada_w, ada_b.reshape(DEPTH, 1, n))
    return out.reshape(DEPTH, N_SUB, 3, D_MODEL)


ADALN_ROWS = 32


def _adaln_to(hn_ref, h_ref, g_ref, mod_ref):
    g = g_ref[...]
    scale1 = 1.0 + mod_ref[1:2, :]
    shift = mod_ref[0:1, :]

    def body(r, c):
        rows = pl.ds(pl.multiple_of(r * ADALN_ROWS, ADALN_ROWS), ADALN_ROWS)
        x = h_ref[rows, :]
        ms = jnp.mean(x * x, axis=-1, keepdims=True)
        y = (x * lax.rsqrt(ms + EPS)) * g
        hn_ref[rows, :] = (y * scale1 + shift).astype(hn_ref.dtype)
        return c

    lax.fori_loop(0, h_ref.shape[0] // ADALN_ROWS, body, 0, unroll=4)


def _ffn_kernel(h_ref, g_ref, mod_ref, wg_ref, wu_ref, wd_ref, *rest, n_f, final):
    if final:
        fg_ref, o_ref, hn_ref = rest
    else:
        o_ref, hn_ref = rest
    f = pl.program_id(1)

    @pl.when(f == 0)
    def _():
        _adaln_to(hn_ref, h_ref, g_ref, mod_ref)
        o_ref[...] = jnp.zeros_like(o_ref)

    def accumulate(width):
        hn = hn_ref[...]
        gate = jnp.dot(hn, wg_ref[:, :width].astype(BF16), preferred_element_type=F32)
        up = jnp.dot(hn, wu_ref[:, :width].astype(BF16), preferred_element_type=F32)
        act = (gate * jax.nn.sigmoid(gate)) * up
        o_ref[...] += jnp.dot(act.astype(BF16), wd_ref[:width, :].astype(BF16), preferred_element_type=F32)

    @pl.when(f < n_f - 1)
    def _():
        accumulate(FFN_TF)

    @pl.when(f == n_f - 1)
    def _():
        accumulate(D_FF - (n_f - 1) * FFN_TF)
        half_gate = 0.5 * mod_ref[2:3, :]

        def finish(r, c):
            rows = pl.ds(pl.multiple_of(r * ADALN_ROWS, ADALN_ROWS), ADALN_ROWS)
            out = h_ref[rows, :] + half_gate * o_ref[rows, :]
            if final:
                ms = jnp.mean(out * out, axis=-1, keepdims=True)
                out = (out * lax.rsqrt(ms + EPS)) * fg_ref[...]
            o_ref[rows, :] = out
            return c

        lax.fori_loop(0, FFN_TM // ADALN_ROWS, finish, 0, unroll=4)


def _ffn(h, g, mod, wg, wu, wd, layer, which, final_g=None):
    n_f = pl.cdiv(D_FF, FFN_TF)
    final = final_g is not None
    in_specs = [
        pl.BlockSpec((FFN_TM, D_MODEL), lambda i, f: (i, 0)),
        pl.BlockSpec((1, D_MODEL), lambda i, f: (0, 0)),
        pl.BlockSpec((3, D_MODEL), lambda i, f: (0, 0)),
        pl.BlockSpec((None, None, D_MODEL, FFN_TF), lambda i, f: (layer, which, 0, f)),
        pl.BlockSpec((None, None, D_MODEL, FFN_TF), lambda i, f: (layer, which, 0, f)),
        pl.BlockSpec((None, None, FFN_TF, D_MODEL), lambda i, f: (layer, which, f, 0)),
    ]
    args = [h, g.reshape(1, D_MODEL), mod, wg, wu, wd]
    if final:
        in_specs.append(pl.BlockSpec((1, D_MODEL), lambda i, f: (0, 0)))
        args.append(final_g.reshape(1, D_MODEL))
    return pl.pallas_call(
        functools.partial(_ffn_kernel, n_f=n_f, final=final),
        grid=(SEQ // FFN_TM, n_f),
        in_specs=in_specs,
        out_specs=pl.BlockSpec((FFN_TM, D_MODEL), lambda i, f: (i, 0)),
        out_shape=jax.ShapeDtypeStruct((SEQ, D_MODEL), F32),
        scratch_shapes=[pltpu.VMEM((FFN_TM, D_MODEL), BF16)],
        compiler_params=pltpu.CompilerParams(dimension_semantics=("arbitrary", "arbitrary"),
                                             vmem_limit_bytes=FFN_VMEM_LIMIT),
        name="ffn_swiglu",
    )(*args)


def _proj_kernel(h_ref, g_ref, mod_ref, w_ref, o_ref, hn_ref):
    @pl.when(pl.program_id(1) == 0)
    def _():
        _adaln_to(hn_ref, h_ref, g_ref, mod_ref)

    o_ref[...] = jnp.dot(hn_ref[...], w_ref[...], preferred_element_type=F32).astype(o_ref.dtype)


def _norm_proj(h, g, mod, w, out_dtype):
    n = w.shape[1]
    return pl.pallas_call(
        _proj_kernel,
        grid=(SEQ // PROJ_TM, n // PROJ_TN),
        in_specs=[
            pl.BlockSpec((PROJ_TM, D_MODEL), lambda i, j: (i, 0)),
            pl.BlockSpec((1, D_MODEL), lambda i, j: (0, 0)),
            pl.BlockSpec((3, D_MODEL), lambda i, j: (0, 0)),
            pl.BlockSpec((D_MODEL, PROJ_TN), lambda i, j: (0, j)),
        ],
        out_specs=pl.BlockSpec((PROJ_TM, PROJ_TN), lambda i, j: (i, j)),
        out_shape=jax.ShapeDtypeStruct((SEQ, n), out_dtype),
        scratch_shapes=[pltpu.VMEM((PROJ_TM, D_MODEL), BF16)],
        compiler_params=_cparams(("arbitrary", "arbitrary")),
        name="adaln_proj",
    )(h, g.reshape(1, D_MODEL), mod, w)


def _out_kernel(*refs, n_lhs):
    lhs = refs[:n_lhs]
    ws = refs[n_lhs:2 * n_lhs]
    h_ref, mod_ref, o_ref = refs[2 * n_lhs:]
    y = jnp.dot(lhs[0][...], ws[0][...], preferred_element_type=F32)
    for a_ref, w_ref in zip(lhs[1:], ws[1:]):
        y += jnp.dot(a_ref[...], w_ref[...], preferred_element_type=F32)
    o_ref[...] = h_ref[...] + mod_ref[2:3, :] * y


def _out_proj(lhs, ws, h, mod):
    n_lhs = len(lhs)
    in_specs = [pl.BlockSpec((OUT_TM, a.shape[1]), lambda i: (i, 0)) for a in lhs]
    in_specs += [pl.BlockSpec(w.shape, lambda i: (0, 0)) for w in ws]
    in_specs += [pl.BlockSpec((OUT_TM, D_MODEL), lambda i: (i, 0)), pl.BlockSpec((3, D_MODEL), lambda i: (0, 0))]
    return pl.pallas_call(
        functools.partial(_out_kernel, n_lhs=n_lhs),
        grid=(SEQ // OUT_TM,),
        in_specs=in_specs,
        out_specs=pl.BlockSpec((OUT_TM, D_MODEL), lambda i: (i, 0)),
        out_shape=jax.ShapeDtypeStruct((SEQ, D_MODEL), F32),
        compiler_params=_cparams(("arbitrary",)),
        name="out_proj_residual",
    )(*lhs, *ws, h, mod)


def _s5_param_kernel(lr_ref, li_ref, ldt_ref, br_ref, bi_ref, are_ref, aim_ref, bbr_ref, bbi_ref):
    lr = lr_ref[...]
    li = li_ref[...]
    dt = jnp.exp(ldt_ref[...])
    mag = jnp.exp(lr * dt)
    ab_re = mag * jnp.cos(li * dt)
    ab_im = mag * jnp.sin(li * dt)
    den = lr * lr + li * li
    nr = ab_re - 1.0
    f_re = (nr * lr + ab_im * li) / den
    f_im = (ab_im * lr - nr * li) / den
    are_ref[...] = ab_re
    aim_ref[...] = ab_im
    br = br_ref[...]
    bi = bi_ref[...]
    bbr_ref[...] = f_re * br - f_im * bi
    bbi_ref[...] = f_re * bi + f_im * br


def _s5_params(lam_re, lam_im, log_dt, b_re, b_im):
    G, P, CG = S5_GROUPS, S5_STATE, S5_GROUP
    return pl.pallas_call(
        _s5_param_kernel,
        out_shape=[jax.ShapeDtypeStruct((G, 1, P), F32), jax.ShapeDtypeStruct((G, 1, P), F32),
                   jax.ShapeDtypeStruct((G, CG, P), F32), jax.ShapeDtypeStruct((G, CG, P), F32)],
        name="s5_zoh_params",
    )(lam_re.reshape(G, 1, P), lam_im.reshape(G, 1, P), log_dt.reshape(G, 1, 1),
      jnp.swapaxes(b_re, 1, 2), jnp.swapaxes(b_im, 1, 2))


def _block_diag_slices(m):
    _, r, c = m.shape
    m4 = m.reshape(S5_SLICES, SUBLANES, r, c)
    eye = jnp.eye(SUBLANES, dtype=m.dtype)
    out = m4[:, :, :, None, :] * eye[None, :, None, :, None]
    return out.reshape(S5_SLICES, SUBLANES * r, SUBLANES * c)


def _s5_kernel(u_ref, bbr_ref, bbi_ref, are_ref, aim_ref, ccr_ref, cci_ref, d_ref, wglu_ref, bglu_ref,
               o_ref, xr_ref, xi_ref, pr_ref, pi_ref, er_ref, ei_ref, cr_ref, ci_ref):
    W = S5_NSTATE // S5_SLICES
    TS = S5_T // S5_NSEG
    CW = 512

    @pl.when(pl.program_id(0) == 0)
    def _():
        pr_ref[0:1, :] = are_ref[...]
        pi_ref[0:1, :] = aim_ref[...]

        def power(s, c):
            ar, ai = are_ref[...], aim_ref[...]
            qr, qi = pr_ref[pl.ds(s - 1, 1), :], pi_ref[pl.ds(s - 1, 1), :]
            pr_ref[pl.ds(s, 1), :] = ar * qr - ai * qi
            pi_ref[pl.ds(s, 1), :] = ar * qi + ai * qr
            return c

        lax.fori_loop(1, TS, power, 0)
        cr_ref[...] = jnp.zeros_like(cr_ref)
        ci_ref[...] = jnp.zeros_like(ci_ref)

    for k in range(S5_SLICES):
        uk = u_ref[:, k * LANES:(k + 1) * LANES].astype(BF16)
        xr_ref[:, k * W:(k + 1) * W] = jnp.dot(uk, bbr_ref[k], preferred_element_type=F32)
        xi_ref[:, k * W:(k + 1) * W] = jnp.dot(uk, bbi_ref[k], preferred_element_type=F32)

    for cg in range(S5_NSTATE // CW):
        cols = slice(cg * CW, (cg + 1) * CW)
        ar = jnp.broadcast_to(are_ref[:, cols], (S5_NSEG, CW))
        ai = jnp.broadcast_to(aim_ref[:, cols], (S5_NSEG, CW))

        def step(s, carry, cols=cols, ar=ar, ai=ai):
            sr, si = carry
            rows = pl.ds(pl.multiple_of(s * S5_NSEG, S5_NSEG), S5_NSEG)
            nr = ar * sr - ai * si + xr_ref[rows, cols]
            ni = ar * si + ai * sr + xi_ref[rows, cols]
            xr_ref[rows, cols] = nr
            xi_ref[rows, cols] = ni
            return nr, ni

        zero = jnp.zeros((S5_NSEG, CW), F32)
        er, ei = lax.fori_loop(0, TS, step, (zero, zero), unroll=4)
        er_ref[:, cols] = er
        ei_ref[:, cols] = ei

    pwr, pwi = pr_ref[TS - 1:TS, :], pi_ref[TS - 1:TS, :]
    c_r, c_i = cr_ref[S5_NSEG:S5_NSEG + 1, :], ci_ref[S5_NSEG:S5_NSEG + 1, :]
    for j in range(S5_NSEG):
        cr_ref[j:j + 1, :] = c_r
        ci_ref[j:j + 1, :] = c_i
        c_r, c_i = (er_ref[j:j + 1, :] + pwr * c_r - pwi * c_i,
                    ei_ref[j:j + 1, :] + pwr * c_i + pwi * c_r)
    cr_ref[S5_NSEG:S5_NSEG + 1, :] = c_r
    ci_ref[S5_NSEG:S5_NSEG + 1, :] = c_i

    for cg in range(S5_NSTATE // CW):
        cols = slice(cg * CW, (cg + 1) * CW)
        c_r, c_i = cr_ref[0:S5_NSEG, cols], ci_ref[0:S5_NSEG, cols]

        def fix(s, carry, cols=cols, c_r=c_r, c_i=c_i):
            rows = pl.ds(pl.multiple_of(s * S5_NSEG, S5_NSEG), S5_NSEG)
            p_r, p_i = pr_ref[pl.ds(s, 1), cols], pi_ref[pl.ds(s, 1), cols]
            xr_ref[rows, cols] = xr_ref[rows, cols] + (p_r * c_r - p_i * c_i)
            xi_ref[rows, cols] = xi_ref[rows, cols] + (p_r * c_i + p_i * c_r)
            return carry

        lax.fori_loop(0, TS, fix, 0, unroll=4)

    ys = []
    for k in range(S5_SLICES):
        xr = xr_ref[:, k * W:(k + 1) * W].astype(BF16)
        xi = xi_ref[:, k * W:(k + 1) * W].astype(BF16)
        ys.append(jnp.dot(xr, ccr_ref[k], preferred_element_type=F32)
                  + jnp.dot(xi, cci_ref[k], preferred_element_type=F32))
    y = jnp.concatenate(ys, axis=-1) + d_ref[...] * u_ref[...]
    y = jax.nn.gelu(y, approximate=True)
    z = jnp.dot(y.astype(BF16), wglu_ref[...], preferred_element_type=F32) + bglu_ref[...]
    o_ref[...] = (y * jax.nn.sigmoid(z)).astype(o_ref.dtype)


def _s5_mixer(proj, a_re, a_im, bb_re, bb_im, c_re, c_im, d_skip, w_glu, b_glu):
    W = S5_NSTATE // S5_SLICES
    bbr = _block_diag_slices(bb_re).astype(BF16)
    bbi = _block_diag_slices(bb_im).astype(BF16)
    ccr = _block_diag_slices(jnp.swapaxes(c_re, 1, 2)).astype(BF16)
    cci = _block_diag_slices(-jnp.swapaxes(c_im, 1, 2)).astype(BF16)
    const3 = lambda t: (0, 0, 0)
    const2 = lambda t: (0, 0)
    return pl.pallas_call(
        _s5_kernel,
        grid=(SEQ // S5_T,),
        in_specs=[
            pl.BlockSpec((S5_T, D_A), lambda t: (t, 0)),
            pl.BlockSpec((S5_SLICES, LANES, W), const3),
            pl.BlockSpec((S5_SLICES, LANES, W), const3),
            pl.BlockSpec((1, S5_NSTATE), const2),
            pl.BlockSpec((1, S5_NSTATE), const2),
            pl.BlockSpec((S5_SLICES, W, LANES), const3),
            pl.BlockSpec((S5_SLICES, W, LANES), const3),
            pl.BlockSpec((1, D_A), const2),
            pl.BlockSpec((D_A, D_A), const2),
            pl.BlockSpec((1, D_A), const2),
        ],
        out_specs=pl.BlockSpec((S5_T, D_A), lambda t: (t, 0)),
        out_shape=jax.ShapeDtypeStruct((SEQ, D_A), BF16),
        scratch_shapes=[pltpu.VMEM((S5_T, S5_NSTATE), F32), pltpu.VMEM((S5_T, S5_NSTATE), F32),
                        pltpu.VMEM((S5_T // S5_NSEG, S5_NSTATE), F32), pltpu.VMEM((S5_T // S5_NSEG, S5_NSTATE), F32),
                        pltpu.VMEM((S5_NSEG, S5_NSTATE), F32), pltpu.VMEM((S5_NSEG, S5_NSTATE), F32),
                        pltpu.VMEM((2 * S5_NSEG, S5_NSTATE), F32), pltpu.VMEM((2 * S5_NSEG, S5_NSTATE), F32)],
        compiler_params=_cparams(("arbitrary",)),
        name="s5_mixer",
    )(proj, bbr, bbi, a_re.reshape(1, S5_NSTATE), a_im.reshape(1, S5_NSTATE), ccr, cci,
      d_skip.reshape(1, D_A), w_glu, b_glu.reshape(1, D_A))


def _kv_kernel(lat_ref, g_ref, wk_ref, wvt_ref, k_ref, vt_ref):
    x = lat_ref[...]
    ms = jnp.mean(x * x, axis=-1, keepdims=True)
    xn = ((x * lax.rsqrt(ms + EPS)) * g_ref[...]).astype(BF16)
    k_ref[...] = jnp.dot(xn, wk_ref[...], preferred_element_type=F32).astype(k_ref.dtype)
    for j in range(vt_ref.shape[0]):
        vt = lax.dot_general(wvt_ref[...], xn[j * DSA_SUB:(j + 1) * DSA_SUB], (((1,), (1,)), ((), ())),
                             preferred_element_type=F32)
        vt_ref[j] = vt.astype(vt_ref.dtype)


def _kv_up(proj, g, w):
    tm = DSA_KT
    lat_block = (2 * D_A + IDX_HEADS * IDX_DIM) // KV_RANK
    wk = w[:, :D_B]
    wvt = w[:, D_B:].T
    return pl.pallas_call(
        _kv_kernel,
        grid=(SEQ // tm,),
        in_specs=[
            pl.BlockSpec((tm, KV_RANK), lambda i: (i, lat_block)),
            pl.BlockSpec((1, KV_RANK), lambda i: (0, 0)),
            pl.BlockSpec((KV_RANK, D_B), lambda i: (0, 0)),
            pl.BlockSpec((D_B, KV_RANK), lambda i: (0, 0)),
        ],
        out_specs=[pl.BlockSpec((tm, D_B), lambda i: (i, 0)),
                   pl.BlockSpec((tm // DSA_SUB, D_B, DSA_SUB), lambda i: (i, 0, 0))],
        out_shape=[jax.ShapeDtypeStruct((SEQ, D_B), BF16),
                   jax.ShapeDtypeStruct((SEQ // DSA_SUB, D_B, DSA_SUB), BF16)],
        compiler_params=_cparams(("arbitrary",)),
        name="dsa_kv_up",
    )(proj, g.reshape(1, KV_RANK), wk, wvt)


def _dsa_kernel(q_ref, qidx_ref, w_ref, kidx_ref, k_ref, vt_ref, o_ref,
                sc_ref, a2_ref, mb_ref, lg_ref, pb_ref, qst_ref, qi_ref, wt_ref, thr_ref, lo_ref, hi_ref, clo_ref,
                m_ref, l_ref, acc_ref):
    TQ, KT, SUB = DSA_TQ, DSA_KT, DSA_SUB
    K = float(TOPK)
    i = pl.program_id(0)
    kt = pl.program_id(1)
    last_kt = (i * TQ + TQ - 1) // KT
    n_sub = i + 1
    q_pos = i * TQ + lax.broadcasted_iota(jnp.int32, (1, TQ), 1)
    q_chunk = q_pos // CHUNK

    @pl.when(jnp.logical_and(i == 0, kt == 0))
    def _():
        d = (lax.broadcasted_iota(jnp.int32, (KT, TQ), 1) - lax.broadcasted_iota(jnp.int32, (KT, TQ), 0)).astype(F32)
        for h in range(B_HEADS):
            a2_ref[h] = (LOG2E * 2.0 ** (-8.0 * (h + 1) / B_HEADS)) * d

    @pl.when(kt == 0)
    def _():
        qst_ref[...] = (q_ref[...] * (HEAD_DIM ** -0.5 * LOG2E)).T.astype(BF16)
        qv = qidx_ref[...]
        for h in range(IDX_HEADS):
            qi_ref[h] = qv[:, h * IDX_DIM:(h + 1) * IDX_DIM].astype(BF16)
        wt_ref[...] = w_ref[...].T[IDX_DIM:IDX_DIM + IDX_HEADS, :]

        def scores(g):
            s0 = pl.multiple_of(g * SUB, SUB)
            kk = kidx_ref[pl.ds(s0, SUB), :IDX_DIM].astype(BF16)
            tot = jnp.zeros((SUB, TQ), F32)
            for h in range(IDX_HEADS):
                s = lax.dot_general(kk, qi_ref[h], (((1,), (1,)), ((), ())), preferred_element_type=F32)
                tot = tot + jnp.maximum(s, 0.0) * wt_ref[h:h + 1, :]
            return tot * ((IDX_DIM ** -0.5) * (IDX_HEADS ** -0.5))

        def past_tile(g, carry):
            mx, mn = carry
            tot = scores(g)
            sc_ref[g] = tot
            return (jnp.maximum(mx, jnp.max(tot, axis=0, keepdims=True)),
                    jnp.minimum(mn, jnp.min(tot, axis=0, keepdims=True)))

        mx, mn = lax.fori_loop(0, i, past_tile,
                               (jnp.full((1, TQ), -jnp.inf, F32), jnp.full((1, TQ), jnp.inf, F32)))
        tot = scores(i)
        key_chunk = (i * SUB + lax.broadcasted_iota(jnp.int32, (SUB, 1), 0)) // CHUNK
        adm = key_chunk <= q_chunk
        sc_ref[i] = jnp.where(adm, tot, -jnp.inf)
        mx = jnp.maximum(mx, jnp.max(jnp.where(adm, tot, -jnp.inf), axis=0, keepdims=True))
        mn = jnp.minimum(mn, jnp.min(jnp.where(adm, tot, jnp.inf), axis=0, keepdims=True))

        def fill_tile(g, c):
            sc_ref[g] = jnp.full((SUB, TQ), -jnp.inf, F32)
            return c

        lax.fori_loop(n_sub, (last_kt + 1) * (KT // SUB), fill_tile, 0)

        n_adm = ((q_chunk + 1) * CHUNK).astype(F32)
        keep_all = n_adm <= K
        lo_ref[...] = jnp.where(keep_all, -F32_MAX, mn)
        hi_ref[...] = mx
        clo_ref[...] = jnp.where(keep_all, K, n_adm)

        def count_ge(mid):
            def tile_count(g):
                ind = jnp.where(sc_ref[g] >= mid, 1.0, 0.0)
                return jnp.sum(ind.reshape(SUB // SUBLANES, SUBLANES, TQ), axis=0)

            def body(g2, acc):
                return acc + tile_count(2 * g2) + tile_count(2 * g2 + 1)

            acc = lax.fori_loop(0, n_sub // 2, body, jnp.zeros((SUBLANES, TQ), F32))
            odd = (n_sub % 2).astype(F32)
            acc = acc + odd * tile_count(n_sub - 1)
            return jnp.sum(acc, axis=0, keepdims=True)

        def n_open():
            return jnp.max(jnp.where(clo_ref[...] != K, 1.0, 0.0))

        def cond(c):
            it, open_rows = c
            return jnp.logical_and(it < 64, open_rows > 0.5)

        def body(c):
            it, _ = c
            lo, hi, clo = lo_ref[...], hi_ref[...], clo_ref[...]
            mid = lo + 0.5 * (hi - lo)
            c_mid = count_ge(mid)
            live = clo != K
            ge = jnp.logical_and(live, c_mid >= K)
            lt = jnp.logical_and(live, c_mid < K)
            lo_ref[...] = jnp.where(ge, mid, lo)
            clo_ref[...] = jnp.where(ge, c_mid, clo)
            hi_ref[...] = jnp.where(lt, mid, hi)
            return it + 1, n_open()

        lax.while_loop(cond, body, (jnp.int32(0), n_open()))
        thr_ref[...] = lo_ref[...]

        m_ref[...] = jnp.full(m_ref.shape, NEG_BIG, F32)
        l_ref[...] = jnp.zeros_like(l_ref)
        acc_ref[...] = jnp.zeros_like(acc_ref)

    def attend(last):
        nj = KT // SUB
        thr = thr_ref[...]
        for jj in range(nj):
            mb_ref[jj * SUB:(jj + 1) * SUB, :] = jnp.where(sc_ref[kt * nj + jj] >= thr, 0.0, NEG_BIG)
        gap = (i * TQ - kt * KT).astype(F32)
        m_all, l_all = m_ref[...], l_ref[...]
        m_rows, l_rows = [], []

        def qk(h):
            hs = slice(h * HEAD_DIM, (h + 1) * HEAD_DIM)
            for half in range(2):
                rows = slice(half * (KT // 2), (half + 1) * (KT // 2))
                lg_ref[h % 2, rows, :] = jnp.dot(k_ref[rows, hs], qst_ref[hs, :], preferred_element_type=F32)

        qk(0)
        for h in range(B_HEADS):
            if h + 1 < B_HEADS:
                qk(h + 1)
            hs = slice(h * HEAD_DIM, (h + 1) * HEAD_DIM)
            slope2 = LOG2E * 2.0 ** (-8.0 * (h + 1) / B_HEADS)
            if last:
                lg = (lg_ref[h % 2] - jnp.abs(a2_ref[h] + slope2 * gap)) + mb_ref[...]
                off = 0.0
            else:
                lg = (lg_ref[h % 2] - a2_ref[h]) + mb_ref[...]
                off = slope2 * gap
            lg_ref[h % 2] = lg
            m_old = m_all[h:h + 1, :]
            m_new = jnp.maximum(m_old, jnp.max(lg, axis=0, keepdims=True) - off)
            alpha = jnp.exp2(m_old - m_new)
            p = jnp.exp2(lg_ref[h % 2] - (m_new + off))
            l_rows.append(alpha * l_all[h:h + 1, :] + jnp.sum(p, axis=0, keepdims=True))
            m_rows.append(m_new)
            pb_ref[h % 2] = p.astype(BF16)
            pv = jnp.dot(vt_ref[0, hs, :], pb_ref[h % 2, :SUB, :], preferred_element_type=F32)
            for jj in range(1, nj):
                pv += jnp.dot(vt_ref[jj, hs, :], pb_ref[h % 2, jj * SUB:(jj + 1) * SUB, :],
                              preferred_element_type=F32)
            acc_ref[hs, :] = alpha * acc_ref[hs, :] + pv
        m_ref[...] = jnp.concatenate(m_rows, axis=0)
        l_ref[...] = jnp.concatenate(l_rows, axis=0)

    @pl.when(kt < last_kt)
    def _():
        attend(False)

    @pl.when(kt == last_kt)
    def _():
        attend(True)

    @pl.when(kt == last_kt)
    def _():
        for h in range(B_HEADS):
            hs = slice(h * HEAD_DIM, (h + 1) * HEAD_DIM)
            o_ref[:, hs] = (acc_ref[hs, :] / l_ref[h:h + 1, :]).T.astype(o_ref.dtype)


def _dsa_attention(proj, k, vt):
    TQ, KT = DSA_TQ, DSA_KT
    nqb, nkt = SEQ // TQ, SEQ // KT
    kw_block = (2 * D_A + IDX_HEADS * IDX_DIM + KV_RANK) // LANES

    def last_tile(i, kt):
        return jnp.minimum(kt, (i * TQ + TQ - 1) // KT)

    return pl.pallas_call(
        _dsa_kernel,
        grid=(nqb, nkt),
        in_specs=[
            pl.BlockSpec((TQ, D_B), lambda i, kt: (i, 1)),
            pl.BlockSpec((TQ, IDX_HEADS * IDX_DIM), lambda i, kt: (i, 2)),
            pl.BlockSpec((TQ, LANES), lambda i, kt: (i, kw_block)),
            pl.BlockSpec((SEQ, LANES), lambda i, kt: (0, kw_block)),
            pl.BlockSpec((KT, D_B), lambda i, kt: (last_tile(i, kt), 0)),
            pl.BlockSpec((KT // DSA_SUB, D_B, DSA_SUB), lambda i, kt: (last_tile(i, kt), 0, 0)),
        ],
        out_specs=pl.BlockSpec((TQ, D_B), lambda i, kt: (i, 0)),
        out_shape=jax.ShapeDtypeStruct((SEQ, D_B), BF16),
        scratch_shapes=[
            pltpu.VMEM((SEQ // DSA_SUB, DSA_SUB, TQ), F32),
            pltpu.VMEM((B_HEADS, KT, TQ), F32),
            pltpu.VMEM((KT, TQ), F32),
            pltpu.VMEM((2, KT, TQ), F32),
            pltpu.VMEM((2, KT, TQ), BF16),
            pltpu.VMEM((D_B, TQ), BF16),
            pltpu.VMEM((IDX_HEADS, TQ, IDX_DIM), BF16),
            pltpu.VMEM((IDX_HEADS, TQ), F32),
            pltpu.VMEM((1, TQ), F32),
            pltpu.VMEM((1, TQ), F32),
            pltpu.VMEM((1, TQ), F32),
            pltpu.VMEM((1, TQ), F32),
            pltpu.VMEM((B_HEADS, TQ), F32),
            pltpu.VMEM((B_HEADS, TQ), F32),
            pltpu.VMEM((D_B, TQ), F32),
        ],
        compiler_params=_cparams(("arbitrary", "arbitrary")),
        name="dsa_attention",
    )(proj, proj, proj, proj, k, vt)


def _ca_kernel(q_ref, k0_ref, k1_ref, k2_ref, v0_ref, v1_ref, v2_ref, relw_ref, o_ref, bias_ref, lg_ref, pb_ref):
    i = pl.program_id(1)
    k_refs = (k0_ref, k1_ref, k2_ref)
    v_refs = (v0_ref, v1_ref, v2_ref)
    width = CA_NKB * CA_TQ

    @pl.when(i == 0)
    def _():
        r_chunk = lax.broadcasted_iota(jnp.int32, (CA_TQ, width), 0) // CHUNK
        c_chunk = lax.broadcasted_iota(jnp.int32, (CA_TQ, width), 1) // CHUNK
        band = jnp.logical_and(c_chunk >= r_chunk, c_chunk <= r_chunk + C_LEFT_CHUNKS)
        for hh in range(CA_HB):
            row = jnp.broadcast_to(relw_ref[hh], (CA_TQ, CA_PERIOD))
            rolled = pltpu.roll(row, CA_PERIOD - (CA_TQ - 1), 1, stride=1, stride_axis=0)
            bias_ref[hh] = jnp.where(band, rolled[:, :width] * LOG2E, NEG_BIG)

    def qk(hh):
        hs = slice(hh * HEAD_DIM, (hh + 1) * HEAD_DIM)
        qh = q_ref[:, hs]
        for j in range(CA_NKB):
            cols = slice(j * CA_TQ, (j + 1) * CA_TQ)
            lg = lax.dot_general(qh, k_refs[j][:, hs], (((1,), (1,)), ((), ())), preferred_element_type=F32)
            lg = lg * (HEAD_DIM ** -0.5 * LOG2E) + bias_ref[hh, :, cols]
            lg_ref[hh % 2, :, cols] = jnp.where(i + j >= CA_NKB - 1, lg, NEG_BIG)

    qk(0)
    for hh in range(CA_HB):
        if hh + 1 < CA_HB:
            qk(hh + 1)
        hs = slice(hh * HEAD_DIM, (hh + 1) * HEAD_DIM)
        lg = lg_ref[hh % 2]
        m = jnp.max(lg, axis=-1, keepdims=True)
        p = jnp.exp2(lg - m)
        l = jnp.sum(p, axis=-1, keepdims=True)
        pb_ref[hh % 2] = p.astype(BF16)
        acc = jnp.dot(pb_ref[hh % 2, :, :CA_TQ], v_refs[0][:, hs], preferred_element_type=F32)
        for j in range(1, CA_NKB):
            acc += jnp.dot(pb_ref[hh % 2, :, j * CA_TQ:(j + 1) * CA_TQ], v_refs[j][:, hs],
                           preferred_element_type=F32)
        o_ref[:, hs] = (acc / l).astype(o_ref.dtype)


def _ca_rel_row(rel_bias):
    n_heads = rel_bias.shape[0]
    span = CA_NKB * CA_TQ + CA_TQ - 1
    v = jnp.concatenate([rel_bias.astype(F32)[:, MAX_REL - CA_TQ + 1:],
                         jnp.broadcast_to(rel_bias.astype(F32)[:, -1:], (n_heads, span - MAX_REL - CA_TQ))], axis=1)
    return jnp.pad(v[:, ::-1], ((0, 0), (0, CA_PERIOD - span))).reshape(n_heads, 1, CA_PERIOD)


def _chunk_attention(qkv, rel_row):
    nt = SEQ // CA_TQ
    hw = CA_HB * HEAD_DIM
    nhb = C_HEADS // CA_HB

    def kmap(j, base):
        return lambda hb, i: (jnp.maximum(i - (CA_NKB - 1) + j, 0), base + hb)

    in_specs = [pl.BlockSpec((CA_TQ, hw), lambda hb, i: (i, hb))]
    in_specs += [pl.BlockSpec((CA_TQ, hw), kmap(j, nhb)) for j in range(CA_NKB)]
    in_specs += [pl.BlockSpec((CA_TQ, hw), kmap(j, 2 * nhb)) for j in range(CA_NKB)]
    in_specs += [pl.BlockSpec((CA_HB, 1, CA_PERIOD), lambda hb, i: (hb, 0, 0))]
    return pl.pallas_call(
        _ca_kernel,
        grid=(nhb, nt),
        in_specs=in_specs,
        out_specs=pl.BlockSpec((CA_TQ, hw), lambda hb, i: (i, hb)),
        out_shape=jax.ShapeDtypeStruct((SEQ, D_MODEL), BF16),
        scratch_shapes=[pltpu.VMEM((CA_HB, CA_TQ, CA_NKB * CA_TQ), F32),
                        pltpu.VMEM((2, CA_TQ, CA_NKB * CA_TQ), F32),
                        pltpu.VMEM((2, CA_TQ, CA_NKB * CA_TQ), BF16)],
        compiler_params=_cparams(("arbitrary", "arbitrary")),
        name="chunk_attention",
    )(qkv, qkv, qkv, qkv, qkv, qkv, qkv, rel_row)


def _pad_cols(w, n):
    return jnp.pad(w, ((0, 0), (0, n - w.shape[1])))


def kernel(x, c, ada_w, ada_b, norm_g, ffn_w_gate, ffn_w_up, ffn_w_down, ab_w_in, s5_lam_re, s5_lam_im, s5_log_dt, s5_b_re, s5_b_im, s5_c_re, s5_c_im, s5_d, s5_w_glu, s5_b_glu, dsa_kv_norm_g, dsa_w_kv_up, ab_w_out, c_w_qkv, c_rel_bias, c_w_out, final_norm_g):
    mod = _modulation(c, ada_w, ada_b)
    h = x.reshape(SEQ, D_MODEL)
    wg, wu, wd = ffn_w_gate, ffn_w_up, ffn_w_down

    for layer in range(DEPTH):
        h = _ffn(h, norm_g[layer, 0], mod[layer, 0], wg, wu, wd, layer, 0)
        if layer % 2 == 0:
            e = layer // 2
            w_in = ab_w_in[e]
            o_q, o_kv, o_qi = D_A, D_A + D_B, D_A + D_B + KV_RANK
            o_ki = o_qi + IDX_HEADS * IDX_DIM
            w_in = jnp.concatenate([w_in[:, :o_kv], w_in[:, o_qi:o_ki], w_in[:, o_kv:o_qi], w_in[:, o_ki:]], axis=1)
            proj = _norm_proj(h, norm_g[layer, 1], mod[layer, 1], _pad_cols(w_in, AB_N_PAD).astype(BF16), F32)

            a_re, a_im, bb_re, bb_im = _s5_params(s5_lam_re[e], s5_lam_im[e], s5_log_dt[e], s5_b_re[e], s5_b_im[e])
            n_blk, ts = SEQ // S5_T, S5_T // S5_NSEG
            u_il = (proj[:, :D_A].reshape(n_blk, S5_NSEG, ts, D_A).transpose(0, 2, 1, 3).reshape(SEQ, D_A))
            y_a = _s5_mixer(u_il, a_re, a_im, bb_re, bb_im, s5_c_re[e], s5_c_im[e], s5_d[e],
                            s5_w_glu[e].astype(BF16), s5_b_glu[e])
            y_a = y_a.reshape(n_blk, ts, S5_NSEG, D_A).transpose(0, 2, 1, 3).reshape(SEQ, D_A)

            k, vt = _kv_up(proj, dsa_kv_norm_g[e], dsa_w_kv_up[e].astype(BF16))
            y_b = _dsa_attention(proj, k, vt)

            w_out = ab_w_out[e].astype(BF16)
            h = _out_proj([y_a, y_b], [w_out[:D_A], w_out[D_A:]], h, mod[layer, 1])
        else:
            o = layer // 2
            qkv = _norm_proj(h, norm_g[layer, 1], mod[layer, 1], c_w_qkv[o].astype(BF16), BF16)
            att = _chunk_attention(qkv, _ca_rel_row(c_rel_bias[o]))
            h = _out_proj([att], [c_w_out[o].astype(BF16)], h, mod[layer, 1])
        h = _ffn(h, norm_g[layer, 2], mod[layer, 2], wg, wu, wd, layer, 1,
                 final_g=final_norm_g if layer == DEPTH - 1 else None)
    return h.reshape(1, SEQ, D_MODEL)
```

```python
import functools
import math

import jax
import jax.numpy as jnp
import numpy as np
from jax import lax
from jax.experimental import pallas as pl
from jax.experimental.pallas import tpu as pltpu

F32 = jnp.float32
BF16 = jnp.bfloat16

D_MODEL = 2048
SEQ = 8192
DEPTH = 2
CHUNK = 64
HEAD_DIM = 128
D_FF = 5504
N_SUB = 3
EPS = 1e-6
D_A = D_MODEL // 2
S5_GROUP = 16
S5_GROUPS = D_A // S5_GROUP
S5_STATE = 64
D_B = D_MODEL // 2
B_HEADS = D_B // HEAD_DIM
KV_RANK = D_MODEL // 8
IDX_HEADS = 16
IDX_DIM = 64
TOPK = 256
C_HEADS = D_MODEL // HEAD_DIM
C_LEFT_CHUNKS = 8
MAX_REL = 256

LANES = 128
SUBLANES = 8
VMEM_LIMIT = 56 * 1024 * 1024
NEG_BIG = -1e30
LOG2E = math.log2(math.e)
F32_MAX = float(np.finfo(np.float32).max)

FFN_TM = 1024
FFN_TF = 256
FFN_VMEM_LIMIT = 60 * 1024 * 1024
PROJ_TM = 1024
PROJ_TN = 512
QKV_TN = 1024
OUT_TM = 512
AB_N_PAD = 3584
S5_T = 256
S5_NSEG = SUBLANES
S5_SLICES = D_A // LANES
S5_NSTATE = S5_GROUPS * S5_STATE
DSA_TQ = 256
DSA_KT = 1024
DSA_SUB = 256
CA_TQ = 256
CA_HB = 8
CA_NKB = 3
CA_PERIOD = 1024


def _cparams(sem):
    return pltpu.CompilerParams(dimension_semantics=sem, vmem_limit_bytes=VMEM_LIMIT)


def _mod_kernel(c_ref, w_ref, b_ref, o_ref):
    rows = 256
    tn = o_ref.shape[-1]

    def body(i, acc):
        r0 = pl.multiple_of(i * rows, rows)
        cc = c_ref[pl.ds(r0, rows), :]
        cc = cc * jax.nn.sigmoid(cc)
        w = w_ref[0, pl.ds(r0, rows), :]
        return acc + jnp.sum((w * cc).reshape(rows // SUBLANES, SUBLANES, tn), axis=0)

    acc = lax.fori_loop(0, D_MODEL // rows, body, jnp.zeros((SUBLANES, tn), F32))
    o_ref[0] = jnp.sum(acc, axis=0, keepdims=True) + b_ref[0]


def _modulation(c, ada_w, ada_b):
    n = N_SUB * 3 * D_MODEL
    tn = 1024
    c_col = c.reshape(D_MODEL, 1)
    out = pl.pallas_call(
        _mod_kernel,
        grid=(DEPTH, n // tn),
        in_specs=[
            pl.BlockSpec((D_MODEL, 1), lambda l, j: (0, 0)),
            pl.BlockSpec((1, D_MODEL, tn), lambda l, j: (l, 0, j)),
            pl.BlockSpec((1, 1, tn), lambda l, j: (l, 0, j)),
        ],
        out_specs=pl.BlockSpec((1, 1, tn), lambda l, j: (l, 0, j)),
        out_shape=jax.ShapeDtypeStruct((DEPTH, 1, n), F32),
        compiler_params=_cparams(("arbitrary", "arbitrary")),
        name="adaln_mod",
    )(c_col, ada_w, ada_b.reshape(DEPTH, 1, n))
    return out.reshape(DEPTH, N_SUB, 3, D_MODEL)


ADALN_ROWS = 32


def _adaln_to(hn_ref, h_ref, g_ref, mod_ref):
    g = g_ref[...]
    scale1 = 1.0 + mod_ref[1:2, :]
    shift = mod_ref[0:1, :]

    def body(r, c):
        rows = pl.ds(pl.multiple_of(r * ADALN_ROWS, ADALN_ROWS), ADALN_ROWS)
        x = h_ref[rows, :]
        ms = jnp.mean(x * x, axis=-1, keepdims=True)
        y = (x * lax.rsqrt(ms + EPS)) * g
        hn_ref[rows, :] = (y * scale1 + shift).astype(hn_ref.dtype)
        return c

    lax.fori_loop(0, h_ref.shape[0] // ADALN_ROWS, body, 0, unroll=4)


def _ffn_kernel(h_ref, g_ref, mod_ref, wg_ref, wu_ref, wd_ref, *rest, n_f, final):
    if final:
        fg_ref, o_ref, hn_ref = rest
    else:
        o_ref, hn_ref = rest
    f = pl.program_id(1)

    @pl.when(f == 0)
    def _():
        _adaln_to(hn_ref, h_ref, g_ref, mod_ref)
        o_ref[...] = jnp.zeros_like(o_ref)

    def accumulate(width):
        hn = hn_ref[...]
        gate = jnp.dot(hn, wg_ref[:, :width].astype(BF16), preferred_element_type=F32)
        up = jnp.dot(hn, wu_ref[:, :width].astype(BF16), preferred_element_type=F32)
        act = (gate * jax.nn.sigmoid(gate)) * up
        o_ref[...] += jnp.dot(act.astype(BF16), wd_ref[:width, :].astype(BF16), preferred_element_type=F32)

    @pl.when(f < n_f - 1)
    def _():
        accumulate(FFN_TF)

    @pl.when(f == n_f - 1)
    def _():
        accumulate(D_FF - (n_f - 1) * FFN_TF)
        half_gate = 0.5 * mod_ref[2:3, :]

        def finish(r, c):
            rows = pl.ds(pl.multiple_of(r * ADALN_ROWS, ADALN_ROWS), ADALN_ROWS)
            out = h_ref[rows, :] + half_gate * o_ref[rows, :]
            if final:
                ms = jnp.mean(out * out, axis=-1, keepdims=True)
                out = (out * lax.rsqrt(ms + EPS)) * fg_ref[...]
            o_ref[rows, :] = out
            return c

        lax.fori_loop(0, FFN_TM // ADALN_ROWS, finish, 0, unroll=4)


def _ffn(h, g, mod, wg, wu, wd, layer, which, final_g=None):
    n_f = pl.cdiv(D_FF, FFN_TF)
    final = final_g is not None
    in_specs = [
        pl.BlockSpec((FFN_TM, D_MODEL), lambda i, f: (i, 0)),
        pl.BlockSpec((1, D_MODEL), lambda i, f: (0, 0)),
        pl.BlockSpec((3, D_MODEL), lambda i, f: (0, 0)),
        pl.BlockSpec((None, None, D_MODEL, FFN_TF), lambda i, f: (layer, which, 0, f)),
        pl.BlockSpec((None, None, D_MODEL, FFN_TF), lambda i, f: (layer, which, 0, f)),
        pl.BlockSpec((None, None, FFN_TF, D_MODEL), lambda i, f: (layer, which, f, 0)),
    ]
    args = [h, g.reshape(1, D_MODEL), mod, wg, wu, wd]
    if final:
        in_specs.append(pl.BlockSpec((1, D_MODEL), lambda i, f: (0, 0)))
        args.append(final_g.reshape(1, D_MODEL))
    return pl.pallas_call(
        functools.partial(_ffn_kernel, n_f=n_f, final=final),
        grid=(SEQ // FFN_TM, n_f),
        in_specs=in_specs,
        out_specs=pl.BlockSpec((FFN_TM, D_MODEL), lambda i, f: (i, 0)),
        out_shape=jax.ShapeDtypeStruct((SEQ, D_MODEL), F32),
        scratch_shapes=[pltpu.VMEM((FFN_TM, D_MODEL), BF16)],
        compiler_params=pltpu.CompilerParams(dimension_semantics=("arbitrary", "arbitrary"),
                                             vmem_limit_bytes=FFN_VMEM_LIMIT),
        name="ffn_swiglu",
    )(*args)


def _proj_kernel(h_ref, g_ref, mod_ref, w_ref, o_ref, hn_ref):
    @pl.when(pl.program_id(1) == 0)
    def _():
        _adaln_to(hn_ref, h_ref, g_ref, mod_ref)

    o_ref[...] = jnp.dot(hn_ref[...], w_ref[...].astype(BF16), preferred_element_type=F32).astype(o_ref.dtype)


def _norm_proj(h, g, mod, w, out_dtype, tn):
    n = w.shape[1]
    return pl.pallas_call(
        _proj_kernel,
        grid=(SEQ // PROJ_TM, n // tn),
        in_specs=[
            pl.BlockSpec((PROJ_TM, D_MODEL), lambda i, j: (i, 0)),
            pl.BlockSpec((1, D_MODEL), lambda i, j: (0, 0)),
            pl.BlockSpec((3, D_MODEL), lambda i, j: (0, 0)),
            pl.BlockSpec((D_MODEL, tn), lambda i, j: (0, j)),
        ],
        out_specs=pl.BlockSpec((PROJ_TM, tn), lambda i, j: (i, j)),
        out_shape=jax.ShapeDtypeStruct((SEQ, n), out_dtype),
        scratch_shapes=[pltpu.VMEM((PROJ_TM, D_MODEL), BF16)],
        compiler_params=_cparams(("arbitrary", "arbitrary")),
        name="adaln_proj",
    )(h, g.reshape(1, D_MODEL), mod, w)


def _out_kernel(*refs, n_lhs):
    lhs = refs[:n_lhs]
    ws = refs[n_lhs:2 * n_lhs]
    h_ref, mod_ref, o_ref = refs[2 * n_lhs:]
    y = jnp.dot(lhs[0][...], ws[0][...], preferred_element_type=F32)
    for a_ref, w_ref in zip(lhs[1:], ws[1:]):
        y += jnp.dot(a_ref[...], w_ref[...], preferred_element_type=F32)
    o_ref[...] = h_ref[...] + mod_ref[2:3, :] * y


def _out_proj(lhs, ws, h, mod):
    n_lhs = len(lhs)
    in_specs = [pl.BlockSpec((OUT_TM, a.shape[1]), lambda i: (i, 0)) for a in lhs]
    in_specs += [pl.BlockSpec(w.shape, lambda i: (0, 0)) for w in ws]
    in_specs += [pl.BlockSpec((OUT_TM, D_MODEL), lambda i: (i, 0)), pl.BlockSpec((3, D_MODEL), lambda i: (0, 0))]
    return pl.pallas_call(
        functools.partial(_out_kernel, n_lhs=n_lhs),
        grid=(SEQ // OUT_TM,),
        in_specs=in_specs,
        out_specs=pl.BlockSpec((OUT_TM, D_MODEL), lambda i: (i, 0)),
        out_shape=jax.ShapeDtypeStruct((SEQ, D_MODEL), F32),
        compiler_params=_cparams(("arbitrary",)),
        name="out_proj_residual",
    )(*lhs, *ws, h, mod)


def _s5_param_kernel(lr_ref, li_ref, ldt_ref, br_ref, bi_ref, are_ref, aim_ref, bbr_ref, bbi_ref):
    lr = lr_ref[...]
    li = li_ref[...]
    dt = jnp.exp(ldt_ref[...])
    mag = jnp.exp(lr * dt)
    ab_re = mag * jnp.cos(li * dt)
    ab_im = mag * jnp.sin(li * dt)
    den = lr * lr + li * li
    nr = ab_re - 1.0
    f_re = (nr * lr + ab_im * li) / den
    f_im = (ab_im * lr - nr * li) / den
    are_ref[...] = ab_re
    aim_ref[...] = ab_im
    br = br_ref[...]
    bi = bi_ref[...]
    bbr_ref[...] = f_re * br - f_im * bi
    bbi_ref[...] = f_re * bi + f_im * br


def _s5_params(lam_re, lam_im, log_dt, b_re, b_im):
    G, P, CG = S5_GROUPS, S5_STATE, S5_GROUP
    return pl.pallas_call(
        _s5_param_kernel,
        out_shape=[jax.ShapeDtypeStruct((G, 1, P), F32), jax.ShapeDtypeStruct((G, 1, P), F32),
                   jax.ShapeDtypeStruct((G, CG, P), F32), jax.ShapeDtypeStruct((G, CG, P), F32)],
        name="s5_zoh_params",
    )(lam_re.reshape(G, 1, P), lam_im.reshape(G, 1, P), log_dt.reshape(G, 1, 1),
      jnp.swapaxes(b_re, 1, 2), jnp.swapaxes(b_im, 1, 2))


def _block_diag_slices(m):
    _, r, c = m.shape
    m4 = m.reshape(S5_SLICES, SUBLANES, r, c)
    eye = jnp.eye(SUBLANES, dtype=m.dtype)
    out = m4[:, :, :, None, :] * eye[None, :, None, :, None]
    return out.reshape(S5_SLICES, SUBLANES * r, SUBLANES * c)


def _s5_kernel(u_ref, bbr_ref, bbi_ref, are_ref, aim_ref, ccr_ref, cci_ref, d_ref, wglu_ref, bglu_ref,
               o_ref, xr_ref, xi_ref, pr_ref, pi_ref, er_ref, ei_ref, cr_ref, ci_ref):
    W = S5_NSTATE // S5_SLICES
    TS = S5_T // S5_NSEG
    CW = 512

    @pl.when(pl.program_id(0) == 0)
    def _():
        pr_ref[0:1, :] = are_ref[...]
        pi_ref[0:1, :] = aim_ref[...]

        def power(s, c):
            ar, ai = are_ref[...], aim_ref[...]
            qr, qi = pr_ref[pl.ds(s - 1, 1), :], pi_ref[pl.ds(s - 1, 1), :]
            pr_ref[pl.ds(s, 1), :] = ar * qr - ai * qi
            pi_ref[pl.ds(s, 1), :] = ar * qi + ai * qr
            return c

        lax.fori_loop(1, TS, power, 0)
        cr_ref[...] = jnp.zeros_like(cr_ref)
        ci_ref[...] = jnp.zeros_like(ci_ref)

    for k in range(S5_SLICES):
        uk = u_ref[:, k * LANES:(k + 1) * LANES].astype(BF16)
        xr_ref[:, k * W:(k + 1) * W] = jnp.dot(uk, bbr_ref[k], preferred_element_type=F32)
        xi_ref[:, k * W:(k + 1) * W] = jnp.dot(uk, bbi_ref[k], preferred_element_type=F32)

    for cg in range(S5_NSTATE // CW):
        cols = slice(cg * CW, (cg + 1) * CW)
        ar = jnp.broadcast_to(are_ref[:, cols], (S5_NSEG, CW))
        ai = jnp.broadcast_to(aim_ref[:, cols], (S5_NSEG, CW))

        def step(s, carry, cols=cols, ar=ar, ai=ai):
            sr, si = carry
            rows = pl.ds(pl.multiple_of(s * S5_NSEG, S5_NSEG), S5_NSEG)
            nr = ar * sr - ai * si + xr_ref[rows, cols]
            ni = ar * si + ai * sr + xi_ref[rows, cols]
            xr_ref[rows, cols] = nr
            xi_ref[rows, cols] = ni
            return nr, ni

        zero = jnp.zeros((S5_NSEG, CW), F32)
        er, ei = lax.fori_loop(0, TS, step, (zero, zero), unroll=4)
        er_ref[:, cols] = er
        ei_ref[:, cols] = ei

    pwr, pwi = pr_ref[TS - 1:TS, :], pi_ref[TS - 1:TS, :]
    c_r, c_i = cr_ref[S5_NSEG:S5_NSEG + 1, :], ci_ref[S5_NSEG:S5_NSEG + 1, :]
    for j in range(S5_NSEG):
        cr_ref[j:j + 1, :] = c_r
        ci_ref[j:j + 1, :] = c_i
        c_r, c_i = (er_ref[j:j + 1, :] + pwr * c_r - pwi * c_i,
                    ei_ref[j:j + 1, :] + pwr * c_i + pwi * c_r)
    cr_ref[S5_NSEG:S5_NSEG + 1, :] = c_r
    ci_ref[S5_NSEG:S5_NSEG + 1, :] = c_i

    for cg in range(S5_NSTATE // CW):
        cols = slice(cg * CW, (cg + 1) * CW)
        c_r, c_i = cr_ref[0:S5_NSEG, cols], ci_ref[0:S5_NSEG, cols]

        def fix(s, carry, cols=cols, c_r=c_r, c_i=c_i):
            rows = pl.ds(pl.multiple_of(s * S5_NSEG, S5_NSEG), S5_NSEG)
            p_r, p_i = pr_ref[pl.ds(s, 1), cols], pi_ref[pl.ds(s, 1), cols]
            xr_ref[rows, cols] = xr_ref[rows, cols] + (p_r * c_r - p_i * c_i)
            xi_ref[rows, cols] = xi_ref[rows, cols] + (p_r * c_i + p_i * c_r)
            return carry

        lax.fori_loop(0, TS, fix, 0, unroll=4)

    ys = []
    for k in range(S5_SLICES):
        xr = xr_ref[:, k * W:(k + 1) * W].astype(BF16)
        xi = xi_ref[:, k * W:(k + 1) * W].astype(BF16)
        ys.append(jnp.dot(xr, ccr_ref[k], preferred_element_type=F32)
                  + jnp.dot(xi, cci_ref[k], preferred_element_type=F32))
    y = jnp.concatenate(ys, axis=-1) + d_ref[...] * u_ref[...]
    y = jax.nn.gelu(y, approximate=True)
    z = jnp.dot(y.astype(BF16), wglu_ref[...], preferred_element_type=F32) + bglu_ref[...]
    o_ref[...] = (y * jax.nn.sigmoid(z)).astype(o_ref.dtype)


def _s5_mixer(proj, a_re, a_im, bb_re, bb_im, c_re, c_im, d_skip, w_glu, b_glu):
    W = S5_NSTATE // S5_SLICES
    bbr = _block_diag_slices(bb_re).astype(BF16)
    bbi = _block_diag_slices(bb_im).astype(BF16)
    ccr = _block_diag_slices(jnp.swapaxes(c_re, 1, 2)).astype(BF16)
    cci = _block_diag_slices(-jnp.swapaxes(c_im, 1, 2)).astype(BF16)
    const3 = lambda t: (0, 0, 0)
    const2 = lambda t: (0, 0)
    return pl.pallas_call(
        _s5_kernel,
        grid=(SEQ // S5_T,),
        in_specs=[
            pl.BlockSpec((S5_T, D_A), lambda t: (t, 0)),
            pl.BlockSpec((S5_SLICES, LANES, W), const3),
            pl.BlockSpec((S5_SLICES, LANES, W), const3),
            pl.BlockSpec((1, S5_NSTATE), const2),
            pl.BlockSpec((1, S5_NSTATE), const2),
            pl.BlockSpec((S5_SLICES, W, LANES), const3),
            pl.BlockSpec((S5_SLICES, W, LANES), const3),
            pl.BlockSpec((1, D_A), const2),
            pl.BlockSpec((D_A, D_A), const2),
            pl.BlockSpec((1, D_A), const2),
        ],
        out_specs=pl.BlockSpec((S5_T, D_A), lambda t: (t, 0)),
        out_shape=jax.ShapeDtypeStruct((SEQ, D_A), BF16),
        scratch_shapes=[pltpu.VMEM((S5_T, S5_NSTATE), F32), pltpu.VMEM((S5_T, S5_NSTATE), F32),
                        pltpu.VMEM((S5_T // S5_NSEG, S5_NSTATE), F32), pltpu.VMEM((S5_T // S5_NSEG, S5_NSTATE), F32),
                        pltpu.VMEM((S5_NSEG, S5_NSTATE), F32), pltpu.VMEM((S5_NSEG, S5_NSTATE), F32),
                        pltpu.VMEM((2 * S5_NSEG, S5_NSTATE), F32), pltpu.VMEM((2 * S5_NSEG, S5_NSTATE), F32)],
        compiler_params=_cparams(("arbitrary",)),
        name="s5_mixer",
    )(proj, bbr, bbi, a_re.reshape(1, S5_NSTATE), a_im.reshape(1, S5_NSTATE), ccr, cci,
      d_skip.reshape(1, D_A), w_glu, b_glu.reshape(1, D_A))


def _kv_kernel(lat_ref, g_ref, wk_ref, wvt_ref, k_ref, vt_ref):
    x = lat_ref[...]
    ms = jnp.mean(x * x, axis=-1, keepdims=True)
    xn = ((x * lax.rsqrt(ms + EPS)) * g_ref[...]).astype(BF16)
    k_ref[...] = jnp.dot(xn, wk_ref[...], preferred_element_type=F32).astype(k_ref.dtype)
    for j in range(vt_ref.shape[0]):
        vt = lax.dot_general(wvt_ref[...], xn[j * DSA_SUB:(j + 1) * DSA_SUB], (((1,), (1,)), ((), ())),
                             preferred_element_type=F32)
        vt_ref[j] = vt.astype(vt_ref.dtype)


def _kv_up(proj, g, w):
    tm = DSA_KT
    lat_block = (2 * D_A + IDX_HEADS * IDX_DIM) // KV_RANK
    wk = w[:, :D_B]
    wvt = w[:, D_B:].T
    return pl.pallas_call(
        _kv_kernel,
        grid=(SEQ // tm,),
        in_specs=[
            pl.BlockSpec((tm, KV_RANK), lambda i: (i, lat_block)),
            pl.BlockSpec((1, KV_RANK), lambda i: (0, 0)),
            pl.BlockSpec((KV_RANK, D_B), lambda i: (0, 0)),
            pl.BlockSpec((D_B, KV_RANK), lambda i: (0, 0)),
        ],
        out_specs=[pl.BlockSpec((tm, D_B), lambda i: (i, 0)),
                   pl.BlockSpec((tm // DSA_SUB, D_B, DSA_SUB), lambda i: (i, 0, 0))],
        out_shape=[jax.ShapeDtypeStruct((SEQ, D_B), BF16),
                   jax.ShapeDtypeStruct((SEQ // DSA_SUB, D_B, DSA_SUB), BF16)],
        compiler_params=_cparams(("arbitrary",)),
        name="dsa_kv_up",
    )(proj, g.reshape(1, KV_RANK), wk, wvt)


def _dsa_kernel(qb_ref, kt_ref, q_ref, qidx_ref, w_ref, kidx_ref, k_ref, vt_ref, o_ref,
                sc_ref, a2_ref, mb_ref, lg_ref, pb_ref, qst_ref, qi_ref, wt_ref, thr_ref, lo_ref, hi_ref, clo_ref,
                m_ref, l_ref, acc_ref):
    TQ, KT, SUB = DSA_TQ, DSA_KT, DSA_SUB
    K = float(TOPK)
    step = pl.program_id(0)
    i = qb_ref[step]
    kt = kt_ref[step]
    last_kt = (i * TQ + TQ - 1) // KT
    n_sub = i + 1
    q_pos = i * TQ + lax.broadcasted_iota(jnp.int32, (1, TQ), 1)
    q_chunk = q_pos // CHUNK

    @pl.when(step == 0)
    def _():
        d = (lax.broadcasted_iota(jnp.int32, (KT, TQ), 1) - lax.broadcasted_iota(jnp.int32, (KT, TQ), 0)).astype(F32)
        for h in range(B_HEADS):
            a2_ref[h] = (LOG2E * 2.0 ** (-8.0 * (h + 1) / B_HEADS)) * d

    @pl.when(kt == 0)
    def _():
        qst_ref[...] = (q_ref[...] * (HEAD_DIM ** -0.5 * LOG2E)).T.astype(BF16)
        qv = qidx_ref[...]
        for h in range(IDX_HEADS):
            qi_ref[h] = qv[:, h * IDX_DIM:(h + 1) * IDX_DIM].astype(BF16)
        wt_ref[...] = w_ref[...].T[IDX_DIM:IDX_DIM + IDX_HEADS, :]

        def scores(g):
            s0 = pl.multiple_of(g * SUB, SUB)
            kk = kidx_ref[pl.ds(s0, SUB), :IDX_DIM].astype(BF16)
            tot = jnp.zeros((SUB, TQ), F32)
            for h in range(IDX_HEADS):
                s = lax.dot_general(kk, qi_ref[h], (((1,), (1,)), ((), ())), preferred_element_type=F32)
                tot = tot + jnp.maximum(s, 0.0) * wt_ref[h:h + 1, :]
            return tot * ((IDX_DIM ** -0.5) * (IDX_HEADS ** -0.5))

        def past_tile(g, carry):
            mx, mn = carry
            tot = scores(g)
            sc_ref[g] = tot
            return (jnp.maximum(mx, jnp.max(tot, axis=0, keepdims=True)),
                    jnp.minimum(mn, jnp.min(tot, axis=0, keepdims=True)))

        mx, mn = lax.fori_loop(0, i, past_tile,
                               (jnp.full((1, TQ), -jnp.inf, F32), jnp.full((1, TQ), jnp.inf, F32)))
        tot = scores(i)
        key_chunk = (i * SUB + lax.broadcasted_iota(jnp.int32, (SUB, 1), 0)) // CHUNK
        adm = key_chunk <= q_chunk
        sc_ref[i] = jnp.where(adm, tot, -jnp.inf)
        mx = jnp.maximum(mx, jnp.max(jnp.where(adm, tot, -jnp.inf), axis=0, keepdims=True))
        mn = jnp.minimum(mn, jnp.min(jnp.where(adm, tot, jnp.inf), axis=0, keepdims=True))

        def fill_tile(g, c):
            sc_ref[g] = jnp.full((SUB, TQ), -jnp.inf, F32)
            return c

        lax.fori_loop(n_sub, (last_kt + 1) * (KT // SUB), fill_tile, 0)

        n_adm = ((q_chunk + 1) * CHUNK).astype(F32)
        keep_all = n_adm <= K
        lo_ref[...] = jnp.where(keep_all, -F32_MAX, mn)
        hi_ref[...] = mx
        clo_ref[...] = jnp.where(keep_all, K, n_adm)

        def count_ge(mid):
            def tile_count(g):
                ind = jnp.where(sc_ref[g] >= mid, 1.0, 0.0)
                return jnp.sum(ind.reshape(SUB // SUBLANES, SUBLANES, TQ), axis=0)

            def body(g2, acc):
                return acc + tile_count(2 * g2) + tile_count(2 * g2 + 1)

            acc = lax.fori_loop(0, n_sub // 2, body, jnp.zeros((SUBLANES, TQ), F32))
            odd = (n_sub % 2).astype(F32)
            acc = acc + odd * tile_count(n_sub - 1)
            return jnp.sum(acc, axis=0, keepdims=True)

        def n_open():
            return jnp.max(jnp.where(clo_ref[...] != K, 1.0, 0.0))

        def cond(c):
            it, open_rows = c
            return jnp.logical_and(it < 64, open_rows > 0.5)

        def body(c):
            it, _ = c
            lo, hi, clo = lo_ref[...], hi_ref[...], clo_ref[...]
            mid = lo + 0.5 * (hi - lo)
            c_mid = count_ge(mid)
            live = clo != K
            ge = jnp.logical_and(live, c_mid >= K)
            lt = jnp.logical_and(live, c_mid < K)
            lo_ref[...] = jnp.where(ge, mid, lo)
            clo_ref[...] = jnp.where(ge, c_mid, clo)
            hi_ref[...] = jnp.where(lt, mid, hi)
            return it + 1, n_open()

        lax.while_loop(cond, body, (jnp.int32(0), n_open()))
        thr_ref[...] = lo_ref[...]

        m_ref[...] = jnp.full(m_ref.shape, NEG_BIG, F32)
        l_ref[...] = jnp.zeros_like(l_ref)
        acc_ref[...] = jnp.zeros_like(acc_ref)

    def attend(last):
        nj = KT // SUB
        thr = thr_ref[...]
        for jj in range(nj):
            mb_ref[jj * SUB:(jj + 1) * SUB, :] = jnp.where(sc_ref[kt * nj + jj] >= thr, 0.0, NEG_BIG)
        gap = (i * TQ - kt * KT).astype(F32)
        m_all, l_all = m_ref[...], l_ref[...]
        m_rows, l_rows = [], []

        def qk(h):
            hs = slice(h * HEAD_DIM, (h + 1) * HEAD_DIM)
            for half in range(2):
                rows = slice(half * (KT // 2), (half + 1) * (KT // 2))
                lg_ref[h % 2, rows, :] = jnp.dot(k_ref[rows, hs], qst_ref[hs, :], preferred_element_type=F32)

        qk(0)
        for h in range(B_HEADS):
            if h + 1 < B_HEADS:
                qk(h + 1)
            hs = slice(h * HEAD_DIM, (h + 1) * HEAD_DIM)
            slope2 = LOG2E * 2.0 ** (-8.0 * (h + 1) / B_HEADS)
            if last:
                lg = (lg_ref[h % 2] - jnp.abs(a2_ref[h] + slope2 * gap)) + mb_ref[...]
                off = 0.0
            else:
                lg = (lg_ref[h % 2] - a2_ref[h]) + mb_ref[...]
                off = slope2 * gap
            lg_ref[h % 2] = lg
            m_old = m_all[h:h + 1, :]
            m_new = jnp.maximum(m_old, jnp.max(lg, axis=0, keepdims=True) - off)
            alpha = jnp.exp2(m_old - m_new)
            p = jnp.exp2(lg_ref[h % 2] - (m_new + off))
            l_rows.append(alpha * l_all[h:h + 1, :] + jnp.sum(p, axis=0, keepdims=True))
            m_rows.append(m_new)
            pb_ref[h % 2] = p.astype(BF16)
            pv = jnp.dot(vt_ref[0, hs, :], pb_ref[h % 2, :SUB, :], preferred_element_type=F32)
            for jj in range(1, nj):
                pv += jnp.dot(vt_ref[jj, hs, :], pb_ref[h % 2, jj * SUB:(jj + 1) * SUB, :],
                              preferred_element_type=F32)
            acc_ref[hs, :] = alpha * acc_ref[hs, :] + pv
        m_ref[...] = jnp.concatenate(m_rows, axis=0)
        l_ref[...] = jnp.concatenate(l_rows, axis=0)

    @pl.when(kt < last_kt)
    def _():
        attend(False)

    @pl.when(kt == last_kt)
    def _():
        attend(True)

    @pl.when(kt == last_kt)
    def _():
        for h in range(B_HEADS):
            hs = slice(h * HEAD_DIM, (h + 1) * HEAD_DIM)
            o_ref[:, hs] = (acc_ref[hs, :] / l_ref[h:h + 1, :]).T.astype(o_ref.dtype)


def _dsa_attention(proj, k, vt):
    TQ, KT = DSA_TQ, DSA_KT
    kw_block = (2 * D_A + IDX_HEADS * IDX_DIM + KV_RANK) // LANES
    pairs = [(i, kt) for i in range(SEQ // TQ) for kt in range((i * TQ + TQ - 1) // KT + 1)]
    qb = jnp.asarray([p[0] for p in pairs], jnp.int32)
    ktile = jnp.asarray([p[1] for p in pairs], jnp.int32)

    grid_spec = pltpu.PrefetchScalarGridSpec(
        num_scalar_prefetch=2,
        grid=(len(pairs),),
        in_specs=[
            pl.BlockSpec((TQ, D_B), lambda s, qb, kt: (qb[s], 1)),
            pl.BlockSpec((TQ, IDX_HEADS * IDX_DIM), lambda s, qb, kt: (qb[s], 2)),
            pl.BlockSpec((TQ, LANES), lambda s, qb, kt: (qb[s], kw_block)),
            pl.BlockSpec((SEQ, LANES), lambda s, qb, kt: (0, kw_block)),
            pl.BlockSpec((KT, D_B), lambda s, qb, kt: (kt[s], 0)),
            pl.BlockSpec((KT // DSA_SUB, D_B, DSA_SUB), lambda s, qb, kt: (kt[s], 0, 0)),
        ],
        out_specs=pl.BlockSpec((TQ, D_B), lambda s, qb, kt: (qb[s], 0)),
        scratch_shapes=[
            pltpu.VMEM((SEQ // DSA_SUB, DSA_SUB, TQ), F32),
            pltpu.VMEM((B_HEADS, KT, TQ), F32),
            pltpu.VMEM((KT, TQ), F32),
            pltpu.VMEM((2, KT, TQ), F32),
            pltpu.VMEM((2, KT, TQ), BF16),
            pltpu.VMEM((D_B, TQ), BF16),
            pltpu.VMEM((IDX_HEADS, TQ, IDX_DIM), BF16),
            pltpu.VMEM((IDX_HEADS, TQ), F32),
            pltpu.VMEM((1, TQ), F32),
            pltpu.VMEM((1, TQ), F32),
            pltpu.VMEM((1, TQ), F32),
            pltpu.VMEM((1, TQ), F32),
            pltpu.VMEM((B_HEADS, TQ), F32),
            pltpu.VMEM((B_HEADS, TQ), F32),
            pltpu.VMEM((D_B, TQ), F32),
        ],
    )
    return pl.pallas_call(
        _dsa_kernel,
        grid_spec=grid_spec,
        out_shape=jax.ShapeDtypeStruct((SEQ, D_B), BF16),
        compiler_params=_cparams(("arbitrary",)),
        name="dsa_attention",
    )(qb, ktile, proj, proj, proj, proj, k, vt)


def _ca_kernel(q_ref, k0_ref, k1_ref, k2_ref, v0_ref, v1_ref, v2_ref, relw_ref, o_ref, bias_ref, lg_ref, pb_ref):
    i = pl.program_id(1)
    k_refs = (k0_ref, k1_ref, k2_ref)
    v_refs = (v0_ref, v1_ref, v2_ref)
    width = CA_NKB * CA_TQ

    @pl.when(i == 0)
    def _():
        r_chunk = lax.broadcasted_iota(jnp.int32, (CA_TQ, width), 0) // CHUNK
        c_chunk = lax.broadcasted_iota(jnp.int32, (CA_TQ, width), 1) // CHUNK
        band = jnp.logical_and(c_chunk >= r_chunk, c_chunk <= r_chunk + C_LEFT_CHUNKS)
        for hh in range(CA_HB):
            row = jnp.broadcast_to(relw_ref[hh], (CA_TQ, CA_PERIOD))
            rolled = pltpu.roll(row, CA_PERIOD - (CA_TQ - 1), 1, stride=1, stride_axis=0)
            bias_ref[hh] = jnp.where(band, rolled[:, :width] * LOG2E, NEG_BIG)

    def qk(hh):
        hs = slice(hh * HEAD_DIM, (hh + 1) * HEAD_DIM)
        qh = q_ref[:, hs]
        for j in range(CA_NKB):
            cols = slice(j * CA_TQ, (j + 1) * CA_TQ)
            lg = lax.dot_general(qh, k_refs[j][:, hs], (((1,), (1,)), ((), ())), preferred_element_type=F32)
            lg = lg * (HEAD_DIM ** -0.5 * LOG2E) + bias_ref[hh, :, cols]
            lg_ref[hh % 2, :, cols] = jnp.where(i + j >= CA_NKB - 1, lg, NEG_BIG)

    qk(0)
    for hh in range(CA_HB):
        if hh + 1 < CA_HB:
            qk(hh + 1)
        hs = slice(hh * HEAD_DIM, (hh + 1) * HEAD_DIM)
        lg = lg_ref[hh % 2]
        m = jnp.max(lg, axis=-1, keepdims=True)
        p = jnp.exp2(lg - m)
        l = jnp.sum(p, axis=-1, keepdims=True)
        pb_ref[hh % 2] = p.astype(BF16)
        acc = jnp.dot(pb_ref[hh % 2, :, :CA_TQ], v_refs[0][:, hs], preferred_element_type=F32)
        for j in range(1, CA_NKB):
            acc += jnp.dot(pb_ref[hh % 2, :, j * CA_TQ:(j + 1) * CA_TQ], v_refs[j][:, hs],
                           preferred_element_type=F32)
        o_ref[:, hs] = (acc / l).astype(o_ref.dtype)


def _ca_rel_row(rel_bias):
    n_heads = rel_bias.shape[0]
    span = CA_NKB * CA_TQ + CA_TQ - 1
    v = jnp.concatenate([rel_bias.astype(F32)[:, MAX_REL - CA_TQ + 1:],
                         jnp.broadcast_to(rel_bias.astype(F32)[:, -1:], (n_heads, span - MAX_REL - CA_TQ))], axis=1)
    return jnp.pad(v[:, ::-1], ((0, 0), (0, CA_PERIOD - span))).reshape(n_heads, 1, CA_PERIOD)


def _chunk_attention(qkv, rel_row):
    nt = SEQ // CA_TQ
    hw = CA_HB * HEAD_DIM
    nhb = C_HEADS // CA_HB

    def kmap(j, base):
        return lambda hb, i: (jnp.maximum(i - (CA_NKB - 1) + j, 0), base + hb)

    in_specs = [pl.BlockSpec((CA_TQ, hw), lambda hb, i: (i, hb))]
    in_specs += [pl.BlockSpec((CA_TQ, hw), kmap(j, nhb)) for j in range(CA_NKB)]
    in_specs += [pl.BlockSpec((CA_TQ, hw), kmap(j, 2 * nhb)) for j in range(CA_NKB)]
    in_specs += [pl.BlockSpec((CA_HB, 1, CA_PERIOD), lambda hb, i: (hb, 0, 0))]
    return pl.pallas_call(
        _ca_kernel,
        grid=(nhb, nt),
        in_specs=in_specs,
        out_specs=pl.BlockSpec((CA_TQ, hw), lambda hb, i: (i, hb)),
        out_shape=jax.ShapeDtypeStruct((SEQ, D_MODEL), BF16),
        scratch_shapes=[pltpu.VMEM((CA_HB, CA_TQ, CA_NKB * CA_TQ), F32),
                        pltpu.VMEM((2, CA_TQ, CA_NKB * CA_TQ), F32),
                        pltpu.VMEM((2, CA_TQ, CA_NKB * CA_TQ), BF16)],
        compiler_params=_cparams(("arbitrary", "arbitrary")),
        name="chunk_attention",
    )(qkv, qkv, qkv, qkv, qkv, qkv, qkv, rel_row)


def _pad_cols(w, n):
    return jnp.pad(w, ((0, 0), (0, n - w.shape[1])))


def kernel(x, c, ada_w, ada_b, norm_g, ffn_w_gate, ffn_w_up, ffn_w_down, ab_w_in, s5_lam_re, s5_lam_im, s5_log_dt, s5_b_re, s5_b_im, s5_c_re, s5_c_im, s5_d, s5_w_glu, s5_b_glu, dsa_kv_norm_g, dsa_w_kv_up, ab_w_out, c_w_qkv, c_rel_bias, c_w_out, final_norm_g):
    mod = _modulation(c, ada_w, ada_b)
    h = x.reshape(SEQ, D_MODEL)
    wg, wu, wd = ffn_w_gate, ffn_w_up, ffn_w_down

    for layer in range(DEPTH):
        h = _ffn(h, norm_g[layer, 0], mod[layer, 0], wg, wu, wd, layer, 0)
        if layer % 2 == 0:
            e = layer // 2
            w_in = ab_w_in[e]
            o_q, o_kv, o_qi = D_A, D_A + D_B, D_A + D_B + KV_RANK
            o_ki = o_qi + IDX_HEADS * IDX_DIM
            w_in = jnp.concatenate([w_in[:, :o_kv], w_in[:, o_qi:o_ki], w_in[:, o_kv:o_qi], w_in[:, o_ki:]], axis=1)
            proj = _norm_proj(h, norm_g[layer, 1], mod[layer, 1], _pad_cols(w_in, AB_N_PAD), F32, PROJ_TN)

            a_re, a_im, bb_re, bb_im = _s5_params(s5_lam_re[e], s5_lam_im[e], s5_log_dt[e], s5_b_re[e], s5_b_im[e])
            n_blk, ts = SEQ // S5_T, S5_T // S5_NSEG
            u_il = (proj[:, :D_A].reshape(n_blk, S5_NSEG, ts, D_A).transpose(0, 2, 1, 3).reshape(SEQ, D_A))
            y_a = _s5_mixer(u_il, a_re, a_im, bb_re, bb_im, s5_c_re[e], s5_c_im[e], s5_d[e],
                            s5_w_glu[e].astype(BF16), s5_b_glu[e])
            y_a = y_a.reshape(n_blk, ts, S5_NSEG, D_A).transpose(0, 2, 1, 3).reshape(SEQ, D_A)

            k, vt = _kv_up(proj, dsa_kv_norm_g[e], dsa_w_kv_up[e].astype(BF16))
            y_b = _dsa_attention(proj, k, vt)

            w_out = ab_w_out[e].astype(BF16)
            h = _out_proj([y_a, y_b], [w_out[:D_A], w_out[D_A:]], h, mod[layer, 1])
        else:
            o = layer // 2
            qkv = _norm_proj(h, norm_g[layer, 1], mod[layer, 1], c_w_qkv[o], BF16, QKV_TN)
            att = _chunk_attention(qkv, _ca_rel_row(c_rel_bias[o]))
            h = _out_proj([att], [c_w_out[o].astype(BF16)], h, mod[layer, 1])
        h = _ffn(h, norm_g[layer, 2], mod[layer, 2], wg, wu, wd, layer, 1,
                 final_g=final_norm_g if layer == DEPTH - 1 else None)
    return h.reshape(1, SEQ, D_MODEL)
```

```python
import functools
import math

import jax
import jax.numpy as jnp
import numpy as np
from jax import lax
from jax.experimental import pallas as pl
from jax.experimental.pallas import tpu as pltpu

F32 = jnp.float32
BF16 = jnp.bfloat16

D_MODEL = 2048
SEQ = 8192
DEPTH = 2
CHUNK = 64
HEAD_DIM = 128
D_FF = 5504
N_SUB = 3
EPS = 1e-6
D_A = D_MODEL // 2
S5_GROUP = 16
S5_GROUPS = D_A // S5_GROUP
S5_STATE = 64
D_B = D_MODEL // 2
B_HEADS = D_B // HEAD_DIM
KV_RANK = D_MODEL // 8
IDX_HEADS = 16
IDX_DIM = 64
TOPK = 256
C_HEADS = D_MODEL // HEAD_DIM
C_LEFT_CHUNKS = 8
MAX_REL = 256

LANES = 128
SUBLANES = 8
VMEM_LIMIT = 56 * 1024 * 1024
NEG_BIG = -1e30
LOG2E = math.log2(math.e)
F32_MAX = float(np.finfo(np.float32).max)

FFN_TM = 1024
FFN_TF = 256
FFN_VMEM_LIMIT = 60 * 1024 * 1024
PROJ_TM = 1024
PROJ_TN = 512
QKV_TN = 1024
OUT_TM = 512
AB_N_PAD = 3584
S5_T = 256
S5_NSEG = SUBLANES
S5_SLICES = D_A // LANES
S5_NSTATE = S5_GROUPS * S5_STATE
DSA_TQ = 256
DSA_KT = 1024
DSA_SUB = 256
CA_TQ = 256
CA_HB = 8
CA_NKB = 3
CA_PERIOD = 1024


def _cparams(sem):
    return pltpu.CompilerParams(dimension_semantics=sem, vmem_limit_bytes=VMEM_LIMIT)


def _mod_kernel(c_ref, w_ref, b_ref, o_ref):
    rows = 256
    tn = o_ref.shape[-1]

    def body(i, acc):
        r0 = pl.multiple_of(i * rows, rows)
        cc = c_ref[pl.ds(r0, rows), :]
        cc = cc * jax.nn.sigmoid(cc)
        w = w_ref[0, pl.ds(r0, rows), :]
        return acc + jnp.sum((w * cc).reshape(rows // SUBLANES, SUBLANES, tn), axis=0)

    acc = lax.fori_loop(0, D_MODEL // rows, body, jnp.zeros((SUBLANES, tn), F32))
    o_ref[0] = jnp.sum(acc, axis=0, keepdims=True) + b_ref[0]


def _modulation(c, ada_w, ada_b):
    n = N_SUB * 3 * D_MODEL
    tn = 1024
    c_col = c.reshape(D_MODEL, 1)
    out = pl.pallas_call(
        _mod_kernel,
        grid=(DEPTH, n // tn),
        in_specs=[
            pl.BlockSpec((D_MODEL, 1), lambda l, j: (0, 0)),
            pl.BlockSpec((1, D_MODEL, tn), lambda l, j: (l, 0, j)),
            pl.BlockSpec((1, 1, tn), lambda l, j: (l, 0, j)),
        ],
        out_specs=pl.BlockSpec((1, 1, tn), lambda l, j: (l, 0, j)),
        out_shape=jax.ShapeDtypeStruct((DEPTH, 1, n), F32),
        compiler_params=_cparams(("arbitrary", "arbitrary")),
        name="adaln_mod",
    )(c_col, ada_w, ada_b.reshape(DEPTH, 1, n))
    return out.reshape(DEPTH, N_SUB, 3, D_MODEL)


ADALN_ROWS = 32


def _adaln_to(hn_ref, h_ref, g_ref, mod_ref):
    g = g_ref[...]
    scale1 = 1.0 + mod_ref[1:2, :]
    shift = mod_ref[0:1, :]

    def body(r, c):
        rows = pl.ds(pl.multiple_of(r * ADALN_ROWS, ADALN_ROWS), ADALN_ROWS)
        x = h_ref[rows, :]
        ms = jnp.mean(x * x, axis=-1, keepdims=True)
        y = (x * lax.rsqrt(ms + EPS)) * g
        hn_ref[rows, :] = (y * scale1 + shift).astype(hn_ref.dtype)
        return c

    lax.fori_loop(0, h_ref.shape[0] // ADALN_ROWS, body, 0, unroll=4)


def _ffn_kernel(h_ref, g_ref, mod_ref, wg_hbm, wu_hbm, wd_hbm, *rest, layer, which, final):
    if final:
        fg_ref, o_ref, hn_ref, wg_buf, wu_buf, wd_buf, sem = rest
    else:
        o_ref, hn_ref, wg_buf, wu_buf, wd_buf, sem = rest
    n_full = D_FF // FFN_TF
    tail = D_FF - n_full * FFN_TF

    def tile_copies(f, slot, width):
        c0 = f * FFN_TF
        return (
            pltpu.make_async_copy(wg_hbm.at[layer, which, :, pl.ds(c0, width)],
                                  wg_buf.at[slot, :, pl.ds(0, width)], sem.at[0, slot]),
            pltpu.make_async_copy(wu_hbm.at[layer, which, :, pl.ds(c0, width)],
                                  wu_buf.at[slot, :, pl.ds(0, width)], sem.at[1, slot]),
            pltpu.make_async_copy(wd_hbm.at[layer, which, pl.ds(c0, width), :],
                                  wd_buf.at[slot, pl.ds(0, width), :], sem.at[2, slot]),
        )

    def accumulate(slot, width):
        hn = hn_ref[...]
        gate = jnp.dot(hn, wg_buf[slot, :, :width].astype(BF16), preferred_element_type=F32)
        up = jnp.dot(hn, wu_buf[slot, :, :width].astype(BF16), preferred_element_type=F32)
        act = (gate * jax.nn.sigmoid(gate)) * up
        o_ref[...] += jnp.dot(act.astype(BF16), wd_buf[slot, :width, :].astype(BF16), preferred_element_type=F32)

    for c in tile_copies(0, 0, FFN_TF):
        c.start()
    _adaln_to(hn_ref, h_ref, g_ref, mod_ref)
    o_ref[...] = jnp.zeros_like(o_ref)

    def tile(f, carry):
        slot = lax.rem(f, 2)

        @pl.when(f + 1 < n_full)
        def _():
            for c in tile_copies(f + 1, 1 - slot, FFN_TF):
                c.start()

        @pl.when(f + 1 == n_full)
        def _():
            for c in tile_copies(n_full, 1 - slot, tail):
                c.start()

        for c in tile_copies(f, slot, FFN_TF):
            c.wait()
        accumulate(slot, FFN_TF)
        return carry

    lax.fori_loop(0, n_full, tile, 0)
    for c in tile_copies(n_full, n_full % 2, tail):
        c.wait()
    accumulate(n_full % 2, tail)

    half_gate = 0.5 * mod_ref[2:3, :]

    def finish(r, c):
        rows = pl.ds(pl.multiple_of(r * ADALN_ROWS, ADALN_ROWS), ADALN_ROWS)
        out = h_ref[rows, :] + half_gate * o_ref[rows, :]
        if final:
            ms = jnp.mean(out * out, axis=-1, keepdims=True)
            out = (out * lax.rsqrt(ms + EPS)) * fg_ref[...]
        o_ref[rows, :] = out
        return c

    lax.fori_loop(0, FFN_TM // ADALN_ROWS, finish, 0, unroll=4)


def _ffn(h, g, mod, wg, wu, wd, layer, which, final_g=None):
    final = final_g is not None
    in_specs = [
        pl.BlockSpec((FFN_TM, D_MODEL), lambda i: (i, 0)),
        pl.BlockSpec((1, D_MODEL), lambda i: (0, 0)),
        pl.BlockSpec((3, D_MODEL), lambda i: (0, 0)),
        pl.BlockSpec(memory_space=pl.ANY),
        pl.BlockSpec(memory_space=pl.ANY),
        pl.BlockSpec(memory_space=pl.ANY),
    ]
    args = [h, g.reshape(1, D_MODEL), mod, wg, wu, wd]
    if final:
        in_specs.append(pl.BlockSpec((1, D_MODEL), lambda i: (0, 0)))
        args.append(final_g.reshape(1, D_MODEL))
    return pl.pallas_call(
        functools.partial(_ffn_kernel, layer=layer, which=which, final=final),
        grid=(SEQ // FFN_TM,),
        in_specs=in_specs,
        out_specs=pl.BlockSpec((FFN_TM, D_MODEL), lambda i: (i, 0)),
        out_shape=jax.ShapeDtypeStruct((SEQ, D_MODEL), F32),
        scratch_shapes=[pltpu.VMEM((FFN_TM, D_MODEL), BF16),
                        pltpu.VMEM((2, D_MODEL, FFN_TF), F32), pltpu.VMEM((2, D_MODEL, FFN_TF), F32),
                        pltpu.VMEM((2, FFN_TF, D_MODEL), F32),
                        pltpu.SemaphoreType.DMA((3, 2))],
        compiler_params=pltpu.CompilerParams(dimension_semantics=("arbitrary",),
                                             vmem_limit_bytes=FFN_VMEM_LIMIT),
        name="ffn_swiglu",
    )(*args)


def _proj_kernel(h_ref, g_ref, mod_ref, w_ref, o_ref, hn_ref):
    @pl.when(pl.program_id(1) == 0)
    def _():
        _adaln_to(hn_ref, h_ref, g_ref, mod_ref)

    o_ref[...] = jnp.dot(hn_ref[...], w_ref[...].astype(BF16), preferred_element_type=F32).astype(o_ref.dtype)


def _norm_proj(h, g, mod, w, out_dtype, tn):
    n = w.shape[1]
    return pl.pallas_call(
        _proj_kernel,
        grid=(SEQ // PROJ_TM, n // tn),
        in_specs=[
            pl.BlockSpec((PROJ_TM, D_MODEL), lambda i, j: (i, 0)),
            pl.BlockSpec((1, D_MODEL), lambda i, j: (0, 0)),
            pl.BlockSpec((3, D_MODEL), lambda i, j: (0, 0)),
            pl.BlockSpec((D_MODEL, tn), lambda i, j: (0, j)),
        ],
        out_specs=pl.BlockSpec((PROJ_TM, tn), lambda i, j: (i, j)),
        out_shape=jax.ShapeDtypeStruct((SEQ, n), out_dtype),
        scratch_shapes=[pltpu.VMEM((PROJ_TM, D_MODEL), BF16)],
        compiler_params=_cparams(("arbitrary", "arbitrary")),
        name="adaln_proj",
    )(h, g.reshape(1, D_MODEL), mod, w)


def _out_kernel(*refs, n_lhs):
    lhs = refs[:n_lhs]
    ws = refs[n_lhs:2 * n_lhs]
    h_ref, mod_ref, o_ref = refs[2 * n_lhs:]
    y = jnp.dot(lhs[0][...], ws[0][...], preferred_element_type=F32)
    for a_ref, w_ref in zip(lhs[1:], ws[1:]):
        y += jnp.dot(a_ref[...], w_ref[...], preferred_element_type=F32)
    o_ref[...] = h_ref[...] + mod_ref[2:3, :] * y


def _out_proj(lhs, ws, h, mod):
    n_lhs = len(lhs)
    in_specs = [pl.BlockSpec((OUT_TM, a.shape[1]), lambda i: (i, 0)) for a in lhs]
    in_specs += [pl.BlockSpec(w.shape, lambda i: (0, 0)) for w in ws]
    in_specs += [pl.BlockSpec((OUT_TM, D_MODEL), lambda i: (i, 0)), pl.BlockSpec((3, D_MODEL), lambda i: (0, 0))]
    return pl.pallas_call(
        functools.partial(_out_kernel, n_lhs=n_lhs),
        grid=(SEQ // OUT_TM,),
        in_specs=in_specs,
        out_specs=pl.BlockSpec((OUT_TM, D_MODEL), lambda i: (i, 0)),
        out_shape=jax.ShapeDtypeStruct((SEQ, D_MODEL), F32),
        compiler_params=_cparams(("arbitrary",)),
        name="out_proj_residual",
    )(*lhs, *ws, h, mod)


def _s5_param_kernel(lr_ref, li_ref, ldt_ref, br_ref, bi_ref, are_ref, aim_ref, bbr_ref, bbi_ref):
    lr = lr_ref[...]
    li = li_ref[...]
    dt = jnp.exp(ldt_ref[...])
    mag = jnp.exp(lr * dt)
    ab_re = mag * jnp.cos(li * dt)
    ab_im = mag * jnp.sin(li * dt)
    den = lr * lr + li * li
    nr = ab_re - 1.0
    f_re = (nr * lr + ab_im * li) / den
    f_im = (ab_im * lr - nr * li) / den
    are_ref[...] = ab_re
    aim_ref[...] = ab_im
    br = br_ref[...]
    bi = bi_ref[...]
    bbr_ref[...] = f_re * br - f_im * bi
    bbi_ref[...] = f_re * bi + f_im * br


def _s5_params(lam_re, lam_im, log_dt, b_re, b_im):
    G, P, CG = S5_GROUPS, S5_STATE, S5_GROUP
    return pl.pallas_call(
        _s5_param_kernel,
        out_shape=[jax.ShapeDtypeStruct((G, 1, P), F32), jax.ShapeDtypeStruct((G, 1, P), F32),
                   jax.ShapeDtypeStruct((G, CG, P), F32), jax.ShapeDtypeStruct((G, CG, P), F32)],
        name="s5_zoh_params",
    )(lam_re.reshape(G, 1, P), lam_im.reshape(G, 1, P), log_dt.reshape(G, 1, 1),
      jnp.swapaxes(b_re, 1, 2), jnp.swapaxes(b_im, 1, 2))


def _block_diag_slices(m):
    _, r, c = m.shape
    m4 = m.reshape(S5_SLICES, SUBLANES, r, c)
    eye = jnp.eye(SUBLANES, dtype=m.dtype)
    out = m4[:, :, :, None, :] * eye[None, :, None, :, None]
    return out.reshape(S5_SLICES, SUBLANES * r, SUBLANES * c)


def _s5_kernel(u_ref, bbr_ref, bbi_ref, are_ref, aim_ref, ccr_ref, cci_ref, d_ref, wglu_ref, bglu_ref,
               o_ref, xr_ref, xi_ref, pr_ref, pi_ref, er_ref, ei_ref, cr_ref, ci_ref):
    W = S5_NSTATE // S5_SLICES
    TS = S5_T // S5_NSEG
    CW = 512

    @pl.when(pl.program_id(0) == 0)
    def _():
        pr_ref[0:1, :] = are_ref[...]
        pi_ref[0:1, :] = aim_ref[...]

        def power(s, c):
            ar, ai = are_ref[...], aim_ref[...]
            qr, qi = pr_ref[pl.ds(s - 1, 1), :], pi_ref[pl.ds(s - 1, 1), :]
            pr_ref[pl.ds(s, 1), :] = ar * qr - ai * qi
            pi_ref[pl.ds(s, 1), :] = ar * qi + ai * qr
            return c

        lax.fori_loop(1, TS, power, 0)
        cr_ref[...] = jnp.zeros_like(cr_ref)
        ci_ref[...] = jnp.zeros_like(ci_ref)

    for k in range(S5_SLICES):
        uk = u_ref[:, k * LANES:(k + 1) * LANES].astype(BF16)
        xr_ref[:, k * W:(k + 1) * W] = jnp.dot(uk, bbr_ref[k], preferred_element_type=F32)
        xi_ref[:, k * W:(k + 1) * W] = jnp.dot(uk, bbi_ref[k], preferred_element_type=F32)

    for cg in range(S5_NSTATE // CW):
        cols = slice(cg * CW, (cg + 1) * CW)
        ar = jnp.broadcast_to(are_ref[:, cols], (S5_NSEG, CW))
        ai = jnp.broadcast_to(aim_ref[:, cols], (S5_NSEG, CW))

        def step(s, carry, cols=cols, ar=ar, ai=ai):
            sr, si = carry
            rows = pl.ds(pl.multiple_of(s * S5_NSEG, S5_NSEG), S5_NSEG)
            nr = ar * sr - ai * si + xr_ref[rows, cols]
            ni = ar * si + ai * sr + xi_ref[rows, cols]
            xr_ref[rows, cols] = nr
            xi_ref[rows, cols] = ni
            return nr, ni

        zero = jnp.zeros((S5_NSEG, CW), F32)
        er, ei = lax.fori_loop(0, TS, step, (zero, zero), unroll=4)
        er_ref[:, cols] = er
        ei_ref[:, cols] = ei

    pwr, pwi = pr_ref[TS - 1:TS, :], pi_ref[TS - 1:TS, :]
    c_r, c_i = cr_ref[S5_NSEG:S5_NSEG + 1, :], ci_ref[S5_NSEG:S5_NSEG + 1, :]
    for j in range(S5_NSEG):
        cr_ref[j:j + 1, :] = c_r
        ci_ref[j:j + 1, :] = c_i
        c_r, c_i = (er_ref[j:j + 1, :] + pwr * c_r - pwi * c_i,
                    ei_ref[j:j + 1, :] + pwr * c_i + pwi * c_r)
    cr_ref[S5_NSEG:S5_NSEG + 1, :] = c_r
    ci_ref[S5_NSEG:S5_NSEG + 1, :] = c_i

    for cg in range(S5_NSTATE // CW):
        cols = slice(cg * CW, (cg + 1) * CW)
        c_r, c_i = cr_ref[0:S5_NSEG, cols], ci_ref[0:S5_NSEG, cols]

        def fix(s, carry, cols=cols, c_r=c_r, c_i=c_i):
            rows = pl.ds(pl.multiple_of(s * S5_NSEG, S5_NSEG), S5_NSEG)
            p_r, p_i = pr_ref[pl.ds(s, 1), cols], pi_ref[pl.ds(s, 1), cols]
            xr_ref[rows, cols] = xr_ref[rows, cols] + (p_r * c_r - p_i * c_i)
            xi_ref[rows, cols] = xi_ref[rows, cols] + (p_r * c_i + p_i * c_r)
            return carry

        lax.fori_loop(0, TS, fix, 0, unroll=4)

    ys = []
    for k in range(S5_SLICES):
        xr = xr_ref[:, k * W:(k + 1) * W].astype(BF16)
        xi = xi_ref[:, k * W:(k + 1) * W].astype(BF16)
        ys.append(jnp.dot(xr, ccr_ref[k], preferred_element_type=F32)
                  + jnp.dot(xi, cci_ref[k], preferred_element_type=F32))
    y = jnp.concatenate(ys, axis=-1) + d_ref[...] * u_ref[...]
    y = jax.nn.gelu(y, approximate=True)
    z = jnp.dot(y.astype(BF16), wglu_ref[...], preferred_element_type=F32) + bglu_ref[...]
    o_ref[...] = (y * jax.nn.sigmoid(z)).astype(o_ref.dtype)


def _s5_mixer(proj, a_re, a_im, bb_re, bb_im, c_re, c_im, d_skip, w_glu, b_glu):
    W = S5_NSTATE // S5_SLICES
    bbr = _block_diag_slices(bb_re).astype(BF16)
    bbi = _block_diag_slices(bb_im).astype(BF16)
    ccr = _block_diag_slices(jnp.swapaxes(c_re, 1, 2)).astype(BF16)
    cci = _block_diag_slices(-jnp.swapaxes(c_im, 1, 2)).astype(BF16)
    const3 = lambda t: (0, 0, 0)
    const2 = lambda t: (0, 0)
    return pl.pallas_call(
        _s5_kernel,
        grid=(SEQ // S5_T,),
        in_specs=[
            pl.BlockSpec((S5_T, D_A), lambda t: (t, 0)),
            pl.BlockSpec((S5_SLICES, LANES, W), const3),
            pl.BlockSpec((S5_SLICES, LANES, W), const3),
            pl.BlockSpec((1, S5_NSTATE), const2),
            pl.BlockSpec((1, S5_NSTATE), const2),
            pl.BlockSpec((S5_SLICES, W, LANES), const3),
            pl.BlockSpec((S5_SLICES, W, LANES), const3),
            pl.BlockSpec((1, D_A), const2),
            pl.BlockSpec((D_A, D_A), const2),
            pl.BlockSpec((1, D_A), const2),
        ],
        out_specs=pl.BlockSpec((S5_T, D_A), lambda t: (t, 0)),
        out_shape=jax.ShapeDtypeStruct((SEQ, D_A), BF16),
        scratch_shapes=[pltpu.VMEM((S5_T, S5_NSTATE), F32), pltpu.VMEM((S5_T, S5_NSTATE), F32),
                        pltpu.VMEM((S5_T // S5_NSEG, S5_NSTATE), F32), pltpu.VMEM((S5_T // S5_NSEG, S5_NSTATE), F32),
                        pltpu.VMEM((S5_NSEG, S5_NSTATE), F32), pltpu.VMEM((S5_NSEG, S5_NSTATE), F32),
                        pltpu.VMEM((2 * S5_NSEG, S5_NSTATE), F32), pltpu.VMEM((2 * S5_NSEG, S5_NSTATE), F32)],
        compiler_params=_cparams(("arbitrary",)),
        name="s5_mixer",
    )(proj, bbr, bbi, a_re.reshape(1, S5_NSTATE), a_im.reshape(1, S5_NSTATE), ccr, cci,
      d_skip.reshape(1, D_A), w_glu, b_glu.reshape(1, D_A))


def _kv_kernel(lat_ref, g_ref, wk_ref, wvt_ref, k_ref, vt_ref):
    x = lat_ref[...]
    ms = jnp.mean(x * x, axis=-1, keepdims=True)
    xn = ((x * lax.rsqrt(ms + EPS)) * g_ref[...]).astype(BF16)
    k_ref[...] = jnp.dot(xn, wk_ref[...], preferred_element_type=F32).astype(k_ref.dtype)
    for j in range(vt_ref.shape[0]):
        vt = lax.dot_general(wvt_ref[...], xn[j * DSA_SUB:(j + 1) * DSA_SUB], (((1,), (1,)), ((), ())),
                             preferred_element_type=F32)
        vt_ref[j] = vt.astype(vt_ref.dtype)


def _kv_up(proj, g, w):
    tm = DSA_KT
    lat_block = (2 * D_A + IDX_HEADS * IDX_DIM) // KV_RANK
    wk = w[:, :D_B]
    wvt = w[:, D_B:].T
    return pl.pallas_call(
        _kv_kernel,
        grid=(SEQ // tm,),
        in_specs=[
            pl.BlockSpec((tm, KV_RANK), lambda i: (i, lat_block)),
            pl.BlockSpec((1, KV_RANK), lambda i: (0, 0)),
            pl.BlockSpec((KV_RANK, D_B), lambda i: (0, 0)),
            pl.BlockSpec((D_B, KV_RANK), lambda i: (0, 0)),
        ],
        out_specs=[pl.BlockSpec((tm, D_B), lambda i: (i, 0)),
                   pl.BlockSpec((tm // DSA_SUB, D_B, DSA_SUB), lambda i: (i, 0, 0))],
        out_shape=[jax.ShapeDtypeStruct((SEQ, D_B), BF16),
                   jax.ShapeDtypeStruct((SEQ // DSA_SUB, D_B, DSA_SUB), BF16)],
        compiler_params=_cparams(("arbitrary",)),
        name="dsa_kv_up",
    )(proj, g.reshape(1, KV_RANK), wk, wvt)


def _dsa_kernel(qb_ref, kt_ref, q_ref, qidx_ref, w_ref, kidx_ref, k_ref, vt_ref, o_ref,
                sc_ref, a2_ref, mb_ref, lg_ref, pb_ref, qst_ref, qi_ref, wt_ref, thr_ref, lo_ref, hi_ref, clo_ref,
                m_ref, l_ref, acc_ref):
    TQ, KT, SUB = DSA_TQ, DSA_KT, DSA_SUB
    K = float(TOPK)
    step = pl.program_id(0)
    i = qb_ref[step]
    kt = kt_ref[step]
    last_kt = (i * TQ + TQ - 1) // KT
    n_sub = i + 1
    q_pos = i * TQ + lax.broadcasted_iota(jnp.int32, (1, TQ), 1)
    q_chunk = q_pos // CHUNK

    @pl.when(step == 0)
    def _():
        d = (lax.broadcasted_iota(jnp.int32, (KT, TQ), 1) - lax.broadcasted_iota(jnp.int32, (KT, TQ), 0)).astype(F32)
        for h in range(B_HEADS):
            a2_ref[h] = (LOG2E * 2.0 ** (-8.0 * (h + 1) / B_HEADS)) * d

    @pl.when(kt == 0)
    def _():
        qst_ref[...] = (q_ref[...] * (HEAD_DIM ** -0.5 * LOG2E)).T.astype(BF16)
        qv = qidx_ref[...]
        for h in range(IDX_HEADS):
            qi_ref[h] = qv[:, h * IDX_DIM:(h + 1) * IDX_DIM].astype(BF16)
        wt_ref[...] = w_ref[...].T[IDX_DIM:IDX_DIM + IDX_HEADS, :]

        def scores(g):
            s0 = pl.multiple_of(g * SUB, SUB)
            kk = kidx_ref[pl.ds(s0, SUB), :IDX_DIM].astype(BF16)
            tot = jnp.zeros((SUB, TQ), F32)
            for h in range(IDX_HEADS):
                s = lax.dot_general(kk, qi_ref[h], (((1,), (1,)), ((), ())), preferred_element_type=F32)
                tot = tot + jnp.maximum(s, 0.0) * wt_ref[h:h + 1, :]
            return tot * ((IDX_DIM ** -0.5) * (IDX_HEADS ** -0.5))

        def past_tile(g, carry):
            mx, mn = carry
            tot = scores(g)
            sc_ref[g] = tot
            return (jnp.maximum(mx, jnp.max(tot, axis=0, keepdims=True)),
                    jnp.minimum(mn, jnp.min(tot, axis=0, keepdims=True)))

        mx, mn = lax.fori_loop(0, i, past_tile,
                               (jnp.full((1, TQ), -jnp.inf, F32), jnp.full((1, TQ), jnp.inf, F32)))
        tot = scores(i)
        key_chunk = (i * SUB + lax.broadcasted_iota(jnp.int32, (SUB, 1), 0)) // CHUNK
        adm = key_chunk <= q_chunk
        sc_ref[i] = jnp.where(adm, tot, -jnp.inf)
        mx = jnp.maximum(mx, jnp.max(jnp.where(adm, tot, -jnp.inf), axis=0, keepdims=True))
        mn = jnp.minimum(mn, jnp.min(jnp.where(adm, tot, jnp.inf), axis=0, keepdims=True))

        def fill_tile(g, c):
            sc_ref[g] = jnp.full((SUB, TQ), -jnp.inf, F32)
            return c

        lax.fori_loop(n_sub, (last_kt + 1) * (KT // SUB), fill_tile, 0)

        n_adm = ((q_chunk + 1) * CHUNK).astype(F32)
        keep_all = n_adm <= K
        lo_ref[...] = jnp.where(keep_all, -F32_MAX, mn)
        hi_ref[...] = mx
        clo_ref[...] = jnp.where(keep_all, K, n_adm)

        def count_ge(mid):
            def tile_count(g):
                ind = jnp.where(sc_ref[g] >= mid, 1.0, 0.0)
                return jnp.sum(ind.reshape(SUB // SUBLANES, SUBLANES, TQ), axis=0)

            def body(g2, acc):
                return acc + tile_count(2 * g2) + tile_count(2 * g2 + 1)

            acc = lax.fori_loop(0, n_sub // 2, body, jnp.zeros((SUBLANES, TQ), F32))
            odd = (n_sub % 2).astype(F32)
            acc = acc + odd * tile_count(n_sub - 1)
            return jnp.sum(acc, axis=0, keepdims=True)

        def n_open():
            return jnp.max(jnp.where(clo_ref[...] != K, 1.0, 0.0))

        def cond(c):
            it, open_rows = c
            return jnp.logical_and(it < 64, open_rows > 0.5)

        def body(c):
            it, _ = c
            lo, hi, clo = lo_ref[...], hi_ref[...], clo_ref[...]
            mid = lo + 0.5 * (hi - lo)
            c_mid = count_ge(mid)
            live = clo != K
            ge = jnp.logical_and(live, c_mid >= K)
            lt = jnp.logical_and(live, c_mid < K)
            lo_ref[...] = jnp.where(ge, mid, lo)
            clo_ref[...] = jnp.where(ge, c_mid, clo)
            hi_ref[...] = jnp.where(lt, mid, hi)
            return it + 1, n_open()

        lax.while_loop(cond, body, (jnp.int32(0), n_open()))
        thr_ref[...] = lo_ref[...]

        m_ref[...] = jnp.full(m_ref.shape, NEG_BIG, F32)
        l_ref[...] = jnp.zeros_like(l_ref)
        acc_ref[...] = jnp.zeros_like(acc_ref)

    def attend(last):
        nj = KT // SUB
        thr = thr_ref[...]
        for jj in range(nj):
            mb_ref[jj * SUB:(jj + 1) * SUB, :] = jnp.where(sc_ref[kt * nj + jj] >= thr, 0.0, NEG_BIG)
        gap = (i * TQ - kt * KT).astype(F32)
        m_all, l_all = m_ref[...], l_ref[...]
        m_rows, l_rows = [], []

        def qk(h):
            hs = slice(h * HEAD_DIM, (h + 1) * HEAD_DIM)
            for half in range(2):
                rows = slice(half * (KT // 2), (half + 1) * (KT // 2))
                lg_ref[h % 2, rows, :] = jnp.dot(k_ref[rows, hs], qst_ref[hs, :], preferred_element_type=F32)

        qk(0)
        for h in range(B_HEADS):
            if h + 1 < B_HEADS:
                qk(h + 1)
            hs = slice(h * HEAD_DIM, (h + 1) * HEAD_DIM)
            slope2 = LOG2E * 2.0 ** (-8.0 * (h + 1) / B_HEADS)
            if last:
                lg = (lg_ref[h % 2] - jnp.abs(a2_ref[h] + slope2 * gap)) + mb_ref[...]
                off = 0.0
            else:
                lg = (lg_ref[h % 2] - a2_ref[h]) + mb_ref[...]
                off = slope2 * gap
            lg_ref[h % 2] = lg
            m_old = m_all[h:h + 1, :]
            m_new = jnp.maximum(m_old, jnp.max(lg, axis=0, keepdims=True) - off)
            alpha = jnp.exp2(m_old - m_new)
            p = jnp.exp2(lg_ref[h % 2] - (m_new + off))
            l_rows.append(alpha * l_all[h:h + 1, :] + jnp.sum(p, axis=0, keepdims=True))
            m_rows.append(m_new)
            pb_ref[h % 2] = p.astype(BF16)
            pv = jnp.dot(vt_ref[0, hs, :], pb_ref[h % 2, :SUB, :], preferred_element_type=F32)
            for jj in range(1, nj):
                pv += jnp.dot(vt_ref[jj, hs, :], pb_ref[h % 2, jj * SUB:(jj + 1) * SUB, :],
                              preferred_element_type=F32)
            acc_ref[hs, :] = alpha * acc_ref[hs, :] + pv
        m_ref[...] = jnp.concatenate(m_rows, axis=0)
        l_ref[...] = jnp.concatenate(l_rows, axis=0)

    @pl.when(kt < last_kt)
    def _():
        attend(False)

    @pl.when(kt == last_kt)
    def _():
        attend(True)

    @pl.when(kt == last_kt)
    def _():
        for h in range(B_HEADS):
            hs = slice(h * HEAD_DIM, (h + 1) * HEAD_DIM)
            o_ref[:, hs] = (acc_ref[hs, :] / l_ref[h:h + 1, :]).T.astype(o_ref.dtype)


def _dsa_attention(proj, k, vt):
    TQ, KT = DSA_TQ, DSA_KT
    kw_block = (2 * D_A + IDX_HEADS * IDX_DIM + KV_RANK) // LANES
    pairs = [(i, kt) for i in range(SEQ // TQ) for kt in range((i * TQ + TQ - 1) // KT + 1)]
    qb = jnp.asarray([p[0] for p in pairs], jnp.int32)
    ktile = jnp.asarray([p[1] for p in pairs], jnp.int32)

    grid_spec = pltpu.PrefetchScalarGridSpec(
        num_scalar_prefetch=2,
        grid=(len(pairs),),
        in_specs=[
            pl.BlockSpec((TQ, D_B), lambda s, qb, kt: (qb[s], 1)),
            pl.BlockSpec((TQ, IDX_HEADS * IDX_DIM), lambda s, qb, kt: (qb[s], 2)),
            pl.BlockSpec((TQ, LANES), lambda s, qb, kt: (qb[s], kw_block)),
            pl.BlockSpec((SEQ, LANES), lambda s, qb, kt: (0, kw_block)),
            pl.BlockSpec((KT, D_B), lambda s, qb, kt: (kt[s], 0)),
            pl.BlockSpec((KT // DSA_SUB, D_B, DSA_SUB), lambda s, qb, kt: (kt[s], 0, 0)),
        ],
        out_specs=pl.BlockSpec((TQ, D_B), lambda s, qb, kt: (qb[s], 0)),
        scratch_shapes=[
            pltpu.VMEM((SEQ // DSA_SUB, DSA_SUB, TQ), F32),
            pltpu.VMEM((B_HEADS, KT, TQ), F32),
            pltpu.VMEM((KT, TQ), F32),
            pltpu.VMEM((2, KT, TQ), F32),
            pltpu.VMEM((2, KT, TQ), BF16),
            pltpu.VMEM((D_B, TQ), BF16),
            pltpu.VMEM((IDX_HEADS, TQ, IDX_DIM), BF16),
            pltpu.VMEM((IDX_HEADS, TQ), F32),
            pltpu.VMEM((1, TQ), F32),
            pltpu.VMEM((1, TQ), F32),
            pltpu.VMEM((1, TQ), F32),
            pltpu.VMEM((1, TQ), F32),
            pltpu.VMEM((B_HEADS, TQ), F32),
            pltpu.VMEM((B_HEADS, TQ), F32),
            pltpu.VMEM((D_B, TQ), F32),
        ],
    )
    return pl.pallas_call(
        _dsa_kernel,
        grid_spec=grid_spec,
        out_shape=jax.ShapeDtypeStruct((SEQ, D_B), BF16),
        compiler_params=_cparams(("arbitrary",)),
        name="dsa_attention",
    )(qb, ktile, proj, proj, proj, proj, k, vt)


def _ca_kernel(q_ref, k0_ref, k1_ref, k2_ref, v0_ref, v1_ref, v2_ref, relw_ref, o_ref, bias_ref, lg_ref, pb_ref):
    i = pl.program_id(1)
    k_refs = (k0_ref, k1_ref, k2_ref)
    v_refs = (v0_ref, v1_ref, v2_ref)
    width = CA_NKB * CA_TQ

    @pl.when(i == 0)
    def _():
        r_chunk = lax.broadcasted_iota(jnp.int32, (CA_TQ, width), 0) // CHUNK
        c_chunk = lax.broadcasted_iota(jnp.int32, (CA_TQ, width), 1) // CHUNK
        band = jnp.logical_and(c_chunk >= r_chunk, c_chunk <= r_chunk + C_LEFT_CHUNKS)
        for hh in range(CA_HB):
            row = jnp.broadcast_to(relw_ref[hh], (CA_TQ, CA_PERIOD))
            rolled = pltpu.roll(row, CA_PERIOD - (CA_TQ - 1), 1, stride=1, stride_axis=0)
            bias_ref[hh] = jnp.where(band, rolled[:, :width] * LOG2E, NEG_BIG)

    def qk(hh):
        hs = slice(hh * HEAD_DIM, (hh + 1) * HEAD_DIM)
        qh = q_ref[:, hs]
        for j in range(CA_NKB):
            cols = slice(j * CA_TQ, (j + 1) * CA_TQ)
            lg = lax.dot_general(qh, k_refs[j][:, hs], (((1,), (1,)), ((), ())), preferred_element_type=F32)
            lg = lg * (HEAD_DIM ** -0.5 * LOG2E) + bias_ref[hh, :, cols]
            lg_ref[hh % 2, :, cols] = jnp.where(i + j >= CA_NKB - 1, lg, NEG_BIG)

    qk(0)
    for hh in range(CA_HB):
        if hh + 1 < CA_HB:
            qk(hh + 1)
        hs = slice(hh * HEAD_DIM, (hh + 1) * HEAD_DIM)
        lg = lg_ref[hh % 2]
        m = jnp.max(lg, axis=-1, keepdims=True)
        p = jnp.exp2(lg - m)
        l = jnp.sum(p, axis=-1, keepdims=True)
        pb_ref[hh % 2] = p.astype(BF16)
        acc = jnp.dot(pb_ref[hh % 2, :, :CA_TQ], v_refs[0][:, hs], preferred_element_type=F32)
        for j in range(1, CA_NKB):
            acc += jnp.dot(pb_ref[hh % 2, :, j * CA_TQ:(j + 1) * CA_TQ], v_refs[j][:, hs],
                           preferred_element_type=F32)
        o_ref[:, hs] = (acc / l).astype(o_ref.dtype)


def _ca_rel_row(rel_bias):
    n_heads = rel_bias.shape[0]
    span = CA_NKB * CA_TQ + CA_TQ - 1
    v = jnp.concatenate([rel_bias.astype(F32)[:, MAX_REL - CA_TQ + 1:],
                         jnp.broadcast_to(rel_bias.astype(F32)[:, -1:], (n_heads, span - MAX_REL - CA_TQ))], axis=1)
    return jnp.pad(v[:, ::-1], ((0, 0), (0, CA_PERIOD - span))).reshape(n_heads, 1, CA_PERIOD)


def _chunk_attention(qkv, rel_row):
    nt = SEQ // CA_TQ
    hw = CA_HB * HEAD_DIM
    nhb = C_HEADS // CA_HB

    def kmap(j, base):
        return lambda hb, i: (jnp.maximum(i - (CA_NKB - 1) + j, 0), base + hb)

    in_specs = [pl.BlockSpec((CA_TQ, hw), lambda hb, i: (i, hb))]
    in_specs += [pl.BlockSpec((CA_TQ, hw), kmap(j, nhb)) for j in range(CA_NKB)]
    in_specs += [pl.BlockSpec((CA_TQ, hw), kmap(j, 2 * nhb)) for j in range(CA_NKB)]
    in_specs += [pl.BlockSpec((CA_HB, 1, CA_PERIOD), lambda hb, i: (hb, 0, 0))]
    return pl.pallas_call(
        _ca_kernel,
        grid=(nhb, nt),
        in_specs=in_specs,
        out_specs=pl.BlockSpec((CA_TQ, hw), lambda hb, i: (i, hb)),
        out_shape=jax.ShapeDtypeStruct((SEQ, D_MODEL), BF16),
        scratch_shapes=[pltpu.VMEM((CA_HB, CA_TQ, CA_NKB * CA_TQ), F32),
                        pltpu.VMEM((2, CA_TQ, CA_NKB * CA_TQ), F32),
                        pltpu.VMEM((2, CA_TQ, CA_NKB * CA_TQ), BF16)],
        compiler_params=_cparams(("arbitrary", "arbitrary")),
        name="chunk_attention",
    )(qkv, qkv, qkv, qkv, qkv, qkv, qkv, rel_row)


def _pad_cols(w, n):
    return jnp.pad(w, ((0, 0), (0, n - w.shape[1])))


def kernel(x, c, ada_w, ada_b, norm_g, ffn_w_gate, ffn_w_up, ffn_w_down, ab_w_in, s5_lam_re, s5_lam_im, s5_log_dt, s5_b_re, s5_b_im, s5_c_re, s5_c_im, s5_d, s5_w_glu, s5_b_glu, dsa_kv_norm_g, dsa_w_kv_up, ab_w_out, c_w_qkv, c_rel_bias, c_w_out, final_norm_g):
    mod = _modulation(c, ada_w, ada_b)
    h = x.reshape(SEQ, D_MODEL)
    wg, wu, wd = ffn_w_gate, ffn_w_up, ffn_w_down

    for layer in range(DEPTH):
        h = _ffn(h, norm_g[layer, 0], mod[layer, 0], wg, wu, wd, layer, 0)
        if layer % 2 == 0:
            e = layer // 2
            w_in = ab_w_in[e]
            o_q, o_kv, o_qi = D_A, D_A + D_B, D_A + D_B + KV_RANK
            o_ki = o_qi + IDX_HEADS * IDX_DIM
            w_in = jnp.concatenate([w_in[:, :o_kv], w_in[:, o_qi:o_ki], w_in[:, o_kv:o_qi], w_in[:, o_ki:]], axis=1)
            proj = _norm_proj(h, norm_g[layer, 1], mod[layer, 1], _pad_cols(w_in, AB_N_PAD), F32, PROJ_TN)

            a_re, a_im, bb_re, bb_im = _s5_params(s5_lam_re[e], s5_lam_im[e], s5_log_dt[e], s5_b_re[e], s5_b_im[e])
            n_blk, ts = SEQ // S5_T, S5_T // S5_NSEG
            u_il = (proj[:, :D_A].reshape(n_blk, S5_NSEG, ts, D_A).transpose(0, 2, 1, 3).reshape(SEQ, D_A))
            y_a = _s5_mixer(u_il, a_re, a_im, bb_re, bb_im, s5_c_re[e], s5_c_im[e], s5_d[e],
                            s5_w_glu[e].astype(BF16), s5_b_glu[e])
            y_a = y_a.reshape(n_blk, ts, S5_NSEG, D_A).transpose(0, 2, 1, 3).reshape(SEQ, D_A)

            k, vt = _kv_up(proj, dsa_kv_norm_g[e], dsa_w_kv_up[e].astype(BF16))
            y_b = _dsa_attention(proj, k, vt)

            w_out = ab_w_out[e].astype(BF16)
            h = _out_proj([y_a, y_b], [w_out[:D_A], w_out[D_A:]], h, mod[layer, 1])
        else:
            o = layer // 2
            qkv = _norm_proj(h, norm_g[layer, 1], mod[layer, 1], c_w_qkv[o], BF16, QKV_TN)
            att = _chunk_attention(qkv, _ca_rel_row(c_rel_bias[o]))
            h = _out_proj([att], [c_w_out[o].astype(BF16)], h, mod[layer, 1])
        h = _ffn(h, norm_g[layer, 2], mod[layer, 2], wg, wu, wd, layer, 1,
                 final_g=final_norm_g if layer == DEPTH - 1 else None)
    return h.reshape(1, SEQ, D_MODEL)
```

```python
import functools
import math

import jax
import jax.numpy as jnp
import numpy as np
from jax import lax
from jax.experimental import pallas as pl
from jax.experimental.pallas import tpu as pltpu

F32 = jnp.float32
BF16 = jnp.bfloat16

D_MODEL = 2048
SEQ = 8192
DEPTH = 2
CHUNK = 64
HEAD_DIM = 128
D_FF = 5504
N_SUB = 3
EPS = 1e-6
D_A = D_MODEL // 2
S5_GROUP = 16
S5_GROUPS = D_A // S5_GROUP
S5_STATE = 64
D_B = D_MODEL // 2
B_HEADS = D_B // HEAD_DIM
KV_RANK = D_MODEL // 8
IDX_HEADS = 16
IDX_DIM = 64
TOPK = 256
C_HEADS = D_MODEL // HEAD_DIM
C_LEFT_CHUNKS = 8
MAX_REL = 256

LANES = 128
SUBLANES = 8
VMEM_LIMIT = 56 * 1024 * 1024
NEG_BIG = -1e30
LOG2E = math.log2(math.e)
F32_MAX = float(np.finfo(np.float32).max)

FFN_TM = 1024
FFN_TF = 256
FFN_VMEM_LIMIT = 60 * 1024 * 1024
PROJ_TM = 1024
PROJ_TN = 512
QKV_TN = 1024
OUT_TM = 512
AB_N_PAD = 3584
S5_T = 256
S5_NSEG = SUBLANES
S5_SLICES = D_A // LANES
S5_NSTATE = S5_GROUPS * S5_STATE
DSA_TQ = 256
DSA_KT = 1024
DSA_SUB = 256
CA_TQ = 256
CA_HB = 8
CA_NKB = 3
CA_PERIOD = 1024


def _cparams(sem):
    return pltpu.CompilerParams(dimension_semantics=sem, vmem_limit_bytes=VMEM_LIMIT)


def _mod_kernel(c_ref, w_ref, b_ref, o_ref):
    rows = 256
    tn = o_ref.shape[-1]

    def body(i, acc):
        r0 = pl.multiple_of(i * rows, rows)
        cc = c_ref[pl.ds(r0, rows), :]
        cc = cc * jax.nn.sigmoid(cc)
        w = w_ref[0, pl.ds(r0, rows), :]
        return acc + jnp.sum((w * cc).reshape(rows // SUBLANES, SUBLANES, tn), axis=0)

    acc = lax.fori_loop(0, D_MODEL // rows, body, jnp.zeros((SUBLANES, tn), F32))
    o_ref[0] = jnp.sum(acc, axis=0, keepdims=True) + b_ref[0]


def _modulation(c, ada_w, ada_b):
    n = N_SUB * 3 * D_MODEL
    tn = 1024
    c_col = c.reshape(D_MODEL, 1)
    out = pl.pallas_call(
        _mod_kernel,
        grid=(DEPTH, n // tn),
        in_specs=[
            pl.BlockSpec((D_MODEL, 1), lambda l, j: (0, 0)),
            pl.BlockSpec((1, D_MODEL, tn), lambda l, j: (l, 0, j)),
            pl.BlockSpec((1, 1, tn), lambda l, j: (l, 0, j)),
        ],
        out_specs=pl.BlockSpec((1, 1, tn), lambda l, j: (l, 0, j)),
        out_shape=jax.ShapeDtypeStruct((DEPTH, 1, n), F32),
        compiler_params=_cparams(("arbitrary", "arbitrary")),
        name="adaln_mod",
    )(c_col, ada_w, ada_b.reshape(DEPTH, 1, n))
    return out.reshape(DEPTH, N_SUB, 3, D_MODEL)


ADALN_ROWS = 32


def _adaln_to(hn_ref, h_ref, g_ref, mod_ref):
    g = g_ref[...]
    scale1 = 1.0 + mod_ref[1:2, :]
    shift = mod_ref[0:1, :]

    def body(r, c):
        rows = pl.ds(pl.multiple_of(r * ADALN_ROWS, ADALN_ROWS), ADALN_ROWS)
        x = h_ref[rows, :]
        ms = jnp.mean(x * x, axis=-1, keepdims=True)
        y = (x * lax.rsqrt(ms + EPS)) * g
        hn_ref[rows, :] = (y * scale1 + shift).astype(hn_ref.dtype)
        return c

    lax.fori_loop(0, h_ref.shape[0] // ADALN_ROWS, body, 0, unroll=4)


def _ffn_kernel(h_ref, g_ref, mod_ref, wg_ref, wu_ref, wd_ref, *rest, n_f, final):
    if final:
        fg_ref, o_ref, hn_ref = rest
    else:
        o_ref, hn_ref = rest
    f = pl.program_id(1)

    @pl.when(f == 0)
    def _():
        _adaln_to(hn_ref, h_ref, g_ref, mod_ref)
        o_ref[...] = jnp.zeros_like(o_ref)

    def accumulate(width):
        hn = hn_ref[...]
        gate = jnp.dot(hn, wg_ref[:, :width].astype(BF16), preferred_element_type=F32)
        up = jnp.dot(hn, wu_ref[:, :width].astype(BF16), preferred_element_type=F32)
        act = (gate * jax.nn.sigmoid(gate)) * up
        o_ref[...] += jnp.dot(act.astype(BF16), wd_ref[:width, :].astype(BF16), preferred_element_type=F32)

    @pl.when(f < n_f - 1)
    def _():
        accumulate(FFN_TF)

    @pl.when(f == n_f - 1)
    def _():
        accumulate(D_FF - (n_f - 1) * FFN_TF)
        half_gate = 0.5 * mod_ref[2:3, :]

        def finish(r, c):
            rows = pl.ds(pl.multiple_of(r * ADALN_ROWS, ADALN_ROWS), ADALN_ROWS)
            out = h_ref[rows, :] + half_gate * o_ref[rows, :]
            if final:
                ms = jnp.mean(out * out, axis=-1, keepdims=True)
                out = (out * lax.rsqrt(ms + EPS)) * fg_ref[...]
            o_ref[rows, :] = out
            return c

        lax.fori_loop(0, FFN_TM // ADALN_ROWS, finish, 0, unroll=4)


def _ffn(h, g, mod, wg, wu, wd, layer, which, final_g=None):
    n_f = pl.cdiv(D_FF, FFN_TF)
    final = final_g is not None
    in_specs = [
        pl.BlockSpec((FFN_TM, D_MODEL), lambda i, f: (i, 0)),
        pl.BlockSpec((1, D_MODEL), lambda i, f: (0, 0)),
        pl.BlockSpec((3, D_MODEL), lambda i, f: (0, 0)),
        pl.BlockSpec((None, None, D_MODEL, FFN_TF), lambda i, f: (layer, which, 0, f)),
        pl.BlockSpec((None, None, D_MODEL, FFN_TF), lambda i, f: (layer, which, 0, f)),
        pl.BlockSpec((None, None, FFN_TF, D_MODEL), lambda i, f: (layer, which, f, 0)),
    ]
    args = [h, g.reshape(1, D_MODEL), mod, wg, wu, wd]
    if final:
        in_specs.append(pl.BlockSpec((1, D_MODEL), lambda i, f: (0, 0)))
        args.append(final_g.reshape(1, D_MODEL))
    return pl.pallas_call(
        functools.partial(_ffn_kernel, n_f=n_f, final=final),
        grid=(SEQ // FFN_TM, n_f),
        in_specs=in_specs,
        out_specs=pl.BlockSpec((FFN_TM, D_MODEL), lambda i, f: (i, 0)),
        out_shape=jax.ShapeDtypeStruct((SEQ, D_MODEL), F32),
        scratch_shapes=[pltpu.VMEM((FFN_TM, D_MODEL), BF16)],
        compiler_params=pltpu.CompilerParams(dimension_semantics=("arbitrary", "arbitrary"),
                                             vmem_limit_bytes=FFN_VMEM_LIMIT),
        name="ffn_swiglu",
    )(*args)


def _proj_kernel(h_ref, g_ref, mod_ref, w_ref, o_ref, hn_ref):
    @pl.when(pl.program_id(1) == 0)
    def _():
        _adaln_to(hn_ref, h_ref, g_ref, mod_ref)

    o_ref[...] = jnp.dot(hn_ref[...], w_ref[...].astype(BF16), preferred_element_type=F32).astype(o_ref.dtype)


def _norm_proj(h, g, mod, w, out_dtype, tn):
    n = w.shape[1]
    return pl.pallas_call(
        _proj_kernel,
        grid=(SEQ // PROJ_TM, n // tn),
        in_specs=[
            pl.BlockSpec((PROJ_TM, D_MODEL), lambda i, j: (i, 0)),
            pl.BlockSpec((1, D_MODEL), lambda i, j: (0, 0)),
            pl.BlockSpec((3, D_MODEL), lambda i, j: (0, 0)),
            pl.BlockSpec((D_MODEL, tn), lambda i, j: (0, j)),
        ],
        out_specs=pl.BlockSpec((PROJ_TM, tn), lambda i, j: (i, j)),
        out_shape=jax.ShapeDtypeStruct((SEQ, n), out_dtype),
        scratch_shapes=[pltpu.VMEM((PROJ_TM, D_MODEL), BF16)],
        compiler_params=_cparams(("arbitrary", "arbitrary")),
        name="adaln_proj",
    )(h, g.reshape(1, D_MODEL), mod, w)


def _out_kernel(*refs, n_lhs):
    lhs = refs[:n_lhs]
    ws = refs[n_lhs:2 * n_lhs]
    h_ref, mod_ref, o_ref = refs[2 * n_lhs:]
    y = jnp.dot(lhs[0][...], ws[0][...], preferred_element_type=F32)
    for a_ref, w_ref in zip(lhs[1:], ws[1:]):
        y += jnp.dot(a_ref[...], w_ref[...], preferred_element_type=F32)
    o_ref[...] = h_ref[...] + mod_ref[2:3, :] * y


def _out_proj(lhs, ws, h, mod):
    n_lhs = len(lhs)
    in_specs = [pl.BlockSpec((OUT_TM, a.shape[1]), lambda i: (i, 0)) for a in lhs]
    in_specs += [pl.BlockSpec(w.shape, lambda i: (0, 0)) for w in ws]
    in_specs += [pl.BlockSpec((OUT_TM, D_MODEL), lambda i: (i, 0)), pl.BlockSpec((3, D_MODEL), lambda i: (0, 0))]
    return pl.pallas_call(
        functools.partial(_out_kernel, n_lhs=n_lhs),
        grid=(SEQ // OUT_TM,),
        in_specs=in_specs,
        out_specs=pl.BlockSpec((OUT_TM, D_MODEL), lambda i: (i, 0)),
        out_shape=jax.ShapeDtypeStruct((SEQ, D_MODEL), F32),
        compiler_params=_cparams(("arbitrary",)),
        name="out_proj_residual",
    )(*lhs, *ws, h, mod)


def _s5_param_kernel(lr_ref, li_ref, ldt_ref, br_ref, bi_ref, are_ref, aim_ref, bbr_ref, bbi_ref):
    lr = lr_ref[...]
    li = li_ref[...]
    dt = jnp.exp(ldt_ref[...])
    mag = jnp.exp(lr * dt)
    ab_re = mag * jnp.cos(li * dt)
    ab_im = mag * jnp.sin(li * dt)
    den = lr * lr + li * li
    nr = ab_re - 1.0
    f_re = (nr * lr + ab_im * li) / den
    f_im = (ab_im * lr - nr * li) / den
    are_ref[...] = ab_re
    aim_ref[...] = ab_im
    br = br_ref[...]
    bi = bi_ref[...]
    bbr_ref[...] = f_re * br - f_im * bi
    bbi_ref[...] = f_re * bi + f_im * br


def _s5_params(lam_re, lam_im, log_dt, b_re, b_im):
    G, P, CG = S5_GROUPS, S5_STATE, S5_GROUP
    return pl.pallas_call(
        _s5_param_kernel,
        out_shape=[jax.ShapeDtypeStruct((G, 1, P), F32), jax.ShapeDtypeStruct((G, 1, P), F32),
                   jax.ShapeDtypeStruct((G, CG, P), F32), jax.ShapeDtypeStruct((G, CG, P), F32)],
        name="s5_zoh_params",
    )(lam_re.reshape(G, 1, P), lam_im.reshape(G, 1, P), log_dt.reshape(G, 1, 1),
      jnp.swapaxes(b_re, 1, 2), jnp.swapaxes(b_im, 1, 2))


def _block_diag_slices(m):
    _, r, c = m.shape
    m4 = m.reshape(S5_SLICES, SUBLANES, r, c)
    eye = jnp.eye(SUBLANES, dtype=m.dtype)
    out = m4[:, :, :, None, :] * eye[None, :, None, :, None]
    return out.reshape(S5_SLICES, SUBLANES * r, SUBLANES * c)


def _s5_kernel(u_ref, bbr_ref, bbi_ref, are_ref, aim_ref, ccr_ref, cci_ref, d_ref, wglu_ref, bglu_ref,
               o_ref, xr_ref, xi_ref, pr_ref, pi_ref, er_ref, ei_ref, cr_ref, ci_ref):
    W = S5_NSTATE // S5_SLICES
    TS = S5_T // S5_NSEG
    CW = 512

    @pl.when(pl.program_id(0) == 0)
    def _():
        pr_ref[0:1, :] = are_ref[...]
        pi_ref[0:1, :] = aim_ref[...]

        def power(s, c):
            ar, ai = are_ref[...], aim_ref[...]
            qr, qi = pr_ref[pl.ds(s - 1, 1), :], pi_ref[pl.ds(s - 1, 1), :]
            pr_ref[pl.ds(s, 1), :] = ar * qr - ai * qi
            pi_ref[pl.ds(s, 1), :] = ar * qi + ai * qr
            return c

        lax.fori_loop(1, TS, power, 0)
        cr_ref[...] = jnp.zeros_like(cr_ref)
        ci_ref[...] = jnp.zeros_like(ci_ref)

    for k in range(S5_SLICES):
        uk = u_ref[:, k * LANES:(k + 1) * LANES].astype(BF16)
        xr_ref[:, k * W:(k + 1) * W] = jnp.dot(uk, bbr_ref[k], preferred_element_type=F32)
        xi_ref[:, k * W:(k + 1) * W] = jnp.dot(uk, bbi_ref[k], preferred_element_type=F32)

    for cg in range(S5_NSTATE // CW):
        cols = slice(cg * CW, (cg + 1) * CW)
        ar = jnp.broadcast_to(are_ref[:, cols], (S5_NSEG, CW))
        ai = jnp.broadcast_to(aim_ref[:, cols], (S5_NSEG, CW))

        def step(s, carry, cols=cols, ar=ar, ai=ai):
            sr, si = carry
            rows = pl.ds(pl.multiple_of(s * S5_NSEG, S5_NSEG), S5_NSEG)
            nr = ar * sr - ai * si + xr_ref[rows, cols]
            ni = ar * si + ai * sr + xi_ref[rows, cols]
            xr_ref[rows, cols] = nr
            xi_ref[rows, cols] = ni
            return nr, ni

        zero = jnp.zeros((S5_NSEG, CW), F32)
        er, ei = lax.fori_loop(0, TS, step, (zero, zero), unroll=4)
        er_ref[:, cols] = er
        ei_ref[:, cols] = ei

    pwr, pwi = pr_ref[TS - 1:TS, :], pi_ref[TS - 1:TS, :]
    c_r, c_i = cr_ref[S5_NSEG:S5_NSEG + 1, :], ci_ref[S5_NSEG:S5_NSEG + 1, :]
    for j in range(S5_NSEG):
        cr_ref[j:j + 1, :] = c_r
        ci_ref[j:j + 1, :] = c_i
        c_r, c_i = (er_ref[j:j + 1, :] + pwr * c_r - pwi * c_i,
                    ei_ref[j:j + 1, :] + pwr * c_i + pwi * c_r)
    cr_ref[S5_NSEG:S5_NSEG + 1, :] = c_r
    ci_ref[S5_NSEG:S5_NSEG + 1, :] = c_i

    for cg in range(S5_NSTATE // CW):
        cols = slice(cg * CW, (cg + 1) * CW)
        c_r, c_i = cr_ref[0:S5_NSEG, cols], ci_ref[0:S5_NSEG, cols]

        def fix(s, carry, cols=cols, c_r=c_r, c_i=c_i):
            rows = pl.ds(pl.multiple_of(s * S5_NSEG, S5_NSEG), S5_NSEG)
            p_r, p_i = pr_ref[pl.ds(s, 1), cols], pi_ref[pl.ds(s, 1), cols]
            xr_ref[rows, cols] = xr_ref[rows, cols] + (p_r * c_r - p_i * c_i)
            xi_ref[rows, cols] = xi_ref[rows, cols] + (p_r * c_i + p_i * c_r)
            return carry

        lax.fori_loop(0, TS, fix, 0, unroll=4)

    ys = []
    for k in range(S5_SLICES):
        xr = xr_ref[:, k * W:(k + 1) * W].astype(BF16)
        xi = xi_ref[:, k * W:(k + 1) * W].astype(BF16)
        ys.append(jnp.dot(xr, ccr_ref[k], preferred_element_type=F32)
                  + jnp.dot(xi, cci_ref[k], preferred_element_type=F32))
    y = jnp.concatenate(ys, axis=-1) + d_ref[...] * u_ref[...]
    y = jax.nn.gelu(y, approximate=True)
    z = jnp.dot(y.astype(BF16), wglu_ref[...], preferred_element_type=F32) + bglu_ref[...]
    o_ref[...] = (y * jax.nn.sigmoid(z)).astype(o_ref.dtype)


def _s5_mixer(proj, a_re, a_im, bb_re, bb_im, c_re, c_im, d_skip, w_glu, b_glu):
    W = S5_NSTATE // S5_SLICES
    bbr = _block_diag_slices(bb_re).astype(BF16)
    bbi = _block_diag_slices(bb_im).astype(BF16)
    ccr = _block_diag_slices(jnp.swapaxes(c_re, 1, 2)).astype(BF16)
    cci = _block_diag_slices(-jnp.swapaxes(c_im, 1, 2)).astype(BF16)
    const3 = lambda t: (0, 0, 0)
    const2 = lambda t: (0, 0)
    return pl.pallas_call(
        _s5_kernel,
        grid=(SEQ // S5_T,),
        in_specs=[
            pl.BlockSpec((S5_T, D_A), lambda t: (t, 0)),
            pl.BlockSpec((S5_SLICES, LANES, W), const3),
            pl.BlockSpec((S5_SLICES, LANES, W), const3),
            pl.BlockSpec((1, S5_NSTATE), const2),
            pl.BlockSpec((1, S5_NSTATE), const2),
            pl.BlockSpec((S5_SLICES, W, LANES), const3),
            pl.BlockSpec((S5_SLICES, W, LANES), const3),
            pl.BlockSpec((1, D_A), const2),
            pl.BlockSpec((D_A, D_A), const2),
            pl.BlockSpec((1, D_A), const2),
        ],
        out_specs=pl.BlockSpec((S5_T, D_A), lambda t: (t, 0)),
        out_shape=jax.ShapeDtypeStruct((SEQ, D_A), BF16),
        scratch_shapes=[pltpu.VMEM((S5_T, S5_NSTATE), F32), pltpu.VMEM((S5_T, S5_NSTATE), F32),
                        pltpu.VMEM((S5_T // S5_NSEG, S5_NSTATE), F32), pltpu.VMEM((S5_T // S5_NSEG, S5_NSTATE), F32),
                        pltpu.VMEM((S5_NSEG, S5_NSTATE), F32), pltpu.VMEM((S5_NSEG, S5_NSTATE), F32),
                        pltpu.VMEM((2 * S5_NSEG, S5_NSTATE), F32), pltpu.VMEM((2 * S5_NSEG, S5_NSTATE), F32)],
        compiler_params=_cparams(("arbitrary",)),
        name="s5_mixer",
    )(proj, bbr, bbi, a_re.reshape(1, S5_NSTATE), a_im.reshape(1, S5_NSTATE), ccr, cci,
      d_skip.reshape(1, D_A), w_glu, b_glu.reshape(1, D_A))


def _kv_kernel(lat_ref, g_ref, wk_ref, wvt_ref, k_ref, vt_ref):
    x = lat_ref[...]
    ms = jnp.mean(x * x, axis=-1, keepdims=True)
    xn = ((x * lax.rsqrt(ms + EPS)) * g_ref[...]).astype(BF16)
    k_ref[...] = jnp.dot(xn, wk_ref[...], preferred_element_type=F32).astype(k_ref.dtype)
    vt = lax.dot_general(wvt_ref[...], xn, (((1,), (1,)), ((), ())), preferred_element_type=F32)
    vt_ref[0] = vt.astype(vt_ref.dtype)


def _kv_up(proj, g, w):
    tm = DSA_KT
    lat_block = (2 * D_A + IDX_HEADS * IDX_DIM) // KV_RANK
    wk = w[:, :D_B]
    wvt = w[:, D_B:].T
    return pl.pallas_call(
        _kv_kernel,
        grid=(SEQ // tm,),
        in_specs=[
            pl.BlockSpec((tm, KV_RANK), lambda i: (i, lat_block)),
            pl.BlockSpec((1, KV_RANK), lambda i: (0, 0)),
            pl.BlockSpec((KV_RANK, D_B), lambda i: (0, 0)),
            pl.BlockSpec((D_B, KV_RANK), lambda i: (0, 0)),
        ],
        out_specs=[pl.BlockSpec((tm, D_B), lambda i: (i, 0)),
                   pl.BlockSpec((1, D_B, tm), lambda i: (i, 0, 0))],
        out_shape=[jax.ShapeDtypeStruct((SEQ, D_B), BF16),
                   jax.ShapeDtypeStruct((SEQ // tm, D_B, tm), BF16)],
        compiler_params=_cparams(("arbitrary",)),
        name="dsa_kv_up",
    )(proj, g.reshape(1, KV_RANK), wk, wvt)


def _dsa_kernel(qb_ref, kt_ref, q_ref, qidx_ref, w_ref, kidx_ref, k_ref, vt_ref, o_ref,
                sc_ref, a2_ref, mb_ref, lg_ref, pb_ref, qst_ref, qi_ref, wt_ref, thr_ref, lo_ref, hi_ref, clo_ref,
                m_ref, l_ref, acc_ref):
    TQ, KT, SUB = DSA_TQ, DSA_KT, DSA_SUB
    K = float(TOPK)
    step = pl.program_id(0)
    i = qb_ref[step]
    kt = kt_ref[step]
    last_kt = (i * TQ + TQ - 1) // KT
    n_sub = i + 1
    q_pos = i * TQ + lax.broadcasted_iota(jnp.int32, (1, TQ), 1)
    q_chunk = q_pos // CHUNK

    @pl.when(step == 0)
    def _():
        d = (lax.broadcasted_iota(jnp.int32, (KT, TQ), 1) - lax.broadcasted_iota(jnp.int32, (KT, TQ), 0)).astype(F32)
        for h in range(B_HEADS):
            a2_ref[h] = (LOG2E * 2.0 ** (-8.0 * (h + 1) / B_HEADS)) * d

    @pl.when(kt == 0)
    def _():
        qst_ref[...] = (q_ref[...] * (HEAD_DIM ** -0.5 * LOG2E)).T.astype(BF16)
        qv = qidx_ref[...]
        for h in range(IDX_HEADS):
            qi_ref[h] = qv[:, h * IDX_DIM:(h + 1) * IDX_DIM].astype(BF16)
        wt_ref[...] = w_ref[...].T[IDX_DIM:IDX_DIM + IDX_HEADS, :]

        def scores(g):
            s0 = pl.multiple_of(g * SUB, SUB)
            kk = kidx_ref[pl.ds(s0, SUB), :IDX_DIM].astype(BF16)
            tot = jnp.zeros((SUB, TQ), F32)
            for h in range(IDX_HEADS):
                s = lax.dot_general(kk, qi_ref[h], (((1,), (1,)), ((), ())), preferred_element_type=F32)
                tot = tot + jnp.maximum(s, 0.0) * wt_ref[h:h + 1, :]
            return tot * ((IDX_DIM ** -0.5) * (IDX_HEADS ** -0.5))

        def past_tile(g, carry):
            mx, mn = carry
            tot = scores(g)
            sc_ref[g] = tot
            return (jnp.maximum(mx, jnp.max(tot, axis=0, keepdims=True)),
                    jnp.minimum(mn, jnp.min(tot, axis=0, keepdims=True)))

        def past_pair(g2, carry):
            return past_tile(2 * g2 + 1, past_tile(2 * g2, carry))

        carry = lax.fori_loop(0, i // 2, past_pair,
                              (jnp.full((1, TQ), -jnp.inf, F32), jnp.full((1, TQ), jnp.inf, F32)))
        mx, mn = lax.cond(i % 2 == 1, lambda c: past_tile(i - 1, c), lambda c: c, carry)
        tot = scores(i)
        key_chunk = (i * SUB + lax.broadcasted_iota(jnp.int32, (SUB, 1), 0)) // CHUNK
        adm = key_chunk <= q_chunk
        sc_ref[i] = jnp.where(adm, tot, -jnp.inf)
        mx = jnp.maximum(mx, jnp.max(jnp.where(adm, tot, -jnp.inf), axis=0, keepdims=True))
        mn = jnp.minimum(mn, jnp.min(jnp.where(adm, tot, jnp.inf), axis=0, keepdims=True))

        def fill_tile(g, c):
            sc_ref[g] = jnp.full((SUB, TQ), -jnp.inf, F32)
            return c

        lax.fori_loop(n_sub, (last_kt + 1) * (KT // SUB), fill_tile, 0)

        n_adm = ((q_chunk + 1) * CHUNK).astype(F32)
        keep_all = n_adm <= K
        lo_ref[...] = jnp.where(keep_all, -F32_MAX, mn)
        hi_ref[...] = mx
        clo_ref[...] = jnp.where(keep_all, K, n_adm)

        def count_ge(mid):
            def tile_count(g):
                ind = jnp.where(sc_ref[g] >= mid, 1.0, 0.0)
                return jnp.sum(ind.reshape(SUB // SUBLANES, SUBLANES, TQ), axis=0)

            def body(g2, acc):
                return acc + tile_count(2 * g2) + tile_count(2 * g2 + 1)

            acc = lax.fori_loop(0, n_sub // 2, body, jnp.zeros((SUBLANES, TQ), F32))
            odd = (n_sub % 2).astype(F32)
            acc = acc + odd * tile_count(n_sub - 1)
            return jnp.sum(acc, axis=0, keepdims=True)

        def n_open():
            return jnp.max(jnp.where(clo_ref[...] != K, 1.0, 0.0))

        def cond(c):
            it, open_rows = c
            return jnp.logical_and(it < 64, open_rows > 0.5)

        def body(c):
            it, _ = c
            lo, hi, clo = lo_ref[...], hi_ref[...], clo_ref[...]
            mid = lo + 0.5 * (hi - lo)
            c_mid = count_ge(mid)
            live = clo != K
            ge = jnp.logical_and(live, c_mid >= K)
            lt = jnp.logical_and(live, c_mid < K)
            lo_ref[...] = jnp.where(ge, mid, lo)
            clo_ref[...] = jnp.where(ge, c_mid, clo)
            hi_ref[...] = jnp.where(lt, mid, hi)
            return it + 1, n_open()

        lax.while_loop(cond, body, (jnp.int32(0), n_open()))
        thr_ref[...] = lo_ref[...]

        m_ref[...] = jnp.full(m_ref.shape, NEG_BIG, F32)
        l_ref[...] = jnp.zeros_like(l_ref)
        acc_ref[...] = jnp.zeros_like(acc_ref)

    def attend(last):
        nj = KT // SUB
        thr = thr_ref[...]
        for jj in range(nj):
            mb_ref[jj * SUB:(jj + 1) * SUB, :] = jnp.where(sc_ref[kt * nj + jj] >= thr, 0.0, NEG_BIG)
        gap = (i * TQ - kt * KT).astype(F32)
        m_all, l_all = m_ref[...], l_ref[...]
        m_rows, l_rows = [], []

        def qk(h):
            hs = slice(h * HEAD_DIM, (h + 1) * HEAD_DIM)
            for half in range(2):
                rows = slice(half * (KT // 2), (half + 1) * (KT // 2))
                lg_ref[h % 2, rows, :] = jnp.dot(k_ref[rows, hs], qst_ref[hs, :], preferred_element_type=F32)

        qk(0)
        for h in range(B_HEADS):
            if h + 1 < B_HEADS:
                qk(h + 1)
            hs = slice(h * HEAD_DIM, (h + 1) * HEAD_DIM)
            slope2 = LOG2E * 2.0 ** (-8.0 * (h + 1) / B_HEADS)
            if last:
                lg = (lg_ref[h % 2] - jnp.abs(a2_ref[h] + slope2 * gap)) + mb_ref[...]
                off = 0.0
            else:
                lg = (lg_ref[h % 2] - a2_ref[h]) + mb_ref[...]
                off = slope2 * gap
            lg_ref[h % 2] = lg
            m_old = m_all[h:h + 1, :]
            m_new = jnp.maximum(m_old, jnp.max(lg, axis=0, keepdims=True) - off)
            alpha = jnp.exp2(m_old - m_new)
            p = jnp.exp2(lg_ref[h % 2] - (m_new + off))
            l_rows.append(alpha * l_all[h:h + 1, :] + jnp.sum(p, axis=0, keepdims=True))
            m_rows.append(m_new)
            pb_ref[h % 2] = p.astype(BF16)
            pv = jnp.dot(vt_ref[0, hs, :], pb_ref[h % 2], preferred_element_type=F32)
            acc_ref[hs, :] = alpha * acc_ref[hs, :] + pv
        m_ref[...] = jnp.concatenate(m_rows, axis=0)
        l_ref[...] = jnp.concatenate(l_rows, axis=0)

    @pl.when(kt < last_kt)
    def _():
        attend(False)

    @pl.when(kt == last_kt)
    def _():
        attend(True)

    @pl.when(kt == last_kt)
    def _():
        for h in range(B_HEADS):
            hs = slice(h * HEAD_DIM, (h + 1) * HEAD_DIM)
            o_ref[:, hs] = (acc_ref[hs, :] / l_ref[h:h + 1, :]).T.astype(o_ref.dtype)


def _dsa_attention(proj, k, vt):
    TQ, KT = DSA_TQ, DSA_KT
    kw_block = (2 * D_A + IDX_HEADS * IDX_DIM + KV_RANK) // LANES
    pairs = [(i, kt) for i in range(SEQ // TQ) for kt in range((i * TQ + TQ - 1) // KT + 1)]
    qb = jnp.asarray([p[0] for p in pairs], jnp.int32)
    ktile = jnp.asarray([p[1] for p in pairs], jnp.int32)

    grid_spec = pltpu.PrefetchScalarGridSpec(
        num_scalar_prefetch=2,
        grid=(len(pairs),),
        in_specs=[
            pl.BlockSpec((TQ, D_B), lambda s, qb, kt: (qb[s], 1)),
            pl.BlockSpec((TQ, IDX_HEADS * IDX_DIM), lambda s, qb, kt: (qb[s], 2)),
            pl.BlockSpec((TQ, LANES), lambda s, qb, kt: (qb[s], kw_block)),
            pl.BlockSpec((SEQ, LANES), lambda s, qb, kt: (0, kw_block)),
            pl.BlockSpec((KT, D_B), lambda s, qb, kt: (kt[s], 0)),
            pl.BlockSpec((1, D_B, KT), lambda s, qb, kt: (kt[s], 0, 0)),
        ],
        out_specs=pl.BlockSpec((TQ, D_B), lambda s, qb, kt: (qb[s], 0)),
        scratch_shapes=[
            pltpu.VMEM((SEQ // DSA_SUB, DSA_SUB, TQ), F32),
            pltpu.VMEM((B_HEADS, KT, TQ), F32),
            pltpu.VMEM((KT, TQ), F32),
            pltpu.VMEM((2, KT, TQ), F32),
            pltpu.VMEM((2, KT, TQ), BF16),
            pltpu.VMEM((D_B, TQ), BF16),
            pltpu.VMEM((IDX_HEADS, TQ, IDX_DIM), BF16),
            pltpu.VMEM((IDX_HEADS, TQ), F32),
            pltpu.VMEM((1, TQ), F32),
            pltpu.VMEM((1, TQ), F32),
            pltpu.VMEM((1, TQ), F32),
            pltpu.VMEM((1, TQ), F32),
            pltpu.VMEM((B_HEADS, TQ), F32),
            pltpu.VMEM((B_HEADS, TQ), F32),
            pltpu.VMEM((D_B, TQ), F32),
        ],
    )
    return pl.pallas_call(
        _dsa_kernel,
        grid_spec=grid_spec,
        out_shape=jax.ShapeDtypeStruct((SEQ, D_B), BF16),
        compiler_params=_cparams(("arbitrary",)),
        name="dsa_attention",
    )(qb, ktile, proj, proj, proj, proj, k, vt)


def _ca_kernel(q_ref, k0_ref, k1_ref, k2_ref, v0_ref, v1_ref, v2_ref, relw_ref, o_ref, bias_ref, lg_ref, pb_ref):
    i = pl.program_id(1)
    k_refs = (k0_ref, k1_ref, k2_ref)
    v_refs = (v0_ref, v1_ref, v2_ref)
    width = CA_NKB * CA_TQ

    @pl.when(i == 0)
    def _():
        r_chunk = lax.broadcasted_iota(jnp.int32, (CA_TQ, width), 0) // CHUNK
        c_chunk = lax.broadcasted_iota(jnp.int32, (CA_TQ, width), 1) // CHUNK
        band = jnp.logical_and(c_chunk >= r_chunk, c_chunk <= r_chunk + C_LEFT_CHUNKS)
        for hh in range(CA_HB):
            row = jnp.broadcast_to(relw_ref[hh], (CA_TQ, CA_PERIOD))
            rolled = pltpu.roll(row, CA_PERIOD - (CA_TQ - 1), 1, stride=1, stride_axis=0)
            bias_ref[hh] = jnp.where(band, rolled[:, :width] * LOG2E, NEG_BIG)

    def qk(hh):
        hs = slice(hh * HEAD_DIM, (hh + 1) * HEAD_DIM)
        qh = q_ref[:, hs]
        for j in range(CA_NKB):
            cols = slice(j * CA_TQ, (j + 1) * CA_TQ)
            lg = lax.dot_general(qh, k_refs[j][:, hs], (((1,), (1,)), ((), ())), preferred_element_type=F32)
            lg = lg * (HEAD_DIM ** -0.5 * LOG2E) + bias_ref[hh, :, cols]
            lg_ref[hh % 2, :, cols] = jnp.where(i + j >= CA_NKB - 1, lg, NEG_BIG)

    qk(0)
    for hh in range(CA_HB):
        if hh + 1 < CA_HB:
            qk(hh + 1)
        hs = slice(hh * HEAD_DIM, (hh + 1) * HEAD_DIM)
        lg = lg_ref[hh % 2]
        m = jnp.max(lg, axis=-1, keepdims=True)
        p = jnp.exp2(lg - m)
        l = jnp.sum(p, axis=-1, keepdims=True)
        pb_ref[hh % 2] = p.astype(BF16)
        acc = jnp.dot(pb_ref[hh % 2, :, :CA_TQ], v_refs[0][:, hs], preferred_element_type=F32)
        for j in range(1, CA_NKB):
            acc += jnp.dot(pb_ref[hh % 2, :, j * CA_TQ:(j + 1) * CA_TQ], v_refs[j][:, hs],
                           preferred_element_type=F32)
        o_ref[:, hs] = (acc / l).astype(o_ref.dtype)


def _ca_rel_row(rel_bias):
    n_heads = rel_bias.shape[0]
    span = CA_NKB * CA_TQ + CA_TQ - 1
    v = jnp.concatenate([rel_bias.astype(F32)[:, MAX_REL - CA_TQ + 1:],
                         jnp.broadcast_to(rel_bias.astype(F32)[:, -1:], (n_heads, span - MAX_REL - CA_TQ))], axis=1)
    return jnp.pad(v[:, ::-1], ((0, 0), (0, CA_PERIOD - span))).reshape(n_heads, 1, CA_PERIOD)


def _chunk_attention(qkv, rel_row):
    nt = SEQ // CA_TQ
    hw = CA_HB * HEAD_DIM
    nhb = C_HEADS // CA_HB

    def kmap(j, base):
        return lambda hb, i: (jnp.maximum(i - (CA_NKB - 1) + j, 0), base + hb)

    in_specs = [pl.BlockSpec((CA_TQ, hw), lambda hb, i: (i, hb))]
    in_specs += [pl.BlockSpec((CA_TQ, hw), kmap(j, nhb)) for j in range(CA_NKB)]
    in_specs += [pl.BlockSpec((CA_TQ, hw), kmap(j, 2 * nhb)) for j in range(CA_NKB)]
    in_specs += [pl.BlockSpec((CA_HB, 1, CA_PERIOD), lambda hb, i: (hb, 0, 0))]
    return pl.pallas_call(
        _ca_kernel,
        grid=(nhb, nt),
        in_specs=in_specs,
        out_specs=pl.BlockSpec((CA_TQ, hw), lambda hb, i: (i, hb)),
        out_shape=jax.ShapeDtypeStruct((SEQ, D_MODEL), BF16),
        scratch_shapes=[pltpu.VMEM((CA_HB, CA_TQ, CA_NKB * CA_TQ), F32),
                        pltpu.VMEM((2, CA_TQ, CA_NKB * CA_TQ), F32),
                        pltpu.VMEM((2, CA_TQ, CA_NKB * CA_TQ), BF16)],
        compiler_params=_cparams(("arbitrary", "arbitrary")),
        name="chunk_attention",
    )(qkv, qkv, qkv, qkv, qkv, qkv, qkv, rel_row)


def _pad_cols(w, n):
    return jnp.pad(w, ((0, 0), (0, n - w.shape[1])))


def kernel(x, c, ada_w, ada_b, norm_g, ffn_w_gate, ffn_w_up, ffn_w_down, ab_w_in, s5_lam_re, s5_lam_im, s5_log_dt, s5_b_re, s5_b_im, s5_c_re, s5_c_im, s5_d, s5_w_glu, s5_b_glu, dsa_kv_norm_g, dsa_w_kv_up, ab_w_out, c_w_qkv, c_rel_bias, c_w_out, final_norm_g):
    mod = _modulation(c, ada_w, ada_b)
    h = x.reshape(SEQ, D_MODEL)
    wg, wu, wd = ffn_w_gate, ffn_w_up, ffn_w_down

    for layer in range(DEPTH):
        h = _ffn(h, norm_g[layer, 0], mod[layer, 0], wg, wu, wd, layer, 0)
        if layer % 2 == 0:
            e = layer // 2
            w_in = ab_w_in[e]
            o_q, o_kv, o_qi = D_A, D_A + D_B, D_A + D_B + KV_RANK
            o_ki = o_qi + IDX_HEADS * IDX_DIM
            w_in = jnp.concatenate([w_in[:, :o_kv], w_in[:, o_qi:o_ki], w_in[:, o_kv:o_qi], w_in[:, o_ki:]], axis=1)
            proj = _norm_proj(h, norm_g[layer, 1], mod[layer, 1], _pad_cols(w_in, AB_N_PAD), F32, PROJ_TN)

            a_re, a_im, bb_re, bb_im = _s5_params(s5_lam_re[e], s5_lam_im[e], s5_log_dt[e], s5_b_re[e], s5_b_im[e])
            n_blk, ts = SEQ // S5_T, S5_T // S5_NSEG
            u_il = (proj[:, :D_A].reshape(n_blk, S5_NSEG, ts, D_A).transpose(0, 2, 1, 3).reshape(SEQ, D_A))
            y_a = _s5_mixer(u_il, a_re, a_im, bb_re, bb_im, s5_c_re[e], s5_c_im[e], s5_d[e],
                            s5_w_glu[e].astype(BF16), s5_b_glu[e])
            y_a = y_a.reshape(n_blk, ts, S5_NSEG, D_A).transpose(0, 2, 1, 3).reshape(SEQ, D_A)

            k, vt = _kv_up(proj, dsa_kv_norm_g[e], dsa_w_kv_up[e].astype(BF16))
            y_b = _dsa_attention(proj, k, vt)

            w_out = ab_w_out[e].astype(BF16)
            h = _out_proj([y_a, y_b], [w_out[:D_A], w_out[D_A:]], h, mod[layer, 1])
        else:
            o = layer // 2
            qkv = _norm_proj(h, norm_g[layer, 1], mod[layer, 1], c_w_qkv[o], BF16, QKV_TN)
            att = _chunk_attention(qkv, _ca_rel_row(c_rel_bias[o]))
            h = _out_proj([att], [c_w_out[o].astype(BF16)], h, mod[layer, 1])
        h = _ffn(h, norm_g[layer, 2], mod[layer, 2], wg, wu, wd, layer, 1,
                 final_g=final_norm_g if layer == DEPTH - 1 else None)
    return h.reshape(1, SEQ, D_MODEL)
```

```python
import functools
import math

import jax
import jax.numpy as jnp
import numpy as np
from jax import lax
from jax.experimental import pallas as pl
from jax.experimental.pallas import tpu as pltpu

F32 = jnp.float32
BF16 = jnp.bfloat16

D_MODEL = 2048
SEQ = 8192
DEPTH = 2
CHUNK = 64
HEAD_DIM = 128
D_FF = 5504
N_SUB = 3
EPS = 1e-6
D_A = D_MODEL // 2
S5_GROUP = 16
S5_GROUPS = D_A // S5_GROUP
S5_STATE = 64
D_B = D_MODEL // 2
B_HEADS = D_B // HEAD_DIM
KV_RANK = D_MODEL // 8
IDX_HEADS = 16
IDX_DIM = 64
TOPK = 256
C_HEADS = D_MODEL // HEAD_DIM
C_LEFT_CHUNKS = 8
MAX_REL = 256

LANES = 128
SUBLANES = 8
VMEM_LIMIT = 56 * 1024 * 1024
NEG_BIG = -1e30
LOG2E = math.log2(math.e)
F32_MAX = float(np.finfo(np.float32).max)

FFN_TM = 1024
FFN_TF = 256
FFN_VMEM_LIMIT = 60 * 1024 * 1024
PROJ_TM = 1024
PROJ_TN = 512
QKV_TN = 1024
OUT_TM = 512
AB_N_PAD = 3584
S5_T = 256
S5_NSEG = SUBLANES
S5_SLICES = D_A // LANES
S5_NSTATE = S5_GROUPS * S5_STATE
DSA_TQ = 256
DSA_KT = 1024
DSA_SUB = 256
CA_TQ = 256
CA_HB = 8
CA_NKB = 3
CA_PERIOD = 1024


def _cparams(sem):
    return pltpu.CompilerParams(dimension_semantics=sem, vmem_limit_bytes=VMEM_LIMIT)


def _mod_kernel(c_ref, w_ref, b_ref, o_ref):
    rows = 256
    tn = o_ref.shape[-1]

    def body(i, acc):
        r0 = pl.multiple_of(i * rows, rows)
        cc = c_ref[pl.ds(r0, rows), :]
        cc = cc * jax.nn.sigmoid(cc)
        w = w_ref[0, pl.ds(r0, rows), :]
        return acc + jnp.sum((w * cc).reshape(rows // SUBLANES, SUBLANES, tn), axis=0)

    acc = lax.fori_loop(0, D_MODEL // rows, body, jnp.zeros((SUBLANES, tn), F32))
    o_ref[0] = jnp.sum(acc, axis=0, keepdims=True) + b_ref[0]


def _modulation(c, ada_w, ada_b):
    n = N_SUB * 3 * D_MODEL
    tn = 1024
    c_col = c.reshape(D_MODEL, 1)
    out = pl.pallas_call(
        _mod_kernel,
        grid=(DEPTH, n // tn),
        in_specs=[
            pl.BlockSpec((D_MODEL, 1), lambda l, j: (0, 0)),
            pl.BlockSpec((1, D_MODEL, tn), lambda l, j: (l, 0, j)),
            pl.BlockSpec((1, 1, tn), lambda l, j: (l, 0, j)),
        ],
        out_specs=pl.BlockSpec((1, 1, tn), lambda l, j: (l, 0, j)),
        out_shape=jax.ShapeDtypeStruct((DEPTH, 1, n), F32),
        compiler_params=_cparams(("arbitrary", "arbitrary")),
        name="adaln_mod",
    )(c_col, ada_w, ada_b.reshape(DEPTH, 1, n))
    return out.reshape(DEPTH, N_SUB, 3, D_MODEL)


ADALN_ROWS = 32


def _adaln_to(hn_ref, h_ref, g_ref, mod_ref):
    g = g_ref[...]
    scale1 = 1.0 + mod_ref[1:2, :]
    shift = mod_ref[0:1, :]

    def body(r, c):
        rows = pl.ds(pl.multiple_of(r * ADALN_ROWS, ADALN_ROWS), ADALN_ROWS)
        x = h_ref[rows, :]
        ms = jnp.mean(x * x, axis=-1, keepdims=True)
        y = (x * lax.rsqrt(ms + EPS)) * g
        hn_ref[rows, :] = (y * scale1 + shift).astype(hn_ref.dtype)
        return c

    lax.fori_loop(0, h_ref.shape[0] // ADALN_ROWS, body, 0, unroll=4)


def _ffn_kernel(h_ref, g_ref, mod_ref, wg_ref, wu_ref, wd_ref, *rest, n_f, final):
    if final:
        fg_ref, o_ref, hn_ref = rest
    else:
        o_ref, hn_ref = rest
    f = pl.program_id(1)

    @pl.when(f == 0)
    def _():
        _adaln_to(hn_ref, h_ref, g_ref, mod_ref)
        o_ref[...] = jnp.zeros_like(o_ref)

    def accumulate(width):
        hn = hn_ref[...]
        gate = jnp.dot(hn, wg_ref[:, :width].astype(BF16), preferred_element_type=F32)
        up = jnp.dot(hn, wu_ref[:, :width].astype(BF16), preferred_element_type=F32)
        act = (gate * jax.nn.sigmoid(gate)) * up
        o_ref[...] += jnp.dot(act.astype(BF16), wd_ref[:width, :].astype(BF16), preferred_element_type=F32)

    @pl.when(f < n_f - 1)
    def _():
        accumulate(FFN_TF)

    @pl.when(f == n_f - 1)
    def _():
        accumulate(D_FF - (n_f - 1) * FFN_TF)
        half_gate = 0.5 * mod_ref[2:3, :]

        def finish(r, c):
            rows = pl.ds(pl.multiple_of(r * ADALN_ROWS, ADALN_ROWS), ADALN_ROWS)
            out = h_ref[rows, :] + half_gate * o_ref[rows, :]
            if final:
                ms = jnp.mean(out * out, axis=-1, keepdims=True)
                out = (out * lax.rsqrt(ms + EPS)) * fg_ref[...]
            o_ref[rows, :] = out
            return c

        lax.fori_loop(0, FFN_TM // ADALN_ROWS, finish, 0, unroll=4)


def _ffn(h, g, mod, wg, wu, wd, layer, which, final_g=None):
    n_f = pl.cdiv(D_FF, FFN_TF)
    final = final_g is not None
    in_specs = [
        pl.BlockSpec((FFN_TM, D_MODEL), lambda i, f: (i, 0)),
        pl.BlockSpec((1, D_MODEL), lambda i, f: (0, 0)),
        pl.BlockSpec((3, D_MODEL), lambda i, f: (0, 0)),
        pl.BlockSpec((None, None, D_MODEL, FFN_TF), lambda i, f: (layer, which, 0, f)),
        pl.BlockSpec((None, None, D_MODEL, FFN_TF), lambda i, f: (layer, which, 0, f)),
        pl.BlockSpec((None, None, FFN_TF, D_MODEL), lambda i, f: (layer, which, f, 0)),
    ]
    args = [h, g.reshape(1, D_MODEL), mod, wg, wu, wd]
    if final:
        in_specs.append(pl.BlockSpec((1, D_MODEL), lambda i, f: (0, 0)))
        args.append(final_g.reshape(1, D_MODEL))
    return pl.pallas_call(
        functools.partial(_ffn_kernel, n_f=n_f, final=final),
        grid=(SEQ // FFN_TM, n_f),
        in_specs=in_specs,
        out_specs=pl.BlockSpec((FFN_TM, D_MODEL), lambda i, f: (i, 0)),
        out_shape=jax.ShapeDtypeStruct((SEQ, D_MODEL), F32),
        scratch_shapes=[pltpu.VMEM((FFN_TM, D_MODEL), BF16)],
        compiler_params=pltpu.CompilerParams(dimension_semantics=("arbitrary", "arbitrary"),
                                             vmem_limit_bytes=FFN_VMEM_LIMIT),
        name="ffn_swiglu",
    )(*args)


def _proj_kernel(h_ref, g_ref, mod_ref, w_ref, o_ref, hn_ref):
    @pl.when(pl.program_id(1) == 0)
    def _():
        _adaln_to(hn_ref, h_ref, g_ref, mod_ref)

    o_ref[...] = jnp.dot(hn_ref[...], w_ref[...].astype(BF16), preferred_element_type=F32).astype(o_ref.dtype)


def _norm_proj(h, g, mod, w, out_dtype, tn):
    n = w.shape[1]
    return pl.pallas_call(
        _proj_kernel,
        grid=(SEQ // PROJ_TM, n // tn),
        in_specs=[
            pl.BlockSpec((PROJ_TM, D_MODEL), lambda i, j: (i, 0)),
            pl.BlockSpec((1, D_MODEL), lambda i, j: (0, 0)),
            pl.BlockSpec((3, D_MODEL), lambda i, j: (0, 0)),
            pl.BlockSpec((D_MODEL, tn), lambda i, j: (0, j)),
        ],
        out_specs=pl.BlockSpec((PROJ_TM, tn), lambda i, j: (i, j)),
        out_shape=jax.ShapeDtypeStruct((SEQ, n), out_dtype),
        scratch_shapes=[pltpu.VMEM((PROJ_TM, D_MODEL), BF16)],
        compiler_params=_cparams(("arbitrary", "arbitrary")),
        name="adaln_proj",
    )(h, g.reshape(1, D_MODEL), mod, w)


def _out_kernel(*refs, n_lhs):
    lhs = refs[:n_lhs]
    ws = refs[n_lhs:2 * n_lhs]
    h_ref, mod_ref, o_ref = refs[2 * n_lhs:]
    y = jnp.dot(lhs[0][...], ws[0][...], preferred_element_type=F32)
    for a_ref, w_ref in zip(lhs[1:], ws[1:]):
        y += jnp.dot(a_ref[...], w_ref[...], preferred_element_type=F32)
    o_ref[...] = h_ref[...] + mod_ref[2:3, :] * y


def _out_proj(lhs, ws, h, mod):
    n_lhs = len(lhs)
    in_specs = [pl.BlockSpec((OUT_TM, a.shape[1]), lambda i: (i, 0)) for a in lhs]
    in_specs += [pl.BlockSpec(w.shape, lambda i: (0, 0)) for w in ws]
    in_specs += [pl.BlockSpec((OUT_TM, D_MODEL), lambda i: (i, 0)), pl.BlockSpec((3, D_MODEL), lambda i: (0, 0))]
    return pl.pallas_call(
        functools.partial(_out_kernel, n_lhs=n_lhs),
        grid=(SEQ // OUT_TM,),
        in_specs=in_specs,
        out_specs=pl.BlockSpec((OUT_TM, D_MODEL), lambda i: (i, 0)),
        out_shape=jax.ShapeDtypeStruct((SEQ, D_MODEL), F32),
        compiler_params=_cparams(("arbitrary",)),
        name="out_proj_residual",
    )(*lhs, *ws, h, mod)


def _s5_param_kernel(lr_ref, li_ref, ldt_ref, br_ref, bi_ref, are_ref, aim_ref, bbr_ref, bbi_ref):
    lr = lr_ref[...]
    li = li_ref[...]
    dt = jnp.exp(ldt_ref[...])
    mag = jnp.exp(lr * dt)
    ab_re = mag * jnp.cos(li * dt)
    ab_im = mag * jnp.sin(li * dt)
    den = lr * lr + li * li
    nr = ab_re - 1.0
    f_re = (nr * lr + ab_im * li) / den
    f_im = (ab_im * lr - nr * li) / den
    are_ref[...] = ab_re
    aim_ref[...] = ab_im
    br = br_ref[...]
    bi = bi_ref[...]
    bbr_ref[...] = f_re * br - f_im * bi
    bbi_ref[...] = f_re * bi + f_im * br


def _s5_params(lam_re, lam_im, log_dt, b_re, b_im):
    G, P, CG = S5_GROUPS, S5_STATE, S5_GROUP
    return pl.pallas_call(
        _s5_param_kernel,
        out_shape=[jax.ShapeDtypeStruct((G, 1, P), F32), jax.ShapeDtypeStruct((G, 1, P), F32),
                   jax.ShapeDtypeStruct((G, CG, P), F32), jax.ShapeDtypeStruct((G, CG, P), F32)],
        name="s5_zoh_params",
    )(lam_re.reshape(G, 1, P), lam_im.reshape(G, 1, P), log_dt.reshape(G, 1, 1),
      jnp.swapaxes(b_re, 1, 2), jnp.swapaxes(b_im, 1, 2))


def _block_diag_slices(m):
    _, r, c = m.shape
    m4 = m.reshape(S5_SLICES, SUBLANES, r, c)
    eye = jnp.eye(SUBLANES, dtype=m.dtype)
    out = m4[:, :, :, None, :] * eye[None, :, None, :, None]
    return out.reshape(S5_SLICES, SUBLANES * r, SUBLANES * c)


def _s5_kernel(u_ref, bbr_ref, bbi_ref, are_ref, aim_ref, ccr_ref, cci_ref, d_ref, wglu_ref, bglu_ref,
               o_ref, xr_ref, xi_ref, pr_ref, pi_ref, er_ref, ei_ref, cr_ref, ci_ref):
    W = S5_NSTATE // S5_SLICES
    TS = S5_T // S5_NSEG
    CW = 512

    @pl.when(pl.program_id(0) == 0)
    def _():
        pr_ref[0:1, :] = are_ref[...]
        pi_ref[0:1, :] = aim_ref[...]

        def power(s, c):
            ar, ai = are_ref[...], aim_ref[...]
            qr, qi = pr_ref[pl.ds(s - 1, 1), :], pi_ref[pl.ds(s - 1, 1), :]
            pr_ref[pl.ds(s, 1), :] = ar * qr - ai * qi
            pi_ref[pl.ds(s, 1), :] = ar * qi + ai * qr
            return c

        lax.fori_loop(1, TS, power, 0)
        cr_ref[...] = jnp.zeros_like(cr_ref)
        ci_ref[...] = jnp.zeros_like(ci_ref)

    for k in range(S5_SLICES):
        uk = u_ref[:, k * LANES:(k + 1) * LANES].astype(BF16)
        xr_ref[:, k * W:(k + 1) * W] = jnp.dot(uk, bbr_ref[k], preferred_element_type=F32)
        xi_ref[:, k * W:(k + 1) * W] = jnp.dot(uk, bbi_ref[k], preferred_element_type=F32)

    for cg in range(S5_NSTATE // CW):
        cols = slice(cg * CW, (cg + 1) * CW)
        ar = jnp.broadcast_to(are_ref[:, cols], (S5_NSEG, CW))
        ai = jnp.broadcast_to(aim_ref[:, cols], (S5_NSEG, CW))

        def step(s, carry, cols=cols, ar=ar, ai=ai):
            sr, si = carry
            rows = pl.ds(pl.multiple_of(s * S5_NSEG, S5_NSEG), S5_NSEG)
            nr = ar * sr - ai * si + xr_ref[rows, cols]
            ni = ar * si + ai * sr + xi_ref[rows, cols]
            xr_ref[rows, cols] = nr
            xi_ref[rows, cols] = ni
            return nr, ni

        zero = jnp.zeros((S5_NSEG, CW), F32)
        er, ei = lax.fori_loop(0, TS, step, (zero, zero), unroll=4)
        er_ref[:, cols] = er
        ei_ref[:, cols] = ei

    pwr, pwi = pr_ref[TS - 1:TS, :], pi_ref[TS - 1:TS, :]
    c_r, c_i = cr_ref[S5_NSEG:S5_NSEG + 1, :], ci_ref[S5_NSEG:S5_NSEG + 1, :]
    for j in range(S5_NSEG):
        cr_ref[j:j + 1, :] = c_r
        ci_ref[j:j + 1, :] = c_i
        c_r, c_i = (er_ref[j:j + 1, :] + pwr * c_r - pwi * c_i,
                    ei_ref[j:j + 1, :] + pwr * c_i + pwi * c_r)
    cr_ref[S5_NSEG:S5_NSEG + 1, :] = c_r
    ci_ref[S5_NSEG:S5_NSEG + 1, :] = c_i

    for cg in range(S5_NSTATE // CW):
        cols = slice(cg * CW, (cg + 1) * CW)
        c_r, c_i = cr_ref[0:S5_NSEG, cols], ci_ref[0:S5_NSEG, cols]

        def fix(s, carry, cols=cols, c_r=c_r, c_i=c_i):
            rows = pl.ds(pl.multiple_of(s * S5_NSEG, S5_NSEG), S5_NSEG)
            p_r, p_i = pr_ref[pl.ds(s, 1), cols], pi_ref[pl.ds(s, 1), cols]
            xr_ref[rows, cols] = xr_ref[rows, cols] + (p_r * c_r - p_i * c_i)
            xi_ref[rows, cols] = xi_ref[rows, cols] + (p_r * c_i + p_i * c_r)
            return carry

        lax.fori_loop(0, TS, fix, 0, unroll=4)

    ys = []
    for k in range(S5_SLICES):
        xr = xr_ref[:, k * W:(k + 1) * W].astype(BF16)
        xi = xi_ref[:, k * W:(k + 1) * W].astype(BF16)
        ys.append(jnp.dot(xr, ccr_ref[k], preferred_element_type=F32)
                  + jnp.dot(xi, cci_ref[k], preferred_element_type=F32))
    y = jnp.concatenate(ys, axis=-1) + d_ref[...] * u_ref[...]
    y = jax.nn.gelu(y, approximate=True)
    z = jnp.dot(y.astype(BF16), wglu_ref[...], preferred_element_type=F32) + bglu_ref[...]
    o_ref[...] = (y * jax.nn.sigmoid(z)).astype(o_ref.dtype)


def _s5_mixer(proj, a_re, a_im, bb_re, bb_im, c_re, c_im, d_skip, w_glu, b_glu):
    W = S5_NSTATE // S5_SLICES
    bbr = _block_diag_slices(bb_re).astype(BF16)
    bbi = _block_diag_slices(bb_im).astype(BF16)
    ccr = _block_diag_slices(jnp.swapaxes(c_re, 1, 2)).astype(BF16)
    cci = _block_diag_slices(-jnp.swapaxes(c_im, 1, 2)).astype(BF16)
    const3 = lambda t: (0, 0, 0)
    const2 = lambda t: (0, 0)
    return pl.pallas_call(
        _s5_kernel,
        grid=(SEQ // S5_T,),
        in_specs=[
            pl.BlockSpec((S5_T, D_A), lambda t: (t, 0)),
            pl.BlockSpec((S5_SLICES, LANES, W), const3),
            pl.BlockSpec((S5_SLICES, LANES, W), const3),
            pl.BlockSpec((1, S5_NSTATE), const2),
            pl.BlockSpec((1, S5_NSTATE), const2),
            pl.BlockSpec((S5_SLICES, W, LANES), const3),
            pl.BlockSpec((S5_SLICES, W, LANES), const3),
            pl.BlockSpec((1, D_A), const2),
            pl.BlockSpec((D_A, D_A), const2),
            pl.BlockSpec((1, D_A), const2),
        ],
        out_specs=pl.BlockSpec((S5_T, D_A), lambda t: (t, 0)),
        out_shape=jax.ShapeDtypeStruct((SEQ, D_A), BF16),
        scratch_shapes=[pltpu.VMEM((S5_T, S5_NSTATE), F32), pltpu.VMEM((S5_T, S5_NSTATE), F32),
                        pltpu.VMEM((S5_T // S5_NSEG, S5_NSTATE), F32), pltpu.VMEM((S5_T // S5_NSEG, S5_NSTATE), F32),
                        pltpu.VMEM((S5_NSEG, S5_NSTATE), F32), pltpu.VMEM((S5_NSEG, S5_NSTATE), F32),
                        pltpu.VMEM((2 * S5_NSEG, S5_NSTATE), F32), pltpu.VMEM((2 * S5_NSEG, S5_NSTATE), F32)],
        compiler_params=_cparams(("arbitrary",)),
        name="s5_mixer",
    )(proj, bbr, bbi, a_re.reshape(1, S5_NSTATE), a_im.reshape(1, S5_NSTATE), ccr, cci,
      d_skip.reshape(1, D_A), w_glu, b_glu.reshape(1, D_A))


def _kv_kernel(lat_ref, g_ref, wk_ref, wvt_ref, k_ref, vt_ref):
    x = lat_ref[...]
    ms = jnp.mean(x * x, axis=-1, keepdims=True)
    xn = ((x * lax.rsqrt(ms + EPS)) * g_ref[...]).astype(BF16)
    k_ref[...] = jnp.dot(xn, wk_ref[...], preferred_element_type=F32).astype(k_ref.dtype)
    vt = lax.dot_general(wvt_ref[...], xn, (((1,), (1,)), ((), ())), preferred_element_type=F32)
    vt_ref[0] = vt.astype(vt_ref.dtype)


def _kv_up(proj, g, w):
    tm = DSA_KT
    lat_block = (2 * D_A + IDX_HEADS * IDX_DIM) // KV_RANK
    wk = w[:, :D_B]
    wvt = w[:, D_B:].T
    return pl.pallas_call(
        _kv_kernel,
        grid=(SEQ // tm,),
        in_specs=[
            pl.BlockSpec((tm, KV_RANK), lambda i: (i, lat_block)),
            pl.BlockSpec((1, KV_RANK), lambda i: (0, 0)),
            pl.BlockSpec((KV_RANK, D_B), lambda i: (0, 0)),
            pl.BlockSpec((D_B, KV_RANK), lambda i: (0, 0)),
        ],
        out_specs=[pl.BlockSpec((tm, D_B), lambda i: (i, 0)),
                   pl.BlockSpec((1, D_B, tm), lambda i: (i, 0, 0))],
        out_shape=[jax.ShapeDtypeStruct((SEQ, D_B), BF16),
                   jax.ShapeDtypeStruct((SEQ // tm, D_B, tm), BF16)],
        compiler_params=_cparams(("arbitrary",)),
        name="dsa_kv_up",
    )(proj, g.reshape(1, KV_RANK), wk, wvt)


def _dsa_kernel(qb_ref, kt_ref, q_ref, qidx_ref, w_ref, kidx_ref, k_ref, vt_ref, o_ref,
                sc_ref, a2_ref, mb_ref, lg_ref, pb_ref, qst_ref, qi_ref, wt_ref, thr_ref, lo_ref, hi_ref, clo_ref,
                m_ref, l_ref, acc_ref):
    TQ, KT, SUB = DSA_TQ, DSA_KT, DSA_SUB
    K = float(TOPK)
    step = pl.program_id(0)
    i = qb_ref[step]
    kt = kt_ref[step]
    last_kt = (i * TQ + TQ - 1) // KT
    n_sub = i + 1
    q_pos = i * TQ + lax.broadcasted_iota(jnp.int32, (1, TQ), 1)
    q_chunk = q_pos // CHUNK

    @pl.when(step == 0)
    def _():
        d = (lax.broadcasted_iota(jnp.int32, (KT, TQ), 1) - lax.broadcasted_iota(jnp.int32, (KT, TQ), 0)).astype(F32)
        for h in range(B_HEADS):
            a2_ref[h] = (LOG2E * 2.0 ** (-8.0 * (h + 1) / B_HEADS)) * d

    @pl.when(kt == 0)
    def _():
        qst_ref[...] = (q_ref[...] * (HEAD_DIM ** -0.5 * LOG2E)).T.astype(BF16)
        qv = qidx_ref[...]
        for h in range(IDX_HEADS):
            qi_ref[h] = qv[:, h * IDX_DIM:(h + 1) * IDX_DIM].astype(BF16)
        wt_ref[...] = w_ref[...].T[IDX_DIM:IDX_DIM + IDX_HEADS, :]

        def scores(g):
            s0 = pl.multiple_of(g * SUB, SUB)
            kk = kidx_ref[pl.ds(s0, SUB), :IDX_DIM].astype(BF16)
            tot = jnp.zeros((SUB, TQ), F32)
            for h in range(IDX_HEADS):
                s = lax.dot_general(kk, qi_ref[h], (((1,), (1,)), ((), ())), preferred_element_type=F32)
                tot = tot + jnp.maximum(s, 0.0) * wt_ref[h:h + 1, :]
            return tot * ((IDX_DIM ** -0.5) * (IDX_HEADS ** -0.5))

        def past_tile(g, carry):
            mx, mn = carry
            tot = scores(g)
            sc_ref[g] = tot
            return (jnp.maximum(mx, jnp.max(tot, axis=0, keepdims=True)),
                    jnp.minimum(mn, jnp.min(tot, axis=0, keepdims=True)))

        def past_group(g4, carry):
            for n in range(4):
                carry = past_tile(4 * g4 + n, carry)
            return carry

        carry = lax.fori_loop(0, i // 4, past_group,
                              (jnp.full((1, TQ), -jnp.inf, F32), jnp.full((1, TQ), jnp.inf, F32)))
        mx, mn = lax.fori_loop(4 * (i // 4), i, past_tile, carry)
        tot = scores(i)
        key_chunk = (i * SUB + lax.broadcasted_iota(jnp.int32, (SUB, 1), 0)) // CHUNK
        adm = key_chunk <= q_chunk
        sc_ref[i] = jnp.where(adm, tot, -jnp.inf)
        mx = jnp.maximum(mx, jnp.max(jnp.where(adm, tot, -jnp.inf), axis=0, keepdims=True))
        mn = jnp.minimum(mn, jnp.min(jnp.where(adm, tot, jnp.inf), axis=0, keepdims=True))

        def fill_tile(g, c):
            sc_ref[g] = jnp.full((SUB, TQ), -jnp.inf, F32)
            return c

        lax.fori_loop(n_sub, (last_kt + 1) * (KT // SUB), fill_tile, 0)

        n_adm = ((q_chunk + 1) * CHUNK).astype(F32)
        keep_all = n_adm <= K
        lo_ref[...] = jnp.where(keep_all, -F32_MAX, mn)
        hi_ref[...] = mx
        clo_ref[...] = jnp.where(keep_all, K, n_adm)

        def count_ge(mid):
            def tile_count(g):
                ind = jnp.where(sc_ref[g] >= mid, 1.0, 0.0)
                return jnp.sum(ind.reshape(SUB // SUBLANES, SUBLANES, TQ), axis=0)

            def body(g2, acc):
                return acc + tile_count(2 * g2) + tile_count(2 * g2 + 1)

            acc = lax.fori_loop(0, n_sub // 2, body, jnp.zeros((SUBLANES, TQ), F32))
            odd = (n_sub % 2).astype(F32)
            acc = acc + odd * tile_count(n_sub - 1)
            return jnp.sum(acc, axis=0, keepdims=True)

        def n_open():
            return jnp.max(jnp.where(clo_ref[...] != K, 1.0, 0.0))

        def cond(c):
            it, open_rows = c
            return jnp.logical_and(it < 64, open_rows > 0.5)

        def body(c):
            it, _ = c
            lo, hi, clo = lo_ref[...], hi_ref[...], clo_ref[...]
            mid = lo + 0.5 * (hi - lo)
            c_mid = count_ge(mid)
            live = clo != K
            ge = jnp.logical_and(live, c_mid >= K)
            lt = jnp.logical_and(live, c_mid < K)
            lo_ref[...] = jnp.where(ge, mid, lo)
            clo_ref[...] = jnp.where(ge, c_mid, clo)
            hi_ref[...] = jnp.where(lt, mid, hi)
            return it + 1, n_open()

        lax.while_loop(cond, body, (jnp.int32(0), n_open()))
        thr_ref[...] = lo_ref[...]

        m_ref[...] = jnp.full(m_ref.shape, NEG_BIG, F32)
        l_ref[...] = jnp.zeros_like(l_ref)
        acc_ref[...] = jnp.zeros_like(acc_ref)

    def attend(last, nk):
        thr = thr_ref[...]
        for jj in range(nk // SUB):
            mb_ref[jj * SUB:(jj + 1) * SUB, :] = jnp.where(sc_ref[kt * (KT // SUB) + jj] >= thr, 0.0, NEG_BIG)
        gap = (i * TQ - kt * KT).astype(F32)
        m_all, l_all = m_ref[...], l_ref[...]
        m_rows, l_rows = [], []

        def qk(h):
            hs = slice(h * HEAD_DIM, (h + 1) * HEAD_DIM)
            for half in range(2):
                rows = slice(half * (nk // 2), (half + 1) * (nk // 2))
                lg_ref[h % 2, rows, :] = jnp.dot(k_ref[rows, hs], qst_ref[hs, :], preferred_element_type=F32)

        qk(0)
        for h in range(B_HEADS):
            if h + 1 < B_HEADS:
                qk(h + 1)
            hs = slice(h * HEAD_DIM, (h + 1) * HEAD_DIM)
            slope2 = LOG2E * 2.0 ** (-8.0 * (h + 1) / B_HEADS)
            if last:
                lg = (lg_ref[h % 2, :nk, :] - jnp.abs(a2_ref[h, :nk, :] + slope2 * gap)) + mb_ref[:nk, :]
                off = 0.0
            else:
                lg = (lg_ref[h % 2, :nk, :] - a2_ref[h, :nk, :]) + mb_ref[:nk, :]
                off = slope2 * gap
            lg_ref[h % 2, :nk, :] = lg
            m_old = m_all[h:h + 1, :]
            m_new = jnp.maximum(m_old, jnp.max(lg, axis=0, keepdims=True) - off)
            alpha = jnp.exp2(m_old - m_new)
            p = jnp.exp2(lg_ref[h % 2, :nk, :] - (m_new + off))
            l_rows.append(alpha * l_all[h:h + 1, :] + jnp.sum(p, axis=0, keepdims=True))
            m_rows.append(m_new)
            pb_ref[h % 2, :nk, :] = p.astype(BF16)
            pv = jnp.dot(vt_ref[0, hs, :nk], pb_ref[h % 2, :nk, :], preferred_element_type=F32)
            acc_ref[hs, :] = alpha * acc_ref[hs, :] + pv
        m_ref[...] = jnp.concatenate(m_rows, axis=0)
        l_ref[...] = jnp.concatenate(l_rows, axis=0)

    @pl.when(kt < last_kt)
    def _():
        attend(False, KT)

    own_sub = i - last_kt * (KT // SUB)

    @pl.when(jnp.logical_and(kt == last_kt, own_sub < KT // SUB // 2))
    def _():
        attend(True, KT // 2)

    @pl.when(jnp.logical_and(kt == last_kt, own_sub >= KT // SUB // 2))
    def _():
        attend(True, KT)

    @pl.when(kt == last_kt)
    def _():
        for h in range(B_HEADS):
            hs = slice(h * HEAD_DIM, (h + 1) * HEAD_DIM)
            o_ref[:, hs] = (acc_ref[hs, :] / l_ref[h:h + 1, :]).T.astype(o_ref.dtype)


def _dsa_attention(proj, k, vt):
    TQ, KT = DSA_TQ, DSA_KT
    kw_block = (2 * D_A + IDX_HEADS * IDX_DIM + KV_RANK) // LANES
    pairs = [(i, kt) for i in range(SEQ // TQ) for kt in range((i * TQ + TQ - 1) // KT + 1)]
    qb = jnp.asarray([p[0] for p in pairs], jnp.int32)
    ktile = jnp.asarray([p[1] for p in pairs], jnp.int32)

    grid_spec = pltpu.PrefetchScalarGridSpec(
        num_scalar_prefetch=2,
        grid=(len(pairs),),
        in_specs=[
            pl.BlockSpec((TQ, D_B), lambda s, qb, kt: (qb[s], 1)),
            pl.BlockSpec((TQ, IDX_HEADS * IDX_DIM), lambda s, qb, kt: (qb[s], 2)),
            pl.BlockSpec((TQ, LANES), lambda s, qb, kt: (qb[s], kw_block)),
            pl.BlockSpec((SEQ, LANES), lambda s, qb, kt: (0, kw_block)),
            pl.BlockSpec((KT, D_B), lambda s, qb, kt: (kt[s], 0)),
            pl.BlockSpec((1, D_B, KT), lambda s, qb, kt: (kt[s], 0, 0)),
        ],
        out_specs=pl.BlockSpec((TQ, D_B), lambda s, qb, kt: (qb[s], 0)),
        scratch_shapes=[
            pltpu.VMEM((SEQ // DSA_SUB, DSA_SUB, TQ), F32),
            pltpu.VMEM((B_HEADS, KT, TQ), F32),
            pltpu.VMEM((KT, TQ), F32),
            pltpu.VMEM((2, KT, TQ), F32),
            pltpu.VMEM((2, KT, TQ), BF16),
            pltpu.VMEM((D_B, TQ), BF16),
            pltpu.VMEM((IDX_HEADS, TQ, IDX_DIM), BF16),
            pltpu.VMEM((IDX_HEADS, TQ), F32),
            pltpu.VMEM((1, TQ), F32),
            pltpu.VMEM((1, TQ), F32),
            pltpu.VMEM((1, TQ), F32),
            pltpu.VMEM((1, TQ), F32),
            pltpu.VMEM((B_HEADS, TQ), F32),
            pltpu.VMEM((B_HEADS, TQ), F32),
            pltpu.VMEM((D_B, TQ), F32),
        ],
    )
    return pl.pallas_call(
        _dsa_kernel,
        grid_spec=grid_spec,
        out_shape=jax.ShapeDtypeStruct((SEQ, D_B), BF16),
        compiler_params=_cparams(("arbitrary",)),
        name="dsa_attention",
    )(qb, ktile, proj, proj, proj, proj, k, vt)


def _ca_kernel(q_ref, k0_ref, k1_ref, k2_ref, v0_ref, v1_ref, v2_ref, relw_ref, o_ref, bias_ref, lg_ref, pb_ref):
    i = pl.program_id(1)
    k_refs = (k0_ref, k1_ref, k2_ref)
    v_refs = (v0_ref, v1_ref, v2_ref)
    width = CA_NKB * CA_TQ

    @pl.when(i == 0)
    def _():
        r_chunk = lax.broadcasted_iota(jnp.int32, (CA_TQ, width), 0) // CHUNK
        c_chunk = lax.broadcasted_iota(jnp.int32, (CA_TQ, width), 1) // CHUNK
        band = jnp.logical_and(c_chunk >= r_chunk, c_chunk <= r_chunk + C_LEFT_CHUNKS)
        for hh in range(CA_HB):
            row = jnp.broadcast_to(relw_ref[hh], (CA_TQ, CA_PERIOD))
            rolled = pltpu.roll(row, CA_PERIOD - (CA_TQ - 1), 1, stride=1, stride_axis=0)
            bias_ref[hh] = jnp.where(band, rolled[:, :width] * LOG2E, NEG_BIG)

    def qk(hh):
        hs = slice(hh * HEAD_DIM, (hh + 1) * HEAD_DIM)
        qh = q_ref[:, hs]
        for j in range(CA_NKB):
            cols = slice(j * CA_TQ, (j + 1) * CA_TQ)
            lg = lax.dot_general(qh, k_refs[j][:, hs], (((1,), (1,)), ((), ())), preferred_element_type=F32)
            lg = lg * (HEAD_DIM ** -0.5 * LOG2E) + bias_ref[hh, :, cols]
            lg_ref[hh % 2, :, cols] = jnp.where(i + j >= CA_NKB - 1, lg, NEG_BIG)

    qk(0)
    for hh in range(CA_HB):
        if hh + 1 < CA_HB:
            qk(hh + 1)
        hs = slice(hh * HEAD_DIM, (hh + 1) * HEAD_DIM)
        lg = lg_ref[hh % 2]
        m = jnp.max(lg, axis=-1, keepdims=True)
        p = jnp.exp2(lg - m)
        l = jnp.sum(p, axis=-1, keepdims=True)
        pb_ref[hh % 2] = p.astype(BF16)
        acc = jnp.dot(pb_ref[hh % 2, :, :CA_TQ], v_refs[0][:, hs], preferred_element_type=F32)
        for j in range(1, CA_NKB):
            acc += jnp.dot(pb_ref[hh % 2, :, j * CA_TQ:(j + 1) * CA_TQ], v_refs[j][:, hs],
                           preferred_element_type=F32)
        o_ref[:, hs] = (acc / l).astype(o_ref.dtype)


def _ca_rel_row(rel_bias):
    n_heads = rel_bias.shape[0]
    span = CA_NKB * CA_TQ + CA_TQ - 1
    v = jnp.concatenate([rel_bias.astype(F32)[:, MAX_REL - CA_TQ + 1:],
                         jnp.broadcast_to(rel_bias.astype(F32)[:, -1:], (n_heads, span - MAX_REL - CA_TQ))], axis=1)
    return jnp.pad(v[:, ::-1], ((0, 0), (0, CA_PERIOD - span))).reshape(n_heads, 1, CA_PERIOD)


def _chunk_attention(qkv, rel_row):
    nt = SEQ // CA_TQ
    hw = CA_HB * HEAD_DIM
    nhb = C_HEADS // CA_HB

    def kmap(j, base):
        return lambda hb, i: (jnp.maximum(i - (CA_NKB - 1) + j, 0), base + hb)

    in_specs = [pl.BlockSpec((CA_TQ, hw), lambda hb, i: (i, hb))]
    in_specs += [pl.BlockSpec((CA_TQ, hw), kmap(j, nhb)) for j in range(CA_NKB)]
    in_specs += [pl.BlockSpec((CA_TQ, hw), kmap(j, 2 * nhb)) for j in range(CA_NKB)]
    in_specs += [pl.BlockSpec((CA_HB, 1, CA_PERIOD), lambda hb, i: (hb, 0, 0))]
    return pl.pallas_call(
        _ca_kernel,
        grid=(nhb, nt),
        in_specs=in_specs,
        out_specs=pl.BlockSpec((CA_TQ, hw), lambda hb, i: (i, hb)),
        out_shape=jax.ShapeDtypeStruct((SEQ, D_MODEL), BF16),
        scratch_shapes=[pltpu.VMEM((CA_HB, CA_TQ, CA_NKB * CA_TQ), F32),
                        pltpu.VMEM((2, CA_TQ, CA_NKB * CA_TQ), F32),
                        pltpu.VMEM((2, CA_TQ, CA_NKB * CA_TQ), BF16)],
        compiler_params=_cparams(("arbitrary", "arbitrary")),
        name="chunk_attention",
    )(qkv, qkv, qkv, qkv, qkv, qkv, qkv, rel_row)


def _pad_cols(w, n):
    return jnp.pad(w, ((0, 0), (0, n - w.shape[1])))


def kernel(x, c, ada_w, ada_b, norm_g, ffn_w_gate, ffn_w_up, ffn_w_down, ab_w_in, s5_lam_re, s5_lam_im, s5_log_dt, s5_b_re, s5_b_im, s5_c_re, s5_c_im, s5_d, s5_w_glu, s5_b_glu, dsa_kv_norm_g, dsa_w_kv_up, ab_w_out, c_w_qkv, c_rel_bias, c_w_out, final_norm_g):
    mod = _modulation(c, ada_w, ada_b)
    h = x.reshape(SEQ, D_MODEL)
    wg, wu, wd = ffn_w_gate, ffn_w_up, ffn_w_down

    for layer in range(DEPTH):
        h = _ffn(h, norm_g[layer, 0], mod[layer, 0], wg, wu, wd, layer, 0)
        if layer % 2 == 0:
            e = layer // 2
            w_in = ab_w_in[e]
            o_q, o_kv, o_qi = D_A, D_A + D_B, D_A + D_B + KV_RANK
            o_ki = o_qi + IDX_HEADS * IDX_DIM
            w_in = jnp.concatenate([w_in[:, :o_kv], w_in[:, o_qi:o_ki], w_in[:, o_kv:o_qi], w_in[:, o_ki:]], axis=1)
            proj = _norm_proj(h, norm_g[layer, 1], mod[layer, 1], _pad_cols(w_in, AB_N_PAD), F32, PROJ_TN)

            a_re, a_im, bb_re, bb_im = _s5_params(s5_lam_re[e], s5_lam_im[e], s5_log_dt[e], s5_b_re[e], s5_b_im[e])
            n_blk, ts = SEQ // S5_T, S5_T // S5_NSEG
            u_il = (proj[:, :D_A].reshape(n_blk, S5_NSEG, ts, D_A).transpose(0, 2, 1, 3).reshape(SEQ, D_A))
            y_a = _s5_mixer(u_il, a_re, a_im, bb_re, bb_im, s5_c_re[e], s5_c_im[e], s5_d[e],
                            s5_w_glu[e].astype(BF16), s5_b_glu[e])
            y_a = y_a.reshape(n_blk, ts, S5_NSEG, D_A).transpose(0, 2, 1, 3).reshape(SEQ, D_A)

            k, vt = _kv_up(proj, dsa_kv_norm_g[e], dsa_w_kv_up[e].astype(BF16))
            y_b = _dsa_attention(proj, k, vt)

            w_out = ab_w_out[e].astype(BF16)
            h = _out_proj([y_a, y_b], [w_out[:D_A], w_out[D_A:]], h, mod[layer, 1])
        else:
            o = layer // 2
            qkv = _norm_proj(h, norm_g[layer, 1], mod[layer, 1], c_w_qkv[o], BF16, QKV_TN)
            att = _chunk_attention(qkv, _ca_rel_row(c_rel_bias[o]))
            h = _out_proj([att], [c_w_out[o].astype(BF16)], h, mod[layer, 1])
        h = _ffn(h, norm_g[layer, 2], mod[layer, 2], wg, wu, wd, layer, 1,
                 final_g=final_norm_g if layer == DEPTH - 1 else None)
    return h.reshape(1, SEQ, D_MODEL)
```

```python
import functools
import math

import jax
import jax.numpy as jnp
import numpy as np
from jax import lax
from jax.experimental import pallas as pl
from jax.experimental.pallas import tpu as pltpu

F32 = jnp.float32
BF16 = jnp.bfloat16

D_MODEL = 2048
SEQ = 8192
DEPTH = 2
CHUNK = 64
HEAD_DIM = 128
D_FF = 5504
N_SUB = 3
EPS = 1e-6
D_A = D_MODEL // 2
S5_GROUP = 16
S5_GROUPS = D_A // S5_GROUP
S5_STATE = 64
D_B = D_MODEL // 2
B_HEADS = D_B // HEAD_DIM
KV_RANK = D_MODEL // 8
IDX_HEADS = 16
IDX_DIM = 64
TOPK = 256
C_HEADS = D_MODEL // HEAD_DIM
C_LEFT_CHUNKS = 8
MAX_REL = 256

LANES = 128
SUBLANES = 8
VMEM_LIMIT = 56 * 1024 * 1024
NEG_BIG = -1e30
LOG2E = math.log2(math.e)
F32_MAX = float(np.finfo(np.float32).max)

FFN_TM = 1024
FFN_TF = 256
FFN_VMEM_LIMIT = 60 * 1024 * 1024
PROJ_TM = 1024
PROJ_TN = 512
QKV_TN = 1024
OUT_TM = 512
AB_N_PAD = 3584
S5_T = 256
S5_NSEG = SUBLANES
S5_SLICES = D_A // LANES
S5_NSTATE = S5_GROUPS * S5_STATE
DSA_TQ = 256
DSA_KT = 1024
DSA_SUB = 256
RANK_STEPS = 3.0
CA_TQ = 256
CA_HB = 8
CA_NKB = 3
CA_PERIOD = 1024


def _cparams(sem):
    return pltpu.CompilerParams(dimension_semantics=sem, vmem_limit_bytes=VMEM_LIMIT)


def _mod_kernel(c_ref, w_ref, b_ref, o_ref):
    rows = 256
    tn = o_ref.shape[-1]

    def body(i, acc):
        r0 = pl.multiple_of(i * rows, rows)
        cc = c_ref[pl.ds(r0, rows), :]
        cc = cc * jax.nn.sigmoid(cc)
        w = w_ref[0, pl.ds(r0, rows), :]
        return acc + jnp.sum((w * cc).reshape(rows // SUBLANES, SUBLANES, tn), axis=0)

    acc = lax.fori_loop(0, D_MODEL // rows, body, jnp.zeros((SUBLANES, tn), F32))
    o_ref[0] = jnp.sum(acc, axis=0, keepdims=True) + b_ref[0]


def _modulation(c, ada_w, ada_b):
    n = N_SUB * 3 * D_MODEL
    tn = 1024
    c_col = c.reshape(D_MODEL, 1)
    out = pl.pallas_call(
        _mod_kernel,
        grid=(DEPTH, n // tn),
        in_specs=[
            pl.BlockSpec((D_MODEL, 1), lambda l, j: (0, 0)),
            pl.BlockSpec((1, D_MODEL, tn), lambda l, j: (l, 0, j)),
            pl.BlockSpec((1, 1, tn), lambda l, j: (l, 0, j)),
        ],
        out_specs=pl.BlockSpec((1, 1, tn), lambda l, j: (l, 0, j)),
        out_shape=jax.ShapeDtypeStruct((DEPTH, 1, n), F32),
        compiler_params=_cparams(("arbitrary", "arbitrary")),
        name="adaln_mod",
    )(c_col, ada_w, ada_b.reshape(DEPTH, 1, n))
    return out.reshape(DEPTH, N_SUB, 3, D_MODEL)


ADALN_ROWS = 32


def _adaln_to(hn_ref, h_ref, g_ref, mod_ref):
    g = g_ref[...]
    scale1 = 1.0 + mod_ref[1:2, :]
    shift = mod_ref[0:1, :]

    def body(r, c):
        rows = pl.ds(pl.multiple_of(r * ADALN_ROWS, ADALN_ROWS), ADALN_ROWS)
        x = h_ref[rows, :]
        ms = jnp.mean(x * x, axis=-1, keepdims=True)
        y = (x * lax.rsqrt(ms + EPS)) * g
        hn_ref[rows, :] = (y * scale1 + shift).astype(hn_ref.dtype)
        return c

    lax.fori_loop(0, h_ref.shape[0] // ADALN_ROWS, body, 0, unroll=4)


def _ffn_kernel(h_ref, g_ref, mod_ref, wg_ref, wu_ref, wd_ref, *rest, n_f, final):
    if final:
        fg_ref, o_ref, hn_ref = rest
    else:
        o_ref, hn_ref = rest
    f = pl.program_id(1)

    @pl.when(f == 0)
    def _():
        _adaln_to(hn_ref, h_ref, g_ref, mod_ref)
        o_ref[...] = jnp.zeros_like(o_ref)

    def accumulate(width):
        hn = hn_ref[...]
        gate = jnp.dot(hn, wg_ref[:, :width].astype(BF16), preferred_element_type=F32)
        up = jnp.dot(hn, wu_ref[:, :width].astype(BF16), preferred_element_type=F32)
        act = (gate * jax.nn.sigmoid(gate)) * up
        o_ref[...] += jnp.dot(act.astype(BF16), wd_ref[:width, :].astype(BF16), preferred_element_type=F32)

    @pl.when(f < n_f - 1)
    def _():
        accumulate(FFN_TF)

    @pl.when(f == n_f - 1)
    def _():
        accumulate(D_FF - (n_f - 1) * FFN_TF)
        half_gate = 0.5 * mod_ref[2:3, :]

        def finish(r, c):
            rows = pl.ds(pl.multiple_of(r * ADALN_ROWS, ADALN_ROWS), ADALN_ROWS)
            out = h_ref[rows, :] + half_gate * o_ref[rows, :]
            if final:
                ms = jnp.mean(out * out, axis=-1, keepdims=True)
                out = (out * lax.rsqrt(ms + EPS)) * fg_ref[...]
            o_ref[rows, :] = out
            return c

        lax.fori_loop(0, FFN_TM // ADALN_ROWS, finish, 0, unroll=4)


def _ffn(h, g, mod, wg, wu, wd, layer, which, final_g=None):
    n_f = pl.cdiv(D_FF, FFN_TF)
    final = final_g is not None
    in_specs = [
        pl.BlockSpec((FFN_TM, D_MODEL), lambda i, f: (i, 0)),
        pl.BlockSpec((1, D_MODEL), lambda i, f: (0, 0)),
        pl.BlockSpec((3, D_MODEL), lambda i, f: (0, 0)),
        pl.BlockSpec((None, None, D_MODEL, FFN_TF), lambda i, f: (layer, which, 0, f)),
        pl.BlockSpec((None, None, D_MODEL, FFN_TF), lambda i, f: (layer, which, 0, f)),
        pl.BlockSpec((None, None, FFN_TF, D_MODEL), lambda i, f: (layer, which, f, 0)),
    ]
    args = [h, g.reshape(1, D_MODEL), mod, wg, wu, wd]
    if final:
        in_specs.append(pl.BlockSpec((1, D_MODEL), lambda i, f: (0, 0)))
        args.append(final_g.reshape(1, D_MODEL))
    return pl.pallas_call(
        functools.partial(_ffn_kernel, n_f=n_f, final=final),
        grid=(SEQ // FFN_TM, n_f),
        in_specs=in_specs,
        out_specs=pl.BlockSpec((FFN_TM, D_MODEL), lambda i, f: (i, 0)),
        out_shape=jax.ShapeDtypeStruct((SEQ, D_MODEL), F32),
        scratch_shapes=[pltpu.VMEM((FFN_TM, D_MODEL), BF16)],
        compiler_params=pltpu.CompilerParams(dimension_semantics=("arbitrary", "arbitrary"),
                                             vmem_limit_bytes=FFN_VMEM_LIMIT),
        name="ffn_swiglu",
    )(*args)


def _proj_kernel(h_ref, g_ref, mod_ref, w_ref, o_ref, hn_ref):
    @pl.when(pl.program_id(1) == 0)
    def _():
        _adaln_to(hn_ref, h_ref, g_ref, mod_ref)

    o_ref[...] = jnp.dot(hn_ref[...], w_ref[...].astype(BF16), preferred_element_type=F32).astype(o_ref.dtype)


def _norm_proj(h, g, mod, w, out_dtype, tn):
    n = w.shape[1]
    return pl.pallas_call(
        _proj_kernel,
        grid=(SEQ // PROJ_TM, n // tn),
        in_specs=[
            pl.BlockSpec((PROJ_TM, D_MODEL), lambda i, j: (i, 0)),
            pl.BlockSpec((1, D_MODEL), lambda i, j: (0, 0)),
            pl.BlockSpec((3, D_MODEL), lambda i, j: (0, 0)),
            pl.BlockSpec((D_MODEL, tn), lambda i, j: (0, j)),
        ],
        out_specs=pl.BlockSpec((PROJ_TM, tn), lambda i, j: (i, j)),
        out_shape=jax.ShapeDtypeStruct((SEQ, n), out_dtype),
        scratch_shapes=[pltpu.VMEM((PROJ_TM, D_MODEL), BF16)],
        compiler_params=_cparams(("arbitrary", "arbitrary")),
        name="adaln_proj",
    )(h, g.reshape(1, D_MODEL), mod, w)


def _out_kernel(*refs, n_lhs):
    lhs = refs[:n_lhs]
    ws = refs[n_lhs:2 * n_lhs]
    h_ref, mod_ref, o_ref = refs[2 * n_lhs:]
    y = jnp.dot(lhs[0][...], ws[0][...], preferred_element_type=F32)
    for a_ref, w_ref in zip(lhs[1:], ws[1:]):
        y += jnp.dot(a_ref[...], w_ref[...], preferred_element_type=F32)
    o_ref[...] = h_ref[...] + mod_ref[2:3, :] * y


def _out_proj(lhs, ws, h, mod):
    n_lhs = len(lhs)
    in_specs = [pl.BlockSpec((OUT_TM, a.shape[1]), lambda i: (i, 0)) for a in lhs]
    in_specs += [pl.BlockSpec(w.shape, lambda i: (0, 0)) for w in ws]
    in_specs += [pl.BlockSpec((OUT_TM, D_MODEL), lambda i: (i, 0)), pl.BlockSpec((3, D_MODEL), lambda i: (0, 0))]
    return pl.pallas_call(
        functools.partial(_out_kernel, n_lhs=n_lhs),
        grid=(SEQ // OUT_TM,),
        in_specs=in_specs,
        out_specs=pl.BlockSpec((OUT_TM, D_MODEL), lambda i: (i, 0)),
        out_shape=jax.ShapeDtypeStruct((SEQ, D_MODEL), F32),
        compiler_params=_cparams(("arbitrary",)),
        name="out_proj_residual",
    )(*lhs, *ws, h, mod)


def _s5_param_kernel(lr_ref, li_ref, ldt_ref, br_ref, bi_ref, are_ref, aim_ref, bbr_ref, bbi_ref):
    lr = lr_ref[...]
    li = li_ref[...]
    dt = jnp.exp(ldt_ref[...])
    mag = jnp.exp(lr * dt)
    ab_re = mag * jnp.cos(li * dt)
    ab_im = mag * jnp.sin(li * dt)
    den = lr * lr + li * li
    nr = ab_re - 1.0
    f_re = (nr * lr + ab_im * li) / den
    f_im = (ab_im * lr - nr * li) / den
    are_ref[...] = ab_re
    aim_ref[...] = ab_im
    br = br_ref[...]
    bi = bi_ref[...]
    bbr_ref[...] = f_re * br - f_im * bi
    bbi_ref[...] = f_re * bi + f_im * br


def _s5_params(lam_re, lam_im, log_dt, b_re, b_im):
    G, P, CG = S5_GROUPS, S5_STATE, S5_GROUP
    return pl.pallas_call(
        _s5_param_kernel,
        out_shape=[jax.ShapeDtypeStruct((G, 1, P), F32), jax.ShapeDtypeStruct((G, 1, P), F32),
                   jax.ShapeDtypeStruct((G, CG, P), F32), jax.ShapeDtypeStruct((G, CG, P), F32)],
        name="s5_zoh_params",
    )(lam_re.reshape(G, 1, P), lam_im.reshape(G, 1, P), log_dt.reshape(G, 1, 1),
      jnp.swapaxes(b_re, 1, 2), jnp.swapaxes(b_im, 1, 2))


def _block_diag_slices(m):
    _, r, c = m.shape
    m4 = m.reshape(S5_SLICES, SUBLANES, r, c)
    eye = jnp.eye(SUBLANES, dtype=m.dtype)
    out = m4[:, :, :, None, :] * eye[None, :, None, :, None]
    return out.reshape(S5_SLICES, SUBLANES * r, SUBLANES * c)


def _s5_kernel(u_ref, bbr_ref, bbi_ref, are_ref, aim_ref, ccr_ref, cci_ref, d_ref, wglu_ref, bglu_ref,
               o_ref, xr_ref, xi_ref, pr_ref, pi_ref, er_ref, ei_ref, cr_ref, ci_ref):
    W = S5_NSTATE // S5_SLICES
    TS = S5_T // S5_NSEG
    CW = 512

    @pl.when(pl.program_id(0) == 0)
    def _():
        pr_ref[0:1, :] = are_ref[...]
        pi_ref[0:1, :] = aim_ref[...]

        def power(s, c):
            ar, ai = are_ref[...], aim_ref[...]
            qr, qi = pr_ref[pl.ds(s - 1, 1), :], pi_ref[pl.ds(s - 1, 1), :]
            pr_ref[pl.ds(s, 1), :] = ar * qr - ai * qi
            pi_ref[pl.ds(s, 1), :] = ar * qi + ai * qr
            return c

        lax.fori_loop(1, TS, power, 0)
        cr_ref[...] = jnp.zeros_like(cr_ref)
        ci_ref[...] = jnp.zeros_like(ci_ref)

    for k in range(S5_SLICES):
        uk = u_ref[:, k * LANES:(k + 1) * LANES].astype(BF16)
        xr_ref[:, k * W:(k + 1) * W] = jnp.dot(uk, bbr_ref[k], preferred_element_type=F32)
        xi_ref[:, k * W:(k + 1) * W] = jnp.dot(uk, bbi_ref[k], preferred_element_type=F32)

    for cg in range(S5_NSTATE // CW):
        cols = slice(cg * CW, (cg + 1) * CW)
        ar = jnp.broadcast_to(are_ref[:, cols], (S5_NSEG, CW))
        ai = jnp.broadcast_to(aim_ref[:, cols], (S5_NSEG, CW))

        def step(s, carry, cols=cols, ar=ar, ai=ai):
            sr, si = carry
            rows = pl.ds(pl.multiple_of(s * S5_NSEG, S5_NSEG), S5_NSEG)
            nr = ar * sr - ai * si + xr_ref[rows, cols]
            ni = ar * si + ai * sr + xi_ref[rows, cols]
            xr_ref[rows, cols] = nr
            xi_ref[rows, cols] = ni
            return nr, ni

        zero = jnp.zeros((S5_NSEG, CW), F32)
        er, ei = lax.fori_loop(0, TS, step, (zero, zero), unroll=4)
        er_ref[:, cols] = er
        ei_ref[:, cols] = ei

    pwr, pwi = pr_ref[TS - 1:TS, :], pi_ref[TS - 1:TS, :]
    c_r, c_i = cr_ref[S5_NSEG:S5_NSEG + 1, :], ci_ref[S5_NSEG:S5_NSEG + 1, :]
    for j in range(S5_NSEG):
        cr_ref[j:j + 1, :] = c_r
        ci_ref[j:j + 1, :] = c_i
        c_r, c_i = (er_ref[j:j + 1, :] + pwr * c_r - pwi * c_i,
                    ei_ref[j:j + 1, :] + pwr * c_i + pwi * c_r)
    cr_ref[S5_NSEG:S5_NSEG + 1, :] = c_r
    ci_ref[S5_NSEG:S5_NSEG + 1, :] = c_i

    for cg in range(S5_NSTATE // CW):
        cols = slice(cg * CW, (cg + 1) * CW)
        c_r, c_i = cr_ref[0:S5_NSEG, cols], ci_ref[0:S5_NSEG, cols]

        def fix(s, carry, cols=cols, c_r=c_r, c_i=c_i):
            rows = pl.ds(pl.multiple_of(s * S5_NSEG, S5_NSEG), S5_NSEG)
            p_r, p_i = pr_ref[pl.ds(s, 1), cols], pi_ref[pl.ds(s, 1), cols]
            xr_ref[rows, cols] = xr_ref[rows, cols] + (p_r * c_r - p_i * c_i)
            xi_ref[rows, cols] = xi_ref[rows, cols] + (p_r * c_i + p_i * c_r)
            return carry

        lax.fori_loop(0, TS, fix, 0, unroll=4)

    ys = []
    for k in range(S5_SLICES):
        xr = xr_ref[:, k * W:(k + 1) * W].astype(BF16)
        xi = xi_ref[:, k * W:(k + 1) * W].astype(BF16)
        ys.append(jnp.dot(xr, ccr_ref[k], preferred_element_type=F32)
                  + jnp.dot(xi, cci_ref[k], preferred_element_type=F32))
    y = jnp.concatenate(ys, axis=-1) + d_ref[...] * u_ref[...]
    y = jax.nn.gelu(y, approximate=True)
    z = jnp.dot(y.astype(BF16), wglu_ref[...], preferred_element_type=F32) + bglu_ref[...]
    o_ref[...] = (y * jax.nn.sigmoid(z)).astype(o_ref.dtype)


def _s5_mixer(proj, a_re, a_im, bb_re, bb_im, c_re, c_im, d_skip, w_glu, b_glu):
    W = S5_NSTATE // S5_SLICES
    bbr = _block_diag_slices(bb_re).astype(BF16)
    bbi = _block_diag_slices(bb_im).astype(BF16)
    ccr = _block_diag_slices(jnp.swapaxes(c_re, 1, 2)).astype(BF16)
    cci = _block_diag_slices(-jnp.swapaxes(c_im, 1, 2)).astype(BF16)
    const3 = lambda t: (0, 0, 0)
    const2 = lambda t: (0, 0)
    return pl.pallas_call(
        _s5_kernel,
        grid=(SEQ // S5_T,),
        in_specs=[
            pl.BlockSpec((S5_T, D_A), lambda t: (t, 0)),
            pl.BlockSpec((S5_SLICES, LANES, W), const3),
            pl.BlockSpec((S5_SLICES, LANES, W), const3),
            pl.BlockSpec((1, S5_NSTATE), const2),
            pl.BlockSpec((1, S5_NSTATE), const2),
            pl.BlockSpec((S5_SLICES, W, LANES), const3),
            pl.BlockSpec((S5_SLICES, W, LANES), const3),
            pl.BlockSpec((1, D_A), const2),
            pl.BlockSpec((D_A, D_A), const2),
            pl.BlockSpec((1, D_A), const2),
        ],
        out_specs=pl.BlockSpec((S5_T, D_A), lambda t: (t, 0)),
        out_shape=jax.ShapeDtypeStruct((SEQ, D_A), BF16),
        scratch_shapes=[pltpu.VMEM((S5_T, S5_NSTATE), F32), pltpu.VMEM((S5_T, S5_NSTATE), F32),
                        pltpu.VMEM((S5_T // S5_NSEG, S5_NSTATE), F32), pltpu.VMEM((S5_T // S5_NSEG, S5_NSTATE), F32),
                        pltpu.VMEM((S5_NSEG, S5_NSTATE), F32), pltpu.VMEM((S5_NSEG, S5_NSTATE), F32),
                        pltpu.VMEM((2 * S5_NSEG, S5_NSTATE), F32), pltpu.VMEM((2 * S5_NSEG, S5_NSTATE), F32)],
        compiler_params=_cparams(("arbitrary",)),
        name="s5_mixer",
    )(proj, bbr, bbi, a_re.reshape(1, S5_NSTATE), a_im.reshape(1, S5_NSTATE), ccr, cci,
      d_skip.reshape(1, D_A), w_glu, b_glu.reshape(1, D_A))


def _kv_kernel(lat_ref, g_ref, wk_ref, wvt_ref, k_ref, vt_ref):
    x = lat_ref[...]
    ms = jnp.mean(x * x, axis=-1, keepdims=True)
    xn = ((x * lax.rsqrt(ms + EPS)) * g_ref[...]).astype(BF16)
    k_ref[...] = jnp.dot(xn, wk_ref[...], preferred_element_type=F32).astype(k_ref.dtype)
    vt = lax.dot_general(wvt_ref[...], xn, (((1,), (1,)), ((), ())), preferred_element_type=F32)
    vt_ref[0] = vt.astype(vt_ref.dtype)


def _kv_up(proj, g, w):
    tm = DSA_KT
    lat_block = (2 * D_A + IDX_HEADS * IDX_DIM) // KV_RANK
    wk = w[:, :D_B]
    wvt = w[:, D_B:].T
    return pl.pallas_call(
        _kv_kernel,
        grid=(SEQ // tm,),
        in_specs=[
            pl.BlockSpec((tm, KV_RANK), lambda i: (i, lat_block)),
            pl.BlockSpec((1, KV_RANK), lambda i: (0, 0)),
            pl.BlockSpec((KV_RANK, D_B), lambda i: (0, 0)),
            pl.BlockSpec((D_B, KV_RANK), lambda i: (0, 0)),
        ],
        out_specs=[pl.BlockSpec((tm, D_B), lambda i: (i, 0)),
                   pl.BlockSpec((1, D_B, tm), lambda i: (i, 0, 0))],
        out_shape=[jax.ShapeDtypeStruct((SEQ, D_B), BF16),
                   jax.ShapeDtypeStruct((SEQ // tm, D_B, tm), BF16)],
        compiler_params=_cparams(("arbitrary",)),
        name="dsa_kv_up",
    )(proj, g.reshape(1, KV_RANK), wk, wvt)


def _dsa_kernel(qb_ref, kt_ref, q_ref, qidx_ref, w_ref, kidx_ref, k_ref, vt_ref, o_ref,
                sc_ref, a2_ref, mb_ref, lg_ref, pb_ref, qst_ref, qi_ref, wt_ref, thr_ref, lo_ref, hi_ref, clo_ref,
                m_ref, l_ref, acc_ref):
    TQ, KT, SUB = DSA_TQ, DSA_KT, DSA_SUB
    K = float(TOPK)
    step = pl.program_id(0)
    i = qb_ref[step]
    kt = kt_ref[step]
    last_kt = (i * TQ + TQ - 1) // KT
    n_sub = i + 1
    q_pos = i * TQ + lax.broadcasted_iota(jnp.int32, (1, TQ), 1)
    q_chunk = q_pos // CHUNK

    @pl.when(step == 0)
    def _():
        d = (lax.broadcasted_iota(jnp.int32, (KT, TQ), 1) - lax.broadcasted_iota(jnp.int32, (KT, TQ), 0)).astype(F32)
        for h in range(B_HEADS):
            a2_ref[h] = (LOG2E * 2.0 ** (-8.0 * (h + 1) / B_HEADS)) * d

    @pl.when(kt == 0)
    def _():
        qst_ref[...] = (q_ref[...] * (HEAD_DIM ** -0.5 * LOG2E)).T.astype(BF16)
        qv = qidx_ref[...]
        for h in range(IDX_HEADS):
            qi_ref[h] = qv[:, h * IDX_DIM:(h + 1) * IDX_DIM].astype(BF16)
        wt_ref[...] = w_ref[...].T[IDX_DIM:IDX_DIM + IDX_HEADS, :]

        def scores(g):
            s0 = pl.multiple_of(g * SUB, SUB)
            kk = kidx_ref[pl.ds(s0, SUB), :IDX_DIM].astype(BF16)
            tot = jnp.zeros((SUB, TQ), F32)
            for h in range(IDX_HEADS):
                s = lax.dot_general(kk, qi_ref[h], (((1,), (1,)), ((), ())), preferred_element_type=F32)
                tot = tot + jnp.maximum(s, 0.0) * wt_ref[h:h + 1, :]
            return tot * ((IDX_DIM ** -0.5) * (IDX_HEADS ** -0.5))

        def past_tile(g, carry):
            mx, mn = carry
            tot = scores(g)
            sc_ref[g] = tot
            return (jnp.maximum(mx, jnp.max(tot, axis=0, keepdims=True)),
                    jnp.minimum(mn, jnp.min(tot, axis=0, keepdims=True)))

        def past_group(g4, carry):
            for n in range(4):
                carry = past_tile(4 * g4 + n, carry)
            return carry

        carry = lax.fori_loop(0, i // 4, past_group,
                              (jnp.full((1, TQ), -jnp.inf, F32), jnp.full((1, TQ), jnp.inf, F32)))
        mx, mn = lax.fori_loop(4 * (i // 4), i, past_tile, carry)
        tot = scores(i)
        key_chunk = (i * SUB + lax.broadcasted_iota(jnp.int32, (SUB, 1), 0)) // CHUNK
        adm = key_chunk <= q_chunk
        sc_ref[i] = jnp.where(adm, tot, -jnp.inf)
        mx = jnp.maximum(mx, jnp.max(jnp.where(adm, tot, -jnp.inf), axis=0, keepdims=True))
        mn = jnp.minimum(mn, jnp.min(jnp.where(adm, tot, jnp.inf), axis=0, keepdims=True))

        def fill_tile(g, c):
            sc_ref[g] = jnp.full((SUB, TQ), -jnp.inf, F32)
            return c

        lax.fori_loop(n_sub, (last_kt + 1) * (KT // SUB), fill_tile, 0)

        n_adm = ((q_chunk + 1) * CHUNK).astype(F32)
        keep_all = n_adm <= K
        lo_ref[...] = jnp.where(keep_all, -F32_MAX, mn)
        hi_ref[...] = mx
        clo_ref[...] = jnp.where(keep_all, K, n_adm)

        def count_ge(mid):
            def tile_count(g):
                ind = jnp.where(sc_ref[g] >= mid, 1.0, 0.0)
                return jnp.sum(ind.reshape(SUB // SUBLANES, SUBLANES, TQ), axis=0)

            def body(g2, acc):
                return acc + tile_count(2 * g2) + tile_count(2 * g2 + 1)

            acc = lax.fori_loop(0, n_sub // 2, body, jnp.zeros((SUBLANES, TQ), F32))
            odd = (n_sub % 2).astype(F32)
            acc = acc + odd * tile_count(n_sub - 1)
            return jnp.sum(acc, axis=0, keepdims=True)

        def surplus():
            return jnp.max(clo_ref[...] - K)

        def cond(c):
            it, over = c
            return jnp.logical_and(it < 64, over > RANK_STEPS)

        def body(c):
            it, _ = c
            lo, hi, clo = lo_ref[...], hi_ref[...], clo_ref[...]
            mid = lo + 0.5 * (hi - lo)
            c_mid = count_ge(mid)
            live = clo != K
            ge = jnp.logical_and(live, c_mid >= K)
            lt = jnp.logical_and(live, c_mid < K)
            lo_ref[...] = jnp.where(ge, mid, lo)
            clo_ref[...] = jnp.where(ge, c_mid, clo)
            hi_ref[...] = jnp.where(lt, mid, hi)
            return it + 1, surplus()

        lax.while_loop(cond, body, (jnp.int32(0), surplus()))

        def next_score(lo, strict):
            def tile_min(g):
                t = sc_ref[g]
                keep = (t > lo) if strict else (t >= lo)
                return jnp.min(jnp.where(keep, t, jnp.inf).reshape(SUB // SUBLANES, SUBLANES, TQ), axis=0)

            def pair(g2, acc):
                return jnp.minimum(acc, jnp.minimum(tile_min(2 * g2), tile_min(2 * g2 + 1)))

            acc = lax.fori_loop(0, n_sub // 2, pair, jnp.full((SUBLANES, TQ), jnp.inf, F32))
            acc = jnp.minimum(acc, tile_min(n_sub - 1))
            return jnp.min(acc, axis=0, keepdims=True)

        @pl.when(surplus() > 0.0)
        def _():
            lo, clo = lo_ref[...], clo_ref[...]
            lo_ref[...] = jnp.where(clo != K, next_score(lo, False), lo)

        def step_cond(c):
            it, over = c
            return jnp.logical_and(it < 2 * RANK_STEPS, over > 0.0)

        def step_body(c):
            it, _ = c
            lo, clo = lo_ref[...], clo_ref[...]
            live = clo != K
            lo_ref[...] = jnp.where(live, next_score(lo, True), lo)
            clo_ref[...] = jnp.where(live, clo - 1.0, clo)
            return it + 1, surplus()

        lax.while_loop(step_cond, step_body, (jnp.int32(0), surplus()))
        thr_ref[...] = lo_ref[...]

        m_ref[...] = jnp.full(m_ref.shape, NEG_BIG, F32)
        l_ref[...] = jnp.zeros_like(l_ref)
        acc_ref[...] = jnp.zeros_like(acc_ref)

    def attend(last, nk):
        thr = thr_ref[...]
        for jj in range(nk // SUB):
            mb_ref[jj * SUB:(jj + 1) * SUB, :] = jnp.where(sc_ref[kt * (KT // SUB) + jj] >= thr, 0.0, NEG_BIG)
        gap = (i * TQ - kt * KT).astype(F32)
        m_all, l_all = m_ref[...], l_ref[...]
        m_rows, l_rows = [], []

        def qk(h):
            hs = slice(h * HEAD_DIM, (h + 1) * HEAD_DIM)
            for half in range(2):
                rows = slice(half * (nk // 2), (half + 1) * (nk // 2))
                lg_ref[h % 2, rows, :] = jnp.dot(k_ref[rows, hs], qst_ref[hs, :], preferred_element_type=F32)

        qk(0)
        for h in range(B_HEADS):
            if h + 1 < B_HEADS:
                qk(h + 1)
            hs = slice(h * HEAD_DIM, (h + 1) * HEAD_DIM)
            slope2 = LOG2E * 2.0 ** (-8.0 * (h + 1) / B_HEADS)
            if last:
                lg = (lg_ref[h % 2, :nk, :] - jnp.abs(a2_ref[h, :nk, :] + slope2 * gap)) + mb_ref[:nk, :]
                off = 0.0
            else:
                lg = (lg_ref[h % 2, :nk, :] - a2_ref[h, :nk, :]) + mb_ref[:nk, :]
                off = slope2 * gap
            lg_ref[h % 2, :nk, :] = lg
            m_old = m_all[h:h + 1, :]
            m_new = jnp.maximum(m_old, jnp.max(lg, axis=0, keepdims=True) - off)
            alpha = jnp.exp2(m_old - m_new)
            p = jnp.exp2(lg_ref[h % 2, :nk, :] - (m_new + off))
            l_rows.append(alpha * l_all[h:h + 1, :] + jnp.sum(p, axis=0, keepdims=True))
            m_rows.append(m_new)
            pb_ref[h % 2, :nk, :] = p.astype(BF16)
            pv = jnp.dot(vt_ref[0, hs, :nk], pb_ref[h % 2, :nk, :], preferred_element_type=F32)
            acc_ref[hs, :] = alpha * acc_ref[hs, :] + pv
        m_ref[...] = jnp.concatenate(m_rows, axis=0)
        l_ref[...] = jnp.concatenate(l_rows, axis=0)

    @pl.when(kt < last_kt)
    def _():
        attend(False, KT)

    own_sub = i - last_kt * (KT // SUB)

    @pl.when(jnp.logical_and(kt == last_kt, own_sub < KT // SUB // 2))
    def _():
        attend(True, KT // 2)

    @pl.when(jnp.logical_and(kt == last_kt, own_sub >= KT // SUB // 2))
    def _():
        attend(True, KT)

    @pl.when(kt == last_kt)
    def _():
        for h in range(B_HEADS):
            hs = slice(h * HEAD_DIM, (h + 1) * HEAD_DIM)
            o_ref[:, hs] = (acc_ref[hs, :] / l_ref[h:h + 1, :]).T.astype(o_ref.dtype)


def _dsa_attention(proj, k, vt):
    TQ, KT = DSA_TQ, DSA_KT
    kw_block = (2 * D_A + IDX_HEADS * IDX_DIM + KV_RANK) // LANES
    pairs = [(i, kt) for i in range(SEQ // TQ) for kt in range((i * TQ + TQ - 1) // KT + 1)]
    qb = jnp.asarray([p[0] for p in pairs], jnp.int32)
    ktile = jnp.asarray([p[1] for p in pairs], jnp.int32)

    grid_spec = pltpu.PrefetchScalarGridSpec(
        num_scalar_prefetch=2,
        grid=(len(pairs),),
        in_specs=[
            pl.BlockSpec((TQ, D_B), lambda s, qb, kt: (qb[s], 1)),
            pl.BlockSpec((TQ, IDX_HEADS * IDX_DIM), lambda s, qb, kt: (qb[s], 2)),
            pl.BlockSpec((TQ, LANES), lambda s, qb, kt: (qb[s], kw_block)),
            pl.BlockSpec((SEQ, LANES), lambda s, qb, kt: (0, kw_block)),
            pl.BlockSpec((KT, D_B), lambda s, qb, kt: (kt[s], 0)),
            pl.BlockSpec((1, D_B, KT), lambda s, qb, kt: (kt[s], 0, 0)),
        ],
        out_specs=pl.BlockSpec((TQ, D_B), lambda s, qb, kt: (qb[s], 0)),
        scratch_shapes=[
            pltpu.VMEM((SEQ // DSA_SUB, DSA_SUB, TQ), F32),
            pltpu.VMEM((B_HEADS, KT, TQ), F32),
            pltpu.VMEM((KT, TQ), F32),
            pltpu.VMEM((2, KT, TQ), F32),
            pltpu.VMEM((2, KT, TQ), BF16),
            pltpu.VMEM((D_B, TQ), BF16),
            pltpu.VMEM((IDX_HEADS, TQ, IDX_DIM), BF16),
            pltpu.VMEM((IDX_HEADS, TQ), F32),
            pltpu.VMEM((1, TQ), F32),
            pltpu.VMEM((1, TQ), F32),
            pltpu.VMEM((1, TQ), F32),
            pltpu.VMEM((1, TQ), F32),
            pltpu.VMEM((B_HEADS, TQ), F32),
            pltpu.VMEM((B_HEADS, TQ), F32),
            pltpu.VMEM((D_B, TQ), F32),
        ],
    )
    return pl.pallas_call(
        _dsa_kernel,
        grid_spec=grid_spec,
        out_shape=jax.ShapeDtypeStruct((SEQ, D_B), BF16),
        compiler_params=_cparams(("arbitrary",)),
        name="dsa_attention",
    )(qb, ktile, proj, proj, proj, proj, k, vt)


def _ca_kernel(q_ref, k0_ref, k1_ref, k2_ref, v0_ref, v1_ref, v2_ref, relw_ref, o_ref, bias_ref, lg_ref, pb_ref):
    i = pl.program_id(1)
    k_refs = (k0_ref, k1_ref, k2_ref)
    v_refs = (v0_ref, v1_ref, v2_ref)
    width = CA_NKB * CA_TQ

    @pl.when(i == 0)
    def _():
        r_chunk = lax.broadcasted_iota(jnp.int32, (CA_TQ, width), 0) // CHUNK
        c_chunk = lax.broadcasted_iota(jnp.int32, (CA_TQ, width), 1) // CHUNK
        band = jnp.logical_and(c_chunk >= r_chunk, c_chunk <= r_chunk + C_LEFT_CHUNKS)
        for hh in range(CA_HB):
            row = jnp.broadcast_to(relw_ref[hh], (CA_TQ, CA_PERIOD))
            rolled = pltpu.roll(row, CA_PERIOD - (CA_TQ - 1), 1, stride=1, stride_axis=0)
            bias_ref[hh] = jnp.where(band, rolled[:, :width] * LOG2E, NEG_BIG)

    def qk(hh):
        hs = slice(hh * HEAD_DIM, (hh + 1) * HEAD_DIM)
        qh = q_ref[:, hs]
        for j in range(CA_NKB):
            cols = slice(j * CA_TQ, (j + 1) * CA_TQ)
            lg = lax.dot_general(qh, k_refs[j][:, hs], (((1,), (1,)), ((), ())), preferred_element_type=F32)
            lg = lg * (HEAD_DIM ** -0.5 * LOG2E) + bias_ref[hh, :, cols]
            lg_ref[hh % 2, :, cols] = jnp.where(i + j >= CA_NKB - 1, lg, NEG_BIG)

    qk(0)
    for hh in range(CA_HB):
        if hh + 1 < CA_HB:
            qk(hh + 1)
        hs = slice(hh * HEAD_DIM, (hh + 1) * HEAD_DIM)
        lg = lg_ref[hh % 2]
        m = jnp.max(lg, axis=-1, keepdims=True)
        p = jnp.exp2(lg - m)
        l = jnp.sum(p, axis=-1, keepdims=True)
        pb_ref[hh % 2] = p.astype(BF16)
        acc = jnp.dot(pb_ref[hh % 2, :, :CA_TQ], v_refs[0][:, hs], preferred_element_type=F32)
        for j in range(1, CA_NKB):
            acc += jnp.dot(pb_ref[hh % 2, :, j * CA_TQ:(j + 1) * CA_TQ], v_refs[j][:, hs],
                           preferred_element_type=F32)
        o_ref[:, hs] = (acc / l).astype(o_ref.dtype)


def _ca_rel_row(rel_bias):
    n_heads = rel_bias.shape[0]
    span = CA_NKB * CA_TQ + CA_TQ - 1
    v = jnp.concatenate([rel_bias.astype(F32)[:, MAX_REL - CA_TQ + 1:],
                         jnp.broadcast_to(rel_bias.astype(F32)[:, -1:], (n_heads, span - MAX_REL - CA_TQ))], axis=1)
    return jnp.pad(v[:, ::-1], ((0, 0), (0, CA_PERIOD - span))).reshape(n_heads, 1, CA_PERIOD)


def _chunk_attention(qkv, rel_row):
    nt = SEQ // CA_TQ
    hw = CA_HB * HEAD_DIM
    nhb = C_HEADS // CA_HB

    def kmap(j, base):
        return lambda hb, i: (jnp.maximum(i - (CA_NKB - 1) + j, 0), base + hb)

    in_specs = [pl.BlockSpec((CA_TQ, hw), lambda hb, i: (i, hb))]
    in_specs += [pl.BlockSpec((CA_TQ, hw), kmap(j, nhb)) for j in range(CA_NKB)]
    in_specs += [pl.BlockSpec((CA_TQ, hw), kmap(j, 2 * nhb)) for j in range(CA_NKB)]
    in_specs += [pl.BlockSpec((CA_HB, 1, CA_PERIOD), lambda hb, i: (hb, 0, 0))]
    return pl.pallas_call(
        _ca_kernel,
        grid=(nhb, nt),
        in_specs=in_specs,
        out_specs=pl.BlockSpec((CA_TQ, hw), lambda hb, i: (i, hb)),
        out_shape=jax.ShapeDtypeStruct((SEQ, D_MODEL), BF16),
        scratch_shapes=[pltpu.VMEM((CA_HB, CA_TQ, CA_NKB * CA_TQ), F32),
                        pltpu.VMEM((2, CA_TQ, CA_NKB * CA_TQ), F32),
                        pltpu.VMEM((2, CA_TQ, CA_NKB * CA_TQ), BF16)],
        compiler_params=_cparams(("arbitrary", "arbitrary")),
        name="chunk_attention",
    )(qkv, qkv, qkv, qkv, qkv, qkv, qkv, rel_row)


def _pad_cols(w, n):
    return jnp.pad(w, ((0, 0), (0, n - w.shape[1])))


def kernel(x, c, ada_w, ada_b, norm_g, ffn_w_gate, ffn_w_up, ffn_w_down, ab_w_in, s5_lam_re, s5_lam_im, s5_log_dt, s5_b_re, s5_b_im, s5_c_re, s5_c_im, s5_d, s5_w_glu, s5_b_glu, dsa_kv_norm_g, dsa_w_kv_up, ab_w_out, c_w_qkv, c_rel_bias, c_w_out, final_norm_g):
    mod = _modulation(c, ada_w, ada_b)
    h = x.reshape(SEQ, D_MODEL)
    wg, wu, wd = ffn_w_gate, ffn_w_up, ffn_w_down

    for layer in range(DEPTH):
        h = _ffn(h, norm_g[layer, 0], mod[layer, 0], wg, wu, wd, layer, 0)
        if layer % 2 == 0:
            e = layer // 2
            w_in = ab_w_in[e]
            o_q, o_kv, o_qi = D_A, D_A + D_B, D_A + D_B + KV_RANK
            o_ki = o_qi + IDX_HEADS * IDX_DIM
            w_in = jnp.concatenate([w_in[:, :o_kv], w_in[:, o_qi:o_ki], w_in[:, o_kv:o_qi], w_in[:, o_ki:]], axis=1)
            proj = _norm_proj(h, norm_g[layer, 1], mod[layer, 1], _pad_cols(w_in, AB_N_PAD), F32, PROJ_TN)

            a_re, a_im, bb_re, bb_im = _s5_params(s5_lam_re[e], s5_lam_im[e], s5_log_dt[e], s5_b_re[e], s5_b_im[e])
            n_blk, ts = SEQ // S5_T, S5_T // S5_NSEG
            u_il = (proj[:, :D_A].reshape(n_blk, S5_NSEG, ts, D_A).transpose(0, 2, 1, 3).reshape(SEQ, D_A))
            y_a = _s5_mixer(u_il, a_re, a_im, bb_re, bb_im, s5_c_re[e], s5_c_im[e], s5_d[e],
                            s5_w_glu[e].astype(BF16), s5_b_glu[e])
            y_a = y_a.reshape(n_blk, ts, S5_NSEG, D_A).transpose(0, 2, 1, 3).reshape(SEQ, D_A)

            k, vt = _kv_up(proj, dsa_kv_norm_g[e], dsa_w_kv_up[e].astype(BF16))
            y_b = _dsa_attention(proj, k, vt)

            w_out = ab_w_out[e].astype(BF16)
            h = _out_proj([y_a, y_b], [w_out[:D_A], w_out[D_A:]], h, mod[layer, 1])
        else:
            o = layer // 2
            qkv = _norm_proj(h, norm_g[layer, 1], mod[layer, 1], c_w_qkv[o], BF16, QKV_TN)
            att = _chunk_attention(qkv, _ca_rel_row(c_rel_bias[o]))
            h = _out_proj([att], [c_w_out[o].astype(BF16)], h, mod[layer, 1])
        h = _ffn(h, norm_g[layer, 2], mod[layer, 2], wg, wu, wd, layer, 1,
                 final_g=final_norm_g if layer == DEPTH - 1 else None)
    return h.reshape(1, SEQ, D_MODEL)
```

```python
import functools
import math

import jax
import jax.numpy as jnp
import numpy as np
from jax import lax
from jax.experimental import pallas as pl
from jax.experimental.pallas import tpu as pltpu

F32 = jnp.float32
BF16 = jnp.bfloat16

D_MODEL = 2048
SEQ = 8192
DEPTH = 2
CHUNK = 64
HEAD_DIM = 128
D_FF = 5504
N_SUB = 3
EPS = 1e-6
D_A = D_MODEL // 2
S5_GROUP = 16
S5_GROUPS = D_A // S5_GROUP
S5_STATE = 64
D_B = D_MODEL // 2
B_HEADS = D_B // HEAD_DIM
KV_RANK = D_MODEL // 8
IDX_HEADS = 16
IDX_DIM = 64
TOPK = 256
C_HEADS = D_MODEL // HEAD_DIM
C_LEFT_CHUNKS = 8
MAX_REL = 256

LANES = 128
SUBLANES = 8
VMEM_LIMIT = 56 * 1024 * 1024
NEG_BIG = -1e30
LOG2E = math.log2(math.e)
F32_MAX = float(np.finfo(np.float32).max)

FFN_TM = 1024
FFN_TF = 256
FFN_VMEM_LIMIT = 60 * 1024 * 1024
PROJ_TM = 1024
PROJ_TN = 512
QKV_TN = 1024
OUT_TM = 512
AB_N_PAD = 3584
S5_T = 256
S5_NSEG = SUBLANES
S5_SLICES = D_A // LANES
S5_NSTATE = S5_GROUPS * S5_STATE
DSA_TQ = 256
DSA_KT = 1024
DSA_SUB = 256
RANK_STEPS = 6.0
CA_TQ = 256
CA_HB = 8
CA_NKB = 3
CA_PERIOD = 1024


def _cparams(sem):
    return pltpu.CompilerParams(dimension_semantics=sem, vmem_limit_bytes=VMEM_LIMIT)


def _mod_kernel(c_ref, w_ref, b_ref, o_ref):
    rows = 256
    tn = o_ref.shape[-1]

    def body(i, acc):
        r0 = pl.multiple_of(i * rows, rows)
        cc = c_ref[pl.ds(r0, rows), :]
        cc = cc * jax.nn.sigmoid(cc)
        w = w_ref[0, pl.ds(r0, rows), :]
        return acc + jnp.sum((w * cc).reshape(rows // SUBLANES, SUBLANES, tn), axis=0)

    acc = lax.fori_loop(0, D_MODEL // rows, body, jnp.zeros((SUBLANES, tn), F32))
    o_ref[0] = jnp.sum(acc, axis=0, keepdims=True) + b_ref[0]


def _modulation(c, ada_w, ada_b):
    n = N_SUB * 3 * D_MODEL
    tn = 1024
    c_col = c.reshape(D_MODEL, 1)
    out = pl.pallas_call(
        _mod_kernel,
        grid=(DEPTH, n // tn),
        in_specs=[
            pl.BlockSpec((D_MODEL, 1), lambda l, j: (0, 0)),
            pl.BlockSpec((1, D_MODEL, tn), lambda l, j: (l, 0, j)),
            pl.BlockSpec((1, 1, tn), lambda l, j: (l, 0, j)),
        ],
        out_specs=pl.BlockSpec((1, 1, tn), lambda l, j: (l, 0, j)),
        out_shape=jax.ShapeDtypeStruct((DEPTH, 1, n), F32),
        compiler_params=_cparams(("arbitrary", "arbitrary")),
        name="adaln_mod",
    )(c_col, ada_w, ada_b.reshape(DEPTH, 1, n))
    return out.reshape(DEPTH, N_SUB, 3, D_MODEL)


ADALN_ROWS = 32


def _adaln_to(hn_ref, h_ref, g_ref, mod_ref):
    g = g_ref[...]
    scale1 = 1.0 + mod_ref[1:2, :]
    shift = mod_ref[0:1, :]

    def body(r, c):
        rows = pl.ds(pl.multiple_of(r * ADALN_ROWS, ADALN_ROWS), ADALN_ROWS)
        x = h_ref[rows, :]
        ms = jnp.mean(x * x, axis=-1, keepdims=True)
        y = (x * lax.rsqrt(ms + EPS)) * g
        hn_ref[rows, :] = (y * scale1 + shift).astype(hn_ref.dtype)
        return c

    lax.fori_loop(0, h_ref.shape[0] // ADALN_ROWS, body, 0, unroll=4)


def _ffn_kernel(h_ref, g_ref, mod_ref, wg_ref, wu_ref, wd_ref, *rest, n_f, final):
    if final:
        fg_ref, o_ref, hn_ref = rest
    else:
        o_ref, hn_ref = rest
    f = pl.program_id(1)

    @pl.when(f == 0)
    def _():
        _adaln_to(hn_ref, h_ref, g_ref, mod_ref)
        o_ref[...] = jnp.zeros_like(o_ref)

    def accumulate(width):
        hn = hn_ref[...]
        gate = jnp.dot(hn, wg_ref[:, :width].astype(BF16), preferred_element_type=F32)
        up = jnp.dot(hn, wu_ref[:, :width].astype(BF16), preferred_element_type=F32)
        act = (gate * jax.nn.sigmoid(gate)) * up
        o_ref[...] += jnp.dot(act.astype(BF16), wd_ref[:width, :].astype(BF16), preferred_element_type=F32)

    @pl.when(f < n_f - 1)
    def _():
        accumulate(FFN_TF)

    @pl.when(f == n_f - 1)
    def _():
        accumulate(D_FF - (n_f - 1) * FFN_TF)
        half_gate = 0.5 * mod_ref[2:3, :]

        def finish(r, c):
            rows = pl.ds(pl.multiple_of(r * ADALN_ROWS, ADALN_ROWS), ADALN_ROWS)
            out = h_ref[rows, :] + half_gate * o_ref[rows, :]
            if final:
                ms = jnp.mean(out * out, axis=-1, keepdims=True)
                out = (out * lax.rsqrt(ms + EPS)) * fg_ref[...]
            o_ref[rows, :] = out
            return c

        lax.fori_loop(0, FFN_TM // ADALN_ROWS, finish, 0, unroll=4)


def _ffn(h, g, mod, wg, wu, wd, layer, which, final_g=None):
    n_f = pl.cdiv(D_FF, FFN_TF)
    final = final_g is not None
    in_specs = [
        pl.BlockSpec((FFN_TM, D_MODEL), lambda i, f: (i, 0)),
        pl.BlockSpec((1, D_MODEL), lambda i, f: (0, 0)),
        pl.BlockSpec((3, D_MODEL), lambda i, f: (0, 0)),
        pl.BlockSpec((None, None, D_MODEL, FFN_TF), lambda i, f: (layer, which, 0, f)),
        pl.BlockSpec((None, None, D_MODEL, FFN_TF), lambda i, f: (layer, which, 0, f)),
        pl.BlockSpec((None, None, FFN_TF, D_MODEL), lambda i, f: (layer, which, f, 0)),
    ]
    args = [h, g.reshape(1, D_MODEL), mod, wg, wu, wd]
    if final:
        in_specs.append(pl.BlockSpec((1, D_MODEL), lambda i, f: (0, 0)))
        args.append(final_g.reshape(1, D_MODEL))
    return pl.pallas_call(
        functools.partial(_ffn_kernel, n_f=n_f, final=final),
        grid=(SEQ // FFN_TM, n_f),
        in_specs=in_specs,
        out_specs=pl.BlockSpec((FFN_TM, D_MODEL), lambda i, f: (i, 0)),
        out_shape=jax.ShapeDtypeStruct((SEQ, D_MODEL), F32),
        scratch_shapes=[pltpu.VMEM((FFN_TM, D_MODEL), BF16)],
        compiler_params=pltpu.CompilerParams(dimension_semantics=("arbitrary", "arbitrary"),
                                             vmem_limit_bytes=FFN_VMEM_LIMIT),
        name="ffn_swiglu",
    )(*args)


def _proj_kernel(h_ref, g_ref, mod_ref, w_ref, o_ref, hn_ref):
    @pl.when(pl.program_id(1) == 0)
    def _():
        _adaln_to(hn_ref, h_ref, g_ref, mod_ref)

    o_ref[...] = jnp.dot(hn_ref[...], w_ref[...].astype(BF16), preferred_element_type=F32).astype(o_ref.dtype)


def _norm_proj(h, g, mod, w, out_dtype, tn):
    n = w.shape[1]
    return pl.pallas_call(
        _proj_kernel,
        grid=(SEQ // PROJ_TM, n // tn),
        in_specs=[
            pl.BlockSpec((PROJ_TM, D_MODEL), lambda i, j: (i, 0)),
            pl.BlockSpec((1, D_MODEL), lambda i, j: (0, 0)),
            pl.BlockSpec((3, D_MODEL), lambda i, j: (0, 0)),
            pl.BlockSpec((D_MODEL, tn), lambda i, j: (0, j)),
        ],
        out_specs=pl.BlockSpec((PROJ_TM, tn), lambda i, j: (i, j)),
        out_shape=jax.ShapeDtypeStruct((SEQ, n), out_dtype),
        scratch_shapes=[pltpu.VMEM((PROJ_TM, D_MODEL), BF16)],
        compiler_params=_cparams(("arbitrary", "arbitrary")),
        name="adaln_proj",
    )(h, g.reshape(1, D_MODEL), mod, w)


def _out_kernel(*refs, n_lhs):
    lhs = refs[:n_lhs]
    ws = refs[n_lhs:2 * n_lhs]
    h_ref, mod_ref, o_ref = refs[2 * n_lhs:]
    y = jnp.dot(lhs[0][...], ws[0][...], preferred_element_type=F32)
    for a_ref, w_ref in zip(lhs[1:], ws[1:]):
        y += jnp.dot(a_ref[...], w_ref[...], preferred_element_type=F32)
    o_ref[...] = h_ref[...] + mod_ref[2:3, :] * y


def _out_proj(lhs, ws, h, mod):
    n_lhs = len(lhs)
    in_specs = [pl.BlockSpec((OUT_TM, a.shape[1]), lambda i: (i, 0)) for a in lhs]
    in_specs += [pl.BlockSpec(w.shape, lambda i: (0, 0)) for w in ws]
    in_specs += [pl.BlockSpec((OUT_TM, D_MODEL), lambda i: (i, 0)), pl.BlockSpec((3, D_MODEL), lambda i: (0, 0))]
    return pl.pallas_call(
        functools.partial(_out_kernel, n_lhs=n_lhs),
        grid=(SEQ // OUT_TM,),
        in_specs=in_specs,
        out_specs=pl.BlockSpec((OUT_TM, D_MODEL), lambda i: (i, 0)),
        out_shape=jax.ShapeDtypeStruct((SEQ, D_MODEL), F32),
        compiler_params=_cparams(("arbitrary",)),
        name="out_proj_residual",
    )(*lhs, *ws, h, mod)


def _s5_param_kernel(lr_ref, li_ref, ldt_ref, br_ref, bi_ref, are_ref, aim_ref, bbr_ref, bbi_ref):
    lr = lr_ref[...]
    li = li_ref[...]
    dt = jnp.exp(ldt_ref[...])
    mag = jnp.exp(lr * dt)
    ab_re = mag * jnp.cos(li * dt)
    ab_im = mag * jnp.sin(li * dt)
    den = lr * lr + li * li
    nr = ab_re - 1.0
    f_re = (nr * lr + ab_im * li) / den
    f_im = (ab_im * lr - nr * li) / den
    are_ref[...] = ab_re
    aim_ref[...] = ab_im
    br = br_ref[...]
    bi = bi_ref[...]
    bbr_ref[...] = f_re * br - f_im * bi
    bbi_ref[...] = f_re * bi + f_im * br


def _s5_params(lam_re, lam_im, log_dt, b_re, b_im):
    G, P, CG = S5_GROUPS, S5_STATE, S5_GROUP
    return pl.pallas_call(
        _s5_param_kernel,
        out_shape=[jax.ShapeDtypeStruct((G, 1, P), F32), jax.ShapeDtypeStruct((G, 1, P), F32),
                   jax.ShapeDtypeStruct((G, CG, P), F32), jax.ShapeDtypeStruct((G, CG, P), F32)],
        name="s5_zoh_params",
    )(lam_re.reshape(G, 1, P), lam_im.reshape(G, 1, P), log_dt.reshape(G, 1, 1),
      jnp.swapaxes(b_re, 1, 2), jnp.swapaxes(b_im, 1, 2))


def _block_diag_slices(m):
    _, r, c = m.shape
    m4 = m.reshape(S5_SLICES, SUBLANES, r, c)
    eye = jnp.eye(SUBLANES, dtype=m.dtype)
    out = m4[:, :, :, None, :] * eye[None, :, None, :, None]
    return out.reshape(S5_SLICES, SUBLANES * r, SUBLANES * c)


def _s5_kernel(u_ref, bbr_ref, bbi_ref, are_ref, aim_ref, ccr_ref, cci_ref, d_ref, wglu_ref, bglu_ref,
               o_ref, xr_ref, xi_ref, pr_ref, pi_ref, er_ref, ei_ref, cr_ref, ci_ref):
    W = S5_NSTATE // S5_SLICES
    TS = S5_T // S5_NSEG
    CW = 512

    @pl.when(pl.program_id(0) == 0)
    def _():
        pr_ref[0:1, :] = are_ref[...]
        pi_ref[0:1, :] = aim_ref[...]

        def power(s, c):
            ar, ai = are_ref[...], aim_ref[...]
            qr, qi = pr_ref[pl.ds(s - 1, 1), :], pi_ref[pl.ds(s - 1, 1), :]
            pr_ref[pl.ds(s, 1), :] = ar * qr - ai * qi
            pi_ref[pl.ds(s, 1), :] = ar * qi + ai * qr
            return c

        lax.fori_loop(1, TS, power, 0)
        cr_ref[...] = jnp.zeros_like(cr_ref)
        ci_ref[...] = jnp.zeros_like(ci_ref)

    for k in range(S5_SLICES):
        uk = u_ref[:, k * LANES:(k + 1) * LANES].astype(BF16)
        xr_ref[:, k * W:(k + 1) * W] = jnp.dot(uk, bbr_ref[k], preferred_element_type=F32)
        xi_ref[:, k * W:(k + 1) * W] = jnp.dot(uk, bbi_ref[k], preferred_element_type=F32)

    for cg in range(S5_NSTATE // CW):
        cols = slice(cg * CW, (cg + 1) * CW)
        ar = jnp.broadcast_to(are_ref[:, cols], (S5_NSEG, CW))
        ai = jnp.broadcast_to(aim_ref[:, cols], (S5_NSEG, CW))

        def step(s, carry, cols=cols, ar=ar, ai=ai):
            sr, si = carry
            rows = pl.ds(pl.multiple_of(s * S5_NSEG, S5_NSEG), S5_NSEG)
            nr = ar * sr - ai * si + xr_ref[rows, cols]
            ni = ar * si + ai * sr + xi_ref[rows, cols]
            xr_ref[rows, cols] = nr
            xi_ref[rows, cols] = ni
            return nr, ni

        zero = jnp.zeros((S5_NSEG, CW), F32)
        er, ei = lax.fori_loop(0, TS, step, (zero, zero), unroll=4)
        er_ref[:, cols] = er
        ei_ref[:, cols] = ei

    pwr, pwi = pr_ref[TS - 1:TS, :], pi_ref[TS - 1:TS, :]
    c_r, c_i = cr_ref[S5_NSEG:S5_NSEG + 1, :], ci_ref[S5_NSEG:S5_NSEG + 1, :]
    for j in range(S5_NSEG):
        cr_ref[j:j + 1, :] = c_r
        ci_ref[j:j + 1, :] = c_i
        c_r, c_i = (er_ref[j:j + 1, :] + pwr * c_r - pwi * c_i,
                    ei_ref[j:j + 1, :] + pwr * c_i + pwi * c_r)
    cr_ref[S5_NSEG:S5_NSEG + 1, :] = c_r
    ci_ref[S5_NSEG:S5_NSEG + 1, :] = c_i

    for cg in range(S5_NSTATE // CW):
        cols = slice(cg * CW, (cg + 1) * CW)
        c_r, c_i = cr_ref[0:S5_NSEG, cols], ci_ref[0:S5_NSEG, cols]

        def fix(s, carry, cols=cols, c_r=c_r, c_i=c_i):
            rows = pl.ds(pl.multiple_of(s * S5_NSEG, S5_NSEG), S5_NSEG)
            p_r, p_i = pr_ref[pl.ds(s, 1), cols], pi_ref[pl.ds(s, 1), cols]
            xr_ref[rows, cols] = xr_ref[rows, cols] + (p_r * c_r - p_i * c_i)
            xi_ref[rows, cols] = xi_ref[rows, cols] + (p_r * c_i + p_i * c_r)
            return carry

        lax.fori_loop(0, TS, fix, 0, unroll=4)

    ys = []
    for k in range(S5_SLICES):
        xr = xr_ref[:, k * W:(k + 1) * W].astype(BF16)
        xi = xi_ref[:, k * W:(k + 1) * W].astype(BF16)
        ys.append(jnp.dot(xr, ccr_ref[k], preferred_element_type=F32)
                  + jnp.dot(xi, cci_ref[k], preferred_element_type=F32))
    y = jnp.concatenate(ys, axis=-1) + d_ref[...] * u_ref[...]
    y = jax.nn.gelu(y, approximate=True)
    z = jnp.dot(y.astype(BF16), wglu_ref[...], preferred_element_type=F32) + bglu_ref[...]
    o_ref[...] = (y * jax.nn.sigmoid(z)).astype(o_ref.dtype)


def _s5_mixer(proj, a_re, a_im, bb_re, bb_im, c_re, c_im, d_skip, w_glu, b_glu):
    W = S5_NSTATE // S5_SLICES
    bbr = _block_diag_slices(bb_re).astype(BF16)
    bbi = _block_diag_slices(bb_im).astype(BF16)
    ccr = _block_diag_slices(jnp.swapaxes(c_re, 1, 2)).astype(BF16)
    cci = _block_diag_slices(-jnp.swapaxes(c_im, 1, 2)).astype(BF16)
    const3 = lambda t: (0, 0, 0)
    const2 = lambda t: (0, 0)
    return pl.pallas_call(
        _s5_kernel,
        grid=(SEQ // S5_T,),
        in_specs=[
            pl.BlockSpec((S5_T, D_A), lambda t: (t, 0)),
            pl.BlockSpec((S5_SLICES, LANES, W), const3),
            pl.BlockSpec((S5_SLICES, LANES, W), const3),
            pl.BlockSpec((1, S5_NSTATE), const2),
            pl.BlockSpec((1, S5_NSTATE), const2),
            pl.BlockSpec((S5_SLICES, W, LANES), const3),
            pl.BlockSpec((S5_SLICES, W, LANES), const3),
            pl.BlockSpec((1, D_A), const2),
            pl.BlockSpec((D_A, D_A), const2),
            pl.BlockSpec((1, D_A), const2),
        ],
        out_specs=pl.BlockSpec((S5_T, D_A), lambda t: (t, 0)),
        out_shape=jax.ShapeDtypeStruct((SEQ, D_A), BF16),
        scratch_shapes=[pltpu.VMEM((S5_T, S5_NSTATE), F32), pltpu.VMEM((S5_T, S5_NSTATE), F32),
                        pltpu.VMEM((S5_T // S5_NSEG, S5_NSTATE), F32), pltpu.VMEM((S5_T // S5_NSEG, S5_NSTATE), F32),
                        pltpu.VMEM((S5_NSEG, S5_NSTATE), F32), pltpu.VMEM((S5_NSEG, S5_NSTATE), F32),
                        pltpu.VMEM((2 * S5_NSEG, S5_NSTATE), F32), pltpu.VMEM((2 * S5_NSEG, S5_NSTATE), F32)],
        compiler_params=_cparams(("arbitrary",)),
        name="s5_mixer",
    )(proj, bbr, bbi, a_re.reshape(1, S5_NSTATE), a_im.reshape(1, S5_NSTATE), ccr, cci,
      d_skip.reshape(1, D_A), w_glu, b_glu.reshape(1, D_A))


def _kv_kernel(lat_ref, g_ref, wk_ref, wvt_ref, k_ref, vt_ref):
    x = lat_ref[...]
    ms = jnp.mean(x * x, axis=-1, keepdims=True)
    xn = ((x * lax.rsqrt(ms + EPS)) * g_ref[...]).astype(BF16)
    k_ref[...] = jnp.dot(xn, wk_ref[...], preferred_element_type=F32).astype(k_ref.dtype)
    vt = lax.dot_general(wvt_ref[...], xn, (((1,), (1,)), ((), ())), preferred_element_type=F32)
    vt_ref[0] = vt.astype(vt_ref.dtype)


def _kv_up(proj, g, w):
    tm = DSA_KT
    lat_block = (2 * D_A + IDX_HEADS * IDX_DIM) // KV_RANK
    wk = w[:, :D_B]
    wvt = w[:, D_B:].T
    return pl.pallas_call(
        _kv_kernel,
        grid=(SEQ // tm,),
        in_specs=[
            pl.BlockSpec((tm, KV_RANK), lambda i: (i, lat_block)),
            pl.BlockSpec((1, KV_RANK), lambda i: (0, 0)),
            pl.BlockSpec((KV_RANK, D_B), lambda i: (0, 0)),
            pl.BlockSpec((D_B, KV_RANK), lambda i: (0, 0)),
        ],
        out_specs=[pl.BlockSpec((tm, D_B), lambda i: (i, 0)),
                   pl.BlockSpec((1, D_B, tm), lambda i: (i, 0, 0))],
        out_shape=[jax.ShapeDtypeStruct((SEQ, D_B), BF16),
                   jax.ShapeDtypeStruct((SEQ // tm, D_B, tm), BF16)],
        compiler_params=_cparams(("arbitrary",)),
        name="dsa_kv_up",
    )(proj, g.reshape(1, KV_RANK), wk, wvt)


def _dsa_kernel(qb_ref, kt_ref, q_ref, qidx_ref, w_ref, kidx_ref, k_ref, vt_ref, o_ref,
                sc_ref, a2_ref, mb_ref, lg_ref, pb_ref, qst_ref, qi_ref, wt_ref, thr_ref, lo_ref, hi_ref, clo_ref,
                m_ref, l_ref, acc_ref):
    TQ, KT, SUB = DSA_TQ, DSA_KT, DSA_SUB
    K = float(TOPK)
    step = pl.program_id(0)
    i = qb_ref[step]
    kt = kt_ref[step]
    last_kt = (i * TQ + TQ - 1) // KT
    n_sub = i + 1
    q_pos = i * TQ + lax.broadcasted_iota(jnp.int32, (1, TQ), 1)
    q_chunk = q_pos // CHUNK

    @pl.when(step == 0)
    def _():
        d = (lax.broadcasted_iota(jnp.int32, (KT, TQ), 1) - lax.broadcasted_iota(jnp.int32, (KT, TQ), 0)).astype(F32)
        for h in range(B_HEADS):
            a2_ref[h] = (LOG2E * 2.0 ** (-8.0 * (h + 1) / B_HEADS)) * d

    @pl.when(kt == 0)
    def _():
        qst_ref[...] = (q_ref[...] * (HEAD_DIM ** -0.5 * LOG2E)).T.astype(BF16)
        qv = qidx_ref[...]
        for h in range(IDX_HEADS):
            qi_ref[h] = qv[:, h * IDX_DIM:(h + 1) * IDX_DIM].astype(BF16)
        wt_ref[...] = w_ref[...].T[IDX_DIM:IDX_DIM + IDX_HEADS, :]

        def scores(g):
            s0 = pl.multiple_of(g * SUB, SUB)
            kk = kidx_ref[pl.ds(s0, SUB), :IDX_DIM].astype(BF16)
            tot = jnp.zeros((SUB, TQ), F32)
            for h in range(IDX_HEADS):
                s = lax.dot_general(kk, qi_ref[h], (((1,), (1,)), ((), ())), preferred_element_type=F32)
                tot = tot + jnp.maximum(s, 0.0) * wt_ref[h:h + 1, :]
            return tot * ((IDX_DIM ** -0.5) * (IDX_HEADS ** -0.5))

        def past_tile(g, carry):
            mx, mn = carry
            tot = scores(g)
            sc_ref[g] = tot
            return (jnp.maximum(mx, jnp.max(tot, axis=0, keepdims=True)),
                    jnp.minimum(mn, jnp.min(tot, axis=0, keepdims=True)))

        def past_group(g4, carry):
            for n in range(4):
                carry = past_tile(4 * g4 + n, carry)
            return carry

        carry = lax.fori_loop(0, i // 4, past_group,
                              (jnp.full((1, TQ), -jnp.inf, F32), jnp.full((1, TQ), jnp.inf, F32)))
        mx, mn = lax.fori_loop(4 * (i // 4), i, past_tile, carry)
        tot = scores(i)
        key_chunk = (i * SUB + lax.broadcasted_iota(jnp.int32, (SUB, 1), 0)) // CHUNK
        adm = key_chunk <= q_chunk
        sc_ref[i] = jnp.where(adm, tot, -jnp.inf)
        mx = jnp.maximum(mx, jnp.max(jnp.where(adm, tot, -jnp.inf), axis=0, keepdims=True))
        mn = jnp.minimum(mn, jnp.min(jnp.where(adm, tot, jnp.inf), axis=0, keepdims=True))

        def fill_tile(g, c):
            sc_ref[g] = jnp.full((SUB, TQ), -jnp.inf, F32)
            return c

        lax.fori_loop(n_sub, (last_kt + 1) * (KT // SUB), fill_tile, 0)

        n_adm = ((q_chunk + 1) * CHUNK).astype(F32)
        keep_all = n_adm <= K
        lo_ref[...] = jnp.where(keep_all, -F32_MAX, mn)
        hi_ref[...] = mx
        clo_ref[...] = jnp.where(keep_all, K, n_adm)

        def count_ge(mid):
            def tile_count(g):
                ind = jnp.where(sc_ref[g] >= mid, 1.0, 0.0)
                return jnp.sum(ind.reshape(SUB // SUBLANES, SUBLANES, TQ), axis=0)

            def body(g2, acc):
                return acc + tile_count(2 * g2) + tile_count(2 * g2 + 1)

            acc = lax.fori_loop(0, n_sub // 2, body, jnp.zeros((SUBLANES, TQ), F32))
            odd = (n_sub % 2).astype(F32)
            acc = acc + odd * tile_count(n_sub - 1)
            return jnp.sum(acc, axis=0, keepdims=True)

        def surplus():
            return jnp.max(clo_ref[...] - K)

        def cond(c):
            it, over = c
            return jnp.logical_and(it < 64, over > RANK_STEPS)

        def body(c):
            it, _ = c
            lo, hi, clo = lo_ref[...], hi_ref[...], clo_ref[...]
            mid = lo + 0.5 * (hi - lo)
            c_mid = count_ge(mid)
            live = clo != K
            ge = jnp.logical_and(live, c_mid >= K)
            lt = jnp.logical_and(live, c_mid < K)
            lo_ref[...] = jnp.where(ge, mid, lo)
            clo_ref[...] = jnp.where(ge, c_mid, clo)
            hi_ref[...] = jnp.where(lt, mid, hi)
            return it + 1, surplus()

        lax.while_loop(cond, body, (jnp.int32(0), surplus()))

        def next_score(lo, strict):
            def tile_min(g):
                t = sc_ref[g]
                keep = (t > lo) if strict else (t >= lo)
                return jnp.min(jnp.where(keep, t, jnp.inf).reshape(SUB // SUBLANES, SUBLANES, TQ), axis=0)

            def pair(g2, acc):
                return jnp.minimum(acc, jnp.minimum(tile_min(2 * g2), tile_min(2 * g2 + 1)))

            acc = lax.fori_loop(0, n_sub // 2, pair, jnp.full((SUBLANES, TQ), jnp.inf, F32))
            acc = jnp.minimum(acc, tile_min(n_sub - 1))
            return jnp.min(acc, axis=0, keepdims=True)

        @pl.when(surplus() > 0.0)
        def _():
            lo, clo = lo_ref[...], clo_ref[...]
            lo_ref[...] = jnp.where(clo != K, next_score(lo, False), lo)

        def step_cond(c):
            it, over = c
            return jnp.logical_and(it < 2 * RANK_STEPS, over > 0.0)

        def step_body(c):
            it, _ = c
            lo, clo = lo_ref[...], clo_ref[...]
            live = clo != K
            lo_ref[...] = jnp.where(live, next_score(lo, True), lo)
            clo_ref[...] = jnp.where(live, clo - 1.0, clo)
            return it + 1, surplus()

        lax.while_loop(step_cond, step_body, (jnp.int32(0), surplus()))
        thr_ref[...] = lo_ref[...]

        m_ref[...] = jnp.full(m_ref.shape, NEG_BIG, F32)
        l_ref[...] = jnp.zeros_like(l_ref)
        acc_ref[...] = jnp.zeros_like(acc_ref)

    def attend(last, nk):
        thr = thr_ref[...]
        for jj in range(nk // SUB):
            mb_ref[jj * SUB:(jj + 1) * SUB, :] = jnp.where(sc_ref[kt * (KT // SUB) + jj] >= thr, 0.0, NEG_BIG)
        gap = (i * TQ - kt * KT).astype(F32)
        m_all, l_all = m_ref[...], l_ref[...]
        m_rows, l_rows = [], []

        def qk(h):
            hs = slice(h * HEAD_DIM, (h + 1) * HEAD_DIM)
            for half in range(2):
                rows = slice(half * (nk // 2), (half + 1) * (nk // 2))
                lg_ref[h % 2, rows, :] = jnp.dot(k_ref[rows, hs], qst_ref[hs, :], preferred_element_type=F32)

        qk(0)
        for h in range(B_HEADS):
            if h + 1 < B_HEADS:
                qk(h + 1)
            hs = slice(h * HEAD_DIM, (h + 1) * HEAD_DIM)
            slope2 = LOG2E * 2.0 ** (-8.0 * (h + 1) / B_HEADS)
            if last:
                lg = (lg_ref[h % 2, :nk, :] - jnp.abs(a2_ref[h, :nk, :] + slope2 * gap)) + mb_ref[:nk, :]
                off = 0.0
            else:
                lg = (lg_ref[h % 2, :nk, :] - a2_ref[h, :nk, :]) + mb_ref[:nk, :]
                off = slope2 * gap
            lg_ref[h % 2, :nk, :] = lg
            m_old = m_all[h:h + 1, :]
            m_new = jnp.maximum(m_old, jnp.max(lg, axis=0, keepdims=True) - off)
            alpha = jnp.exp2(m_old - m_new)
            p = jnp.exp2(lg_ref[h % 2, :nk, :] - (m_new + off))
            l_rows.append(alpha * l_all[h:h + 1, :] + jnp.sum(p, axis=0, keepdims=True))
            m_rows.append(m_new)
            pb_ref[h % 2, :nk, :] = p.astype(BF16)
            pv = jnp.dot(vt_ref[0, hs, :nk], pb_ref[h % 2, :nk, :], preferred_element_type=F32)
            acc_ref[hs, :] = alpha * acc_ref[hs, :] + pv
        m_ref[...] = jnp.concatenate(m_rows, axis=0)
        l_ref[...] = jnp.concatenate(l_rows, axis=0)

    @pl.when(kt < last_kt)
    def _():
        attend(False, KT)

    own_sub = i - last_kt * (KT // SUB)

    @pl.when(jnp.logical_and(kt == last_kt, own_sub < KT // SUB // 2))
    def _():
        attend(True, KT // 2)

    @pl.when(jnp.logical_and(kt == last_kt, own_sub >= KT // SUB // 2))
    def _():
        attend(True, KT)

    @pl.when(kt == last_kt)
    def _():
        for h in range(B_HEADS):
            hs = slice(h * HEAD_DIM, (h + 1) * HEAD_DIM)
            o_ref[:, hs] = (acc_ref[hs, :] / l_ref[h:h + 1, :]).T.astype(o_ref.dtype)


def _dsa_attention(proj, k, vt):
    TQ, KT = DSA_TQ, DSA_KT
    kw_block = (2 * D_A + IDX_HEADS * IDX_DIM + KV_RANK) // LANES
    pairs = [(i, kt) for i in range(SEQ // TQ) for kt in range((i * TQ + TQ - 1) // KT + 1)]
    qb = jnp.asarray([p[0] for p in pairs], jnp.int32)
    ktile = jnp.asarray([p[1] for p in pairs], jnp.int32)

    grid_spec = pltpu.PrefetchScalarGridSpec(
        num_scalar_prefetch=2,
        grid=(len(pairs),),
        in_specs=[
            pl.BlockSpec((TQ, D_B), lambda s, qb, kt: (qb[s], 1)),
            pl.BlockSpec((TQ, IDX_HEADS * IDX_DIM), lambda s, qb, kt: (qb[s], 2)),
            pl.BlockSpec((TQ, LANES), lambda s, qb, kt: (qb[s], kw_block)),
            pl.BlockSpec((SEQ, LANES), lambda s, qb, kt: (0, kw_block)),
            pl.BlockSpec((KT, D_B), lambda s, qb, kt: (kt[s], 0)),
            pl.BlockSpec((1, D_B, KT), lambda s, qb, kt: (kt[s], 0, 0)),
        ],
        out_specs=pl.BlockSpec((TQ, D_B), lambda s, qb, kt: (qb[s], 0)),
        scratch_shapes=[
            pltpu.VMEM((SEQ // DSA_SUB, DSA_SUB, TQ), F32),
            pltpu.VMEM((B_HEADS, KT, TQ), F32),
            pltpu.VMEM((KT, TQ), F32),
            pltpu.VMEM((2, KT, TQ), F32),
            pltpu.VMEM((2, KT, TQ), BF16),
            pltpu.VMEM((D_B, TQ), BF16),
            pltpu.VMEM((IDX_HEADS, TQ, IDX_DIM), BF16),
            pltpu.VMEM((IDX_HEADS, TQ), F32),
            pltpu.VMEM((1, TQ), F32),
            pltpu.VMEM((1, TQ), F32),
            pltpu.VMEM((1, TQ), F32),
            pltpu.VMEM((1, TQ), F32),
            pltpu.VMEM((B_HEADS, TQ), F32),
            pltpu.VMEM((B_HEADS, TQ), F32),
            pltpu.VMEM((D_B, TQ), F32),
        ],
    )
    return pl.pallas_call(
        _dsa_kernel,
        grid_spec=grid_spec,
        out_shape=jax.ShapeDtypeStruct((SEQ, D_B), BF16),
        compiler_params=_cparams(("arbitrary",)),
        name="dsa_attention",
    )(qb, ktile, proj, proj, proj, proj, k, vt)


def _ca_kernel(q_ref, k0_ref, k1_ref, k2_ref, v0_ref, v1_ref, v2_ref, relw_ref, o_ref, bias_ref, lg_ref, pb_ref):
    i = pl.program_id(1)
    k_refs = (k0_ref, k1_ref, k2_ref)
    v_refs = (v0_ref, v1_ref, v2_ref)
    width = CA_NKB * CA_TQ

    @pl.when(i == 0)
    def _():
        r_chunk = lax.broadcasted_iota(jnp.int32, (CA_TQ, width), 0) // CHUNK
        c_chunk = lax.broadcasted_iota(jnp.int32, (CA_TQ, width), 1) // CHUNK
        band = jnp.logical_and(c_chunk >= r_chunk, c_chunk <= r_chunk + C_LEFT_CHUNKS)
        for hh in range(CA_HB):
            row = jnp.broadcast_to(relw_ref[hh], (CA_TQ, CA_PERIOD))
            rolled = pltpu.roll(row, CA_PERIOD - (CA_TQ - 1), 1, stride=1, stride_axis=0)
            bias_ref[hh] = jnp.where(band, rolled[:, :width] * LOG2E, NEG_BIG)

    def qk(hh):
        hs = slice(hh * HEAD_DIM, (hh + 1) * HEAD_DIM)
        qh = q_ref[:, hs]
        for j in range(CA_NKB):
            cols = slice(j * CA_TQ, (j + 1) * CA_TQ)
            lg = lax.dot_general(qh, k_refs[j][:, hs], (((1,), (1,)), ((), ())), preferred_element_type=F32)
            lg = lg * (HEAD_DIM ** -0.5 * LOG2E) + bias_ref[hh, :, cols]
            lg_ref[hh % 2, :, cols] = jnp.where(i + j >= CA_NKB - 1, lg, NEG_BIG)

    qk(0)
    for hh in range(CA_HB):
        if hh + 1 < CA_HB:
            qk(hh + 1)
        hs = slice(hh * HEAD_DIM, (hh + 1) * HEAD_DIM)
        lg = lg_ref[hh % 2]
        m = jnp.max(lg, axis=-1, keepdims=True)
        p = jnp.exp2(lg - m)
        l = jnp.sum(p, axis=-1, keepdims=True)
        pb_ref[hh % 2] = p.astype(BF16)
        acc = jnp.dot(pb_ref[hh % 2, :, :CA_TQ], v_refs[0][:, hs], preferred_element_type=F32)
        for j in range(1, CA_NKB):
            acc += jnp.dot(pb_ref[hh % 2, :, j * CA_TQ:(j + 1) * CA_TQ], v_refs[j][:, hs],
                           preferred_element_type=F32)
        o_ref[:, hs] = (acc / l).astype(o_ref.dtype)


def _ca_rel_row(rel_bias):
    n_heads = rel_bias.shape[0]
    span = CA_NKB * CA_TQ + CA_TQ - 1
    v = jnp.concatenate([rel_bias.astype(F32)[:, MAX_REL - CA_TQ + 1:],
                         jnp.broadcast_to(rel_bias.astype(F32)[:, -1:], (n_heads, span - MAX_REL - CA_TQ))], axis=1)
    return jnp.pad(v[:, ::-1], ((0, 0), (0, CA_PERIOD - span))).reshape(n_heads, 1, CA_PERIOD)


def _chunk_attention(qkv, rel_row):
    nt = SEQ // CA_TQ
    hw = CA_HB * HEAD_DIM
    nhb = C_HEADS // CA_HB

    def kmap(j, base):
        return lambda hb, i: (jnp.maximum(i - (CA_NKB - 1) + j, 0), base + hb)

    in_specs = [pl.BlockSpec((CA_TQ, hw), lambda hb, i: (i, hb))]
    in_specs += [pl.BlockSpec((CA_TQ, hw), kmap(j, nhb)) for j in range(CA_NKB)]
    in_specs += [pl.BlockSpec((CA_TQ, hw), kmap(j, 2 * nhb)) for j in range(CA_NKB)]
    in_specs += [pl.BlockSpec((CA_HB, 1, CA_PERIOD), lambda hb, i: (hb, 0, 0))]
    return pl.pallas_call(
        _ca_kernel,
        grid=(nhb, nt),
        in_specs=in_specs,
        out_specs=pl.BlockSpec((CA_TQ, hw), lambda hb, i: (i, hb)),
        out_shape=jax.ShapeDtypeStruct((SEQ, D_MODEL), BF16),
        scratch_shapes=[pltpu.VMEM((CA_HB, CA_TQ, CA_NKB * CA_TQ), F32),
                        pltpu.VMEM((2, CA_TQ, CA_NKB * CA_TQ), F32),
                        pltpu.VMEM((2, CA_TQ, CA_NKB * CA_TQ), BF16)],
        compiler_params=_cparams(("arbitrary", "arbitrary")),
        name="chunk_attention",
    )(qkv, qkv, qkv, qkv, qkv, qkv, qkv, rel_row)


def _pad_cols(w, n):
    return jnp.pad(w, ((0, 0), (0, n - w.shape[1])))


def kernel(x, c, ada_w, ada_b, norm_g, ffn_w_gate, ffn_w_up, ffn_w_down, ab_w_in, s5_lam_re, s5_lam_im, s5_log_dt, s5_b_re, s5_b_im, s5_c_re, s5_c_im, s5_d, s5_w_glu, s5_b_glu, dsa_kv_norm_g, dsa_w_kv_up, ab_w_out, c_w_qkv, c_rel_bias, c_w_out, final_norm_g):
    mod = _modulation(c, ada_w, ada_b)
    h = x.reshape(SEQ, D_MODEL)
    wg, wu, wd = ffn_w_gate, ffn_w_up, ffn_w_down

    for layer in range(DEPTH):
        h = _ffn(h, norm_g[layer, 0], mod[layer, 0], wg, wu, wd, layer, 0)
        if layer % 2 == 0:
            e = layer // 2
            w_in = ab_w_in[e]
            o_q, o_kv, o_qi = D_A, D_A + D_B, D_A + D_B + KV_RANK
            o_ki = o_qi + IDX_HEADS * IDX_DIM
            w_in = jnp.concatenate([w_in[:, :o_kv], w_in[:, o_qi:o_ki], w_in[:, o_kv:o_qi], w_in[:, o_ki:]], axis=1)
            proj = _norm_proj(h, norm_g[layer, 1], mod[layer, 1], _pad_cols(w_in, AB_N_PAD), F32, PROJ_TN)

            a_re, a_im, bb_re, bb_im = _s5_params(s5_lam_re[e], s5_lam_im[e], s5_log_dt[e], s5_b_re[e], s5_b_im[e])
            n_blk, ts = SEQ // S5_T, S5_T // S5_NSEG
            u_il = (proj[:, :D_A].reshape(n_blk, S5_NSEG, ts, D_A).transpose(0, 2, 1, 3).reshape(SEQ, D_A))
            y_a = _s5_mixer(u_il, a_re, a_im, bb_re, bb_im, s5_c_re[e], s5_c_im[e], s5_d[e],
                            s5_w_glu[e].astype(BF16), s5_b_glu[e])
            y_a = y_a.reshape(n_blk, ts, S5_NSEG, D_A).transpose(0, 2, 1, 3).reshape(SEQ, D_A)

            k, vt = _kv_up(proj, dsa_kv_norm_g[e], dsa_w_kv_up[e].astype(BF16))
            y_b = _dsa_attention(proj, k, vt)

            w_out = ab_w_out[e].astype(BF16)
            h = _out_proj([y_a, y_b], [w_out[:D_A], w_out[D_A:]], h, mod[layer, 1])
        else:
            o = layer // 2
            qkv = _norm_proj(h, norm_g[layer, 1], mod[layer, 1], c_w_qkv[o], BF16, QKV_TN)
            att = _chunk_attention(qkv, _ca_rel_row(c_rel_bias[o]))
            h = _out_proj([att], [c_w_out[o].astype(BF16)], h, mod[layer, 1])
        h = _ffn(h, norm_g[layer, 2], mod[layer, 2], wg, wu, wd, layer, 1,
                 final_g=final_norm_g if layer == DEPTH - 1 else None)
    return h.reshape(1, SEQ, D_MODEL)
```
